```python
import math
import jax, jax.numpy as jnp
from jax import lax
import numpy as np

D_MODEL = 2048
BATCH = 4
SEQ = 2048
DEPTH = 2

N_A_LAYERS = DEPTH // 2
N_B_LAYERS = DEPTH - N_A_LAYERS
N_SUBLAYERS = 3
FFN_RESIDUAL_WEIGHT = 0.5
D_FF = 5632
RMS_EPS = 1e-6
ADA_SCALE = 0.1

S5_GROUP = 16
N_S5_GROUPS = D_MODEL // S5_GROUP
S5_STATE = 64
DT_MIN = 1e-3
DT_MAX = 1e-1

HEAD_DIM = 64
N_Q_HEADS = D_MODEL // HEAD_DIM
N_KV_HEADS = N_Q_HEADS // 8
Q_PER_KV = N_Q_HEADS // N_KV_HEADS
WINDOW = 128
ATTN_BLOCK = 128
ROPE_THETA = 10000.0

kernel_name = "yoco_s5_swa_sink_macaron_adaln"


def _rmsnorm(x, g):
    xf = x.astype(jnp.float32)
    xf = xf * lax.rsqrt(jnp.mean(xf * xf, axis=-1, keepdims=True) + RMS_EPS)
    return (xf * g.astype(jnp.float32)).astype(x.dtype)


def _modulate(h, shift, scale):
    return h * (1.0 + scale[:, None, :]) + shift[:, None, :]


def _swiglu(h, w_in, w_out):
    gate, up = jnp.split(h @ w_in, 2, axis=-1)
    return (jax.nn.silu(gate) * up) @ w_out


def _rope_tables(positions):
    inv_freq = 1.0 / (ROPE_THETA ** (jnp.arange(0, HEAD_DIM, 2, dtype=jnp.float32) / HEAD_DIM))
    ang = positions.astype(jnp.float32)[..., None] * inv_freq
    return jnp.cos(ang)[:, :, None, :], jnp.sin(ang)[:, :, None, :]


def _apply_rope(t, cos, sin):
    tf = t.astype(jnp.float32)
    t1, t2 = jnp.split(tf, 2, axis=-1)
    return jnp.concatenate([t1 * cos - t2 * sin, t2 * cos + t1 * sin], axis=-1).astype(t.dtype)


def _s5_mixer(u, w_in, a_re, a_im, b_re, b_im, c_re, c_im, d_skip, log_dt, w_glu, b_glu, w_out):
    Bsz, L, D = u.shape
    v = (u @ w_in).reshape(Bsz, L, N_S5_GROUPS, S5_GROUP).astype(jnp.float32)
    lam = lax.complex(a_re.astype(jnp.float32), a_im.astype(jnp.float32))
    dt = jnp.exp(log_dt.astype(jnp.float32))[:, None]
    lam_bar = jnp.exp(lam * dt)
    b_mat = lax.complex(b_re.astype(jnp.float32), b_im.astype(jnp.float32))
    b_bar = ((lam_bar - 1.0) / lam)[..., None] * b_mat
    c_mat = lax.complex(c_re.astype(jnp.float32), c_im.astype(jnp.float32))
    bu = jnp.einsum('blgh,gph->blgp', v.astype(jnp.complex64), b_bar)
    a_elems = jnp.broadcast_to(lam_bar, (1, L) + lam_bar.shape)

    def combine(left, right):
        a_l, b_l = left
        a_r, b_r = right
        return a_r * a_l, a_r * b_l + b_r

    _, states = lax.associative_scan(combine, (a_elems, bu), axis=1)
    y = jnp.einsum('blgp,ghp->blgh', states, c_mat).real + d_skip.astype(jnp.float32) * v
    y = jax.nn.gelu(y.reshape(Bsz, L, D)).astype(u.dtype)
    y = y * jax.nn.sigmoid(y @ w_glu + b_glu)
    return y @ w_out


def _banded(t):
    Bsz, L = t.shape[:2]
    nb = L // ATTN_BLOCK
    cur = t.reshape(Bsz, nb, ATTN_BLOCK, t.shape[2], t.shape[3])
    prev = jnp.pad(cur[:, :-1], ((0, 0), (1, 0), (0, 0), (0, 0), (0, 0)))
    return jnp.concatenate([prev, cur], axis=2)


def _band_mask(nb):
    q_pos = jnp.arange(ATTN_BLOCK)[:, None] + ATTN_BLOCK
    k_pos = jnp.arange(2 * ATTN_BLOCK)[None, :]
    diff = q_pos - k_pos
    in_window = (diff >= 0) & (diff < WINDOW)
    k_abs = jnp.arange(nb)[:, None] * ATTN_BLOCK - ATTN_BLOCK + k_pos
    return in_window[None] & (k_abs >= 0)[:, None, :]


def _shared_kv(h, c_act, kv_norm_g, w_ada_kv, b_ada_kv, w_kv, cos, sin):
    Bsz, L, _ = h.shape
    shift, scale = jnp.split(c_act @ w_ada_kv + b_ada_kv, 2, axis=-1)
    hn = _modulate(_rmsnorm(h, kv_norm_g), shift, scale)
    k, v = jnp.split(hn @ w_kv, 2, axis=-1)
    k = _apply_rope(k.reshape(Bsz, L, N_KV_HEADS, HEAD_DIM), cos, sin)
    v = v.reshape(Bsz, L, N_KV_HEADS, HEAD_DIM)
    return _banded(k), _banded(v)


def _swa_sink_attention(h, k_band, v_band, mask, cos, sin, w_q, sinks, w_o):
    Bsz, L, _ = h.shape
    nb = L // ATTN_BLOCK
    q = _apply_rope((h @ w_q).reshape(Bsz, L, N_Q_HEADS, HEAD_DIM), cos, sin)
    q = q.reshape(Bsz, nb, ATTN_BLOCK, N_KV_HEADS, Q_PER_KV, HEAD_DIM)
    s = jnp.einsum('bnqhgd,bnkhd->bnhgqk', q, k_band,
                   preferred_element_type=jnp.float32) * (HEAD_DIM ** -0.5)
    s = jnp.where(mask[None, :, None, None], s, -jnp.inf)
    sink = jnp.broadcast_to(sinks.astype(jnp.float32).reshape(1, 1, N_KV_HEADS, Q_PER_KV, 1, 1),
                            s.shape[:-1] + (1,))
    p = jax.nn.softmax(jnp.concatenate([s, sink], axis=-1), axis=-1)[..., :-1]
    o = jnp.einsum('bnhgqk,bnkhd->bnqhgd', p.astype(v_band.dtype), v_band)
    return o.reshape(Bsz, L, N_Q_HEADS * HEAD_DIM) @ w_o


def setup_inputs(seed: int = 0) -> dict:
    key = jax.random.key(seed)
    ks = jax.random.split(key, 32)
    f32 = jnp.float32
    D, F, G, P, H = D_MODEL, D_FF, N_S5_GROUPS, S5_STATE, S5_GROUP
    kvw = N_KV_HEADS * HEAD_DIM
    qw = N_Q_HEADS * HEAD_DIM

    def nrm(k, shape, std):
        return jax.random.normal(k, shape, f32) * std

    x = jax.random.normal(ks[0], (BATCH, SEQ, D), f32)
    c = jax.random.normal(ks[1], (BATCH, D), f32)
    offsets = jax.random.randint(ks[2], (BATCH, 1), 0, 4096, dtype=jnp.int32)
    positions = offsets + jnp.arange(SEQ, dtype=jnp.int32)[None, :]

    norm_g = 1.0 + nrm(ks[3], (DEPTH, N_SUBLAYERS, D), 0.02)
    w_ada = nrm(ks[4], (DEPTH, D, N_SUBLAYERS * 3 * D), ADA_SCALE * D ** -0.5)
    b_ada = nrm(ks[5], (DEPTH, N_SUBLAYERS * 3 * D), 0.01)
    w_ff_in = nrm(ks[6], (DEPTH, 2, D, 2 * F), D ** -0.5)
    w_ff_out = nrm(ks[7], (DEPTH, 2, F, D), F ** -0.5)

    s5_w_in = nrm(ks[8], (N_A_LAYERS, D, D), D ** -0.5)
    n_idx = jnp.arange(P, dtype=f32)
    s5_a_re = -0.5 + nrm(ks[9], (N_A_LAYERS, G, P), 0.01)
    s5_a_im = math.pi * n_idx + nrm(ks[10], (N_A_LAYERS, G, P), 0.01)
    s5_b_re = nrm(ks[11], (N_A_LAYERS, G, P, H), H ** -0.5)
    s5_b_im = nrm(ks[12], (N_A_LAYERS, G, P, H), H ** -0.5)
    s5_c_re = nrm(ks[13], (N_A_LAYERS, G, H, P), P ** -0.5)
    s5_c_im = nrm(ks[14], (N_A_LAYERS, G, H, P), P ** -0.5)
    s5_d = nrm(ks[15], (N_A_LAYERS, G, H), 1.0)
    s5_log_dt = jax.random.uniform(ks[16], (N_A_LAYERS, G), f32,
                                   math.log(DT_MIN), math.log(DT_MAX))
    s5_w_glu = nrm(ks[17], (N_A_LAYERS, D, D), D ** -0.5)
    s5_b_glu = nrm(ks[18], (N_A_LAYERS, D), 0.01)
    s5_w_out = nrm(ks[19], (N_A_LAYERS, D, D), D ** -0.5)

    kv_norm_g = 1.0 + nrm(ks[20], (D,), 0.02)
    w_ada_kv = nrm(ks[21], (D, 2 * D), ADA_SCALE * D ** -0.5)
    b_ada_kv = nrm(ks[22], (2 * D,), 0.01)
    w_kv = nrm(ks[23], (D, 2 * kvw), D ** -0.5)

    attn_w_q = nrm(ks[24], (N_B_LAYERS, D, qw), D ** -0.5)
    attn_sinks = nrm(ks[25], (N_B_LAYERS, N_Q_HEADS), 1.0)
    attn_w_o = nrm(ks[26], (N_B_LAYERS, qw, D), qw ** -0.5)

    final_norm_g = 1.0 + nrm(ks[27], (D,), 0.02)

    return {"x": x, "c": c, "positions": positions,
            "norm_g": norm_g, "w_ada": w_ada, "b_ada": b_ada,
            "w_ff_in": w_ff_in, "w_ff_out": w_ff_out,
            "s5_w_in": s5_w_in, "s5_a_re": s5_a_re, "s5_a_im": s5_a_im,
            "s5_b_re": s5_b_re, "s5_b_im": s5_b_im, "s5_c_re": s5_c_re, "s5_c_im": s5_c_im,
            "s5_d": s5_d, "s5_log_dt": s5_log_dt, "s5_w_glu": s5_w_glu, "s5_b_glu": s5_b_glu,
            "s5_w_out": s5_w_out,
            "kv_norm_g": kv_norm_g, "w_ada_kv": w_ada_kv, "b_ada_kv": b_ada_kv, "w_kv": w_kv,
            "attn_w_q": attn_w_q, "attn_sinks": attn_sinks, "attn_w_o": attn_w_o,
            "final_norm_g": final_norm_g}


def reference(x, c, positions, norm_g, w_ada, b_ada, w_ff_in, w_ff_out,
              s5_w_in, s5_a_re, s5_a_im, s5_b_re, s5_b_im, s5_c_re, s5_c_im,
              s5_d, s5_log_dt, s5_w_glu, s5_b_glu, s5_w_out,
              kv_norm_g, w_ada_kv, b_ada_kv, w_kv,
              attn_w_q, attn_sinks, attn_w_o, final_norm_g):
    Bsz, L, D = x.shape
    c_act = jax.nn.silu(c)
    cos, sin = _rope_tables(positions)
    mask = _band_mask(L // ATTN_BLOCK)
    k_band = v_band = None
    for layer in range(DEPTH):
        mod = (c_act @ w_ada[layer] + b_ada[layer]).reshape(Bsz, N_SUBLAYERS, 3, D)
        shift, scale, gate = mod[:, :, 0], mod[:, :, 1], mod[:, :, 2]
        g = norm_g[layer]
        h = _modulate(_rmsnorm(x, g[0]), shift[:, 0], scale[:, 0])
        x = x + FFN_RESIDUAL_WEIGHT * (1.0 + gate[:, 0, None, :]) * _swiglu(
            h, w_ff_in[layer, 0], w_ff_out[layer, 0])
        h = _modulate(_rmsnorm(x, g[1]), shift[:, 1], scale[:, 1])
        if layer < N_A_LAYERS:
            i = layer
            y = _s5_mixer(h, s5_w_in[i], s5_a_re[i], s5_a_im[i], s5_b_re[i], s5_b_im[i],
                          s5_c_re[i], s5_c_im[i], s5_d[i], s5_log_dt[i],
                          s5_w_glu[i], s5_b_glu[i], s5_w_out[i])
        else:
            j = layer - N_A_LAYERS
            y = _swa_sink_attention(h, k_band, v_band, mask, cos, sin,
                                    attn_w_q[j], attn_sinks[j], attn_w_o[j])
        x = x + (1.0 + gate[:, 1, None, :]) * y
        h = _modulate(_rmsnorm(x, g[2]), shift[:, 2], scale[:, 2])
        x = x + FFN_RESIDUAL_WEIGHT * (1.0 + gate[:, 2, None, :]) * _swiglu(
            h, w_ff_in[layer, 1], w_ff_out[layer, 1])
        if layer == N_A_LAYERS - 1:
            k_band, v_band = _shared_kv(x, c_act, kv_norm_g, w_ada_kv, b_ada_kv, w_kv, cos, sin)
    return _rmsnorm(x, final_norm_g)
```

```python
import functools
import math

import numpy as np
import jax
import jax.numpy as jnp
from jax import lax
from jax.experimental import pallas as pl
from jax.experimental.pallas import tpu as pltpu

F32 = jnp.float32
BF16 = jnp.bfloat16

RMS_EPS = 1e-6
N_SUBLAYERS = 3
S5_GROUP = 16
S5_STATE = 64
HEAD_DIM = 64
Q_PER_KV = 8
WINDOW = 128
ROPE_THETA = 10000.0

LANES = 128
SUBLANES = 8
S5_TILE = 256
VMEM_LIMIT = 52 * 1024 * 1024


def _params(sem):
    return pltpu.CompilerParams(dimension_semantics=sem, vmem_limit_bytes=VMEM_LIMIT)


def _sigmoid(x):
    return 1.0 / (1.0 + jnp.exp(-x))


def _gelu_tanh(x):
    return 0.5 * x * (1.0 + jnp.tanh(math.sqrt(2.0 / math.pi) * (x + 0.044715 * (x * x * x))))


def _ada_kernel(c_ref, w_ref, b_ref, o_ref):
    c = c_ref[...]
    ca = (c * _sigmoid(c)).astype(BF16)
    o_ref[...] = jnp.dot(ca, w_ref[...].astype(BF16), preferred_element_type=F32) + b_ref[...]


def _ada(c_pad, w, b, bn=1024):
    s, d, n = w.shape
    return pl.pallas_call(
        _ada_kernel,
        grid=(s, n // bn),
        in_specs=[pl.BlockSpec((SUBLANES, d), lambda i, j: (0, 0)),
                  pl.BlockSpec((None, d, bn), lambda i, j: (i, 0, j)),
                  pl.BlockSpec((None, 1, bn), lambda i, j: (i, 0, j))],
        out_specs=pl.BlockSpec((None, SUBLANES, bn), lambda i, j: (i, 0, j)),
        out_shape=jax.ShapeDtypeStruct((s, SUBLANES, n), F32),
        compiler_params=_params(("parallel", "parallel")),
        name="ada",
    )(c_pad, w, b.reshape(s, 1, n))


def _norm_kernel(x_ref, g_ref, *rest, modulate):
    if modulate:
        sh_ref, sc_ref, o_ref = rest
    else:
        (o_ref,) = rest
    x = x_ref[...]
    ms = jnp.mean(x * x, axis=-1, keepdims=True)
    y = x * lax.rsqrt(ms + RMS_EPS) * g_ref[...]
    if modulate:
        y = y * (1.0 + sc_ref[...]) + sh_ref[...]
    o_ref[...] = y.astype(o_ref.dtype)


def _norm(x, g, shift=None, scale=None, *, rows_per_batch, out_dtype, bl=512):
    m, d = x.shape
    per = rows_per_batch // bl
    modulate = shift is not None
    in_specs = [pl.BlockSpec((bl, d), lambda i: (i, 0)),
                pl.BlockSpec((1, d), lambda i: (0, 0))]
    args = [x, g.reshape(1, d)]
    if modulate:
        nb = shift.shape[0]
        in_specs += [pl.BlockSpec((None, 1, d), lambda i: (i // per, 0, 0))] * 2
        args += [shift.reshape(nb, 1, d), scale.reshape(nb, 1, d)]
    return pl.pallas_call(
        functools.partial(_norm_kernel, modulate=modulate),
        grid=(m // bl,),
        in_specs=in_specs,
        out_specs=pl.BlockSpec((bl, d), lambda i: (i, 0)),
        out_shape=jax.ShapeDtypeStruct((m, d), out_dtype),
        compiler_params=_params(("parallel",)),
        name="norm",
    )(*args)


def _mm_plain_kernel(x_ref, w_ref, o_ref):
    o_ref[...] = jnp.dot(x_ref[...], w_ref[...], preferred_element_type=F32).astype(o_ref.dtype)


def _mm_swiglu_kernel(x_ref, wg_ref, wu_ref, o_ref):
    x = x_ref[...]
    g = jnp.dot(x, wg_ref[...], preferred_element_type=F32)
    u = jnp.dot(x, wu_ref[...], preferred_element_type=F32)
    o_ref[...] = (g * _sigmoid(g) * u).astype(o_ref.dtype)


def _mm_glu_kernel(x_ref, w_ref, y_ref, b_ref, o_ref):
    acc = jnp.dot(x_ref[...], w_ref[...], preferred_element_type=F32)
    o_ref[...] = (y_ref[...].astype(F32) * _sigmoid(acc + b_ref[...])).astype(o_ref.dtype)


def _mm_rope_kernel(x_ref, w_ref, cos_ref, sin_ref, o_ref, *, out_scale):
    acc = jnp.dot(x_ref[...], w_ref[...], preferred_element_type=F32)
    cos = cos_ref[...]
    sin = sin_ref[...]
    for j in range(acc.shape[1] // LANES):
        a = acc[:, j * LANES:(j + 1) * LANES]
        r = a * cos + pltpu.roll(a, LANES // 2, axis=1) * sin
        o_ref[:, j * LANES:(j + 1) * LANES] = (r * out_scale).astype(o_ref.dtype)


def _mm_resid_kernel(x_ref, w_ref, r_ref, gate_ref, o_ref, *acc_refs, weight, nk):
    part = jnp.dot(x_ref[...], w_ref[...], preferred_element_type=F32)
    coef = weight * (1.0 + gate_ref[...])
    if nk == 1:
        o_ref[...] = r_ref[...] + coef * part
        return
    (acc_ref,) = acc_refs
    k = pl.program_id(2)

    @pl.when(k == 0)
    def _():
        acc_ref[...] = part

    @pl.when(k > 0)
    def _():
        acc_ref[...] += part

    @pl.when(k == nk - 1)
    def _():
        o_ref[...] = r_ref[...] + coef * acc_ref[...]


def _mm_plain(x, w, *, out_dtype, bm=1024, bn=1024):
    m, k = x.shape
    n = w.shape[1]
    bn = min(bn, n)
    return pl.pallas_call(
        _mm_plain_kernel,
        grid=(m // bm, n // bn),
        in_specs=[pl.BlockSpec((bm, k), lambda i, j: (i, 0)),
                  pl.BlockSpec((k, bn), lambda i, j: (0, j))],
        out_specs=pl.BlockSpec((bm, bn), lambda i, j: (i, j)),
        out_shape=jax.ShapeDtypeStruct((m, n), out_dtype),
        compiler_params=_params(("parallel", "parallel")),
        name="mm_plain",
    )(x, w)


def _mm_swiglu(x, w_in, *, bm=1024, bn=512):
    m, k = x.shape
    f = w_in.shape[1] // 2
    nf = f // bn
    return pl.pallas_call(
        _mm_swiglu_kernel,
        grid=(m // bm, nf),
        in_specs=[pl.BlockSpec((bm, k), lambda i, j: (i, 0)),
                  pl.BlockSpec((k, bn), lambda i, j: (0, j)),
                  pl.BlockSpec((k, bn), lambda i, j: (0, j + nf))],
        out_specs=pl.BlockSpec((bm, bn), lambda i, j: (i, j)),
        out_shape=jax.ShapeDtypeStruct((m, f), BF16),
        compiler_params=_params(("parallel", "parallel")),
        name="mm_swiglu",
    )(x, w_in, w_in)


def _mm_glu(y, w, b, *, bm=1024, bn=1024):
    m, k = y.shape
    n = w.shape[1]
    return pl.pallas_call(
        _mm_glu_kernel,
        grid=(m // bm, n // bn),
        in_specs=[pl.BlockSpec((bm, k), lambda i, j: (i, 0)),
                  pl.BlockSpec((k, bn), lambda i, j: (0, j)),
                  pl.BlockSpec((bm, bn), lambda i, j: (i, j)),
                  pl.BlockSpec((1, bn), lambda i, j: (0, j))],
        out_specs=pl.BlockSpec((bm, bn), lambda i, j: (i, j)),
        out_shape=jax.ShapeDtypeStruct((m, n), BF16),
        compiler_params=_params(("parallel", "parallel")),
        name="mm_glu",
    )(y, w, y, b.reshape(1, n))


def _mm_rope(x, w, cos, sin, *, out_scale, out_dtype, bm=1024, bn=1024):
    m, k = x.shape
    n = w.shape[1]
    bn = min(bn, n)
    return pl.pallas_call(
        functools.partial(_mm_rope_kernel, out_scale=out_scale),
        grid=(m // bm, n // bn),
        in_specs=[pl.BlockSpec((bm, k), lambda i, j: (i, 0)),
                  pl.BlockSpec((k, bn), lambda i, j: (0, j)),
                  pl.BlockSpec((bm, LANES), lambda i, j: (i, 0)),
                  pl.BlockSpec((bm, LANES), lambda i, j: (i, 0))],
        out_specs=pl.BlockSpec((bm, bn), lambda i, j: (i, j)),
        out_shape=jax.ShapeDtypeStruct((m, n), out_dtype),
        compiler_params=_params(("parallel", "parallel")),
        name="mm_rope",
    )(x, w, cos, sin)


def _mm_resid(x, w, resid, gate, *, weight, rows_per_batch, bm=1024, bn=1024, bk=None):
    m, k = x.shape
    n = w.shape[1]
    bk = k if bk is None else bk
    nk = k // bk
    per = rows_per_batch // bm
    nb = gate.shape[0]
    return pl.pallas_call(
        functools.partial(_mm_resid_kernel, weight=weight, nk=nk),
        grid=(m // bm, n // bn, nk),
        in_specs=[pl.BlockSpec((bm, bk), lambda i, j, kk: (i, kk)),
                  pl.BlockSpec((bk, bn), lambda i, j, kk: (kk, j)),
                  pl.BlockSpec((bm, bn), lambda i, j, kk: (i, j)),
                  pl.BlockSpec((None, 1, bn), lambda i, j, kk: (i // per, 0, j))],
        out_specs=pl.BlockSpec((bm, bn), lambda i, j, kk: (i, j)),
        out_shape=jax.ShapeDtypeStruct((m, n), F32),
        scratch_shapes=[pltpu.VMEM((bm, bn), F32)] if nk > 1 else [],
        compiler_params=_params(("parallel", "parallel", "arbitrary")),
        name="mm_resid",
    )(x, w, resid, gate.reshape(nb, 1, n))


def _rope_kernel(pos_ref, invf_ref, sign_ref, cos_ref, sin_ref):
    ang = pos_ref[...].astype(F32) * invf_ref[...]
    cos_ref[...] = jnp.cos(ang)
    sin_ref[...] = jnp.sin(ang) * sign_ref[...]


def _rope_tables(positions, bl=1024):
    m = positions.size
    half = HEAD_DIM // 2
    inv_freq = 1.0 / (ROPE_THETA ** (jnp.arange(0, HEAD_DIM, 2, dtype=F32) / HEAD_DIM))
    invf = jnp.tile(inv_freq, LANES // half).reshape(1, LANES)
    sign = jnp.asarray(np.where(np.arange(LANES) < LANES // 2, -1.0, 1.0), F32).reshape(1, LANES)
    return pl.pallas_call(
        _rope_kernel,
        grid=(m // bl,),
        in_specs=[pl.BlockSpec((bl, 1), lambda i: (i, 0)),
                  pl.BlockSpec((1, LANES), lambda i: (0, 0)),
                  pl.BlockSpec((1, LANES), lambda i: (0, 0))],
        out_specs=[pl.BlockSpec((bl, LANES), lambda i: (i, 0))] * 2,
        out_shape=[jax.ShapeDtypeStruct((m, LANES), F32)] * 2,
        compiler_params=_params(("parallel",)),
        name="rope_tables",
    )(positions.reshape(m, 1), invf, sign)


def _pair_perm(n_heads):
    half = HEAD_DIM // 2
    cols = np.arange(n_heads * HEAD_DIM)
    pair, within = cols // LANES, cols % LANES
    slot, i = within // half, within % half
    head = 2 * pair + slot % 2
    return head * HEAD_DIM + (slot // 2) * half + i


def _s5_kernel(v_ref, bblk_ref, cblk_ref, a_ref, d_ref, y_ref, bu_ref, st_ref, x_ref, *, tc, pitch, nb):
    half_w = 4 * LANES
    rows = nb * tc

    @pl.when(pl.program_id(1) == 0)
    def _():
        x_ref[...] = jnp.zeros_like(x_ref)

    v2 = v_ref[...].reshape(rows, S5_TILE)
    vb = v2.astype(BF16)
    for half in range(2):
        r = jnp.dot(vb, bblk_ref[:, half * 2 * half_w:(half + 1) * 2 * half_w],
                    preferred_element_type=F32)
        for b in range(nb):
            for l in range(8):
                bu_ref[l, pl.ds((half * nb + b) * pitch, tc), :] = (
                    r[b * tc:(b + 1) * tc, l * LANES:(l + 1) * LANES])

    a = [a_ref[l] for l in range(8)]

    def step(t, xs):
        new_r, new_i = [], []
        for l in range(4):
            bur = bu_ref[l, pl.ds(t, SUBLANES, stride=pitch), :]
            bui = bu_ref[4 + l, pl.ds(t, SUBLANES, stride=pitch), :]
            xr, xi = xs[l], xs[4 + l]
            nr = a[l] * xr - a[4 + l] * xi + bur
            ni = a[l] * xi + a[4 + l] * xr + bui
            st_ref[l, pl.ds(t, SUBLANES, stride=pitch), :] = nr
            st_ref[4 + l, pl.ds(t, SUBLANES, stride=pitch), :] = ni
            new_r.append(nr)
            new_i.append(ni)
        return tuple(new_r + new_i)

    xs = lax.fori_loop(0, tc, step, tuple(x_ref[l] for l in range(8)), unroll=8)
    for l in range(8):
        x_ref[l] = xs[l]

    acc = None
    for half in range(2):
        blocks = []
        for b in range(nb):
            blocks.append(jnp.concatenate(
                [st_ref[l, pl.ds((half * nb + b) * pitch, tc), :] for l in range(8)], axis=1))
        s = jnp.concatenate(blocks, axis=0).astype(BF16)
        part = jnp.dot(s, cblk_ref[half * 2 * half_w:(half + 1) * 2 * half_w, :],
                       preferred_element_type=F32)
        acc = part if acc is None else acc + part
    y = _gelu_tanh(acc + d_ref[...] * v2)
    y_ref[...] = y.reshape(nb, tc, S5_TILE).astype(y_ref.dtype)


def _s5_discretise(a_re, a_im, b_re, b_im, c_re, c_im, log_dt, nb):
    g, p = a_re.shape
    gpt = S5_TILE // S5_GROUP
    nt = g // gpt
    dt = jnp.exp(log_dt)[:, None]
    mag = jnp.exp(a_re * dt)
    lb_re, lb_im = mag * jnp.cos(a_im * dt), mag * jnp.sin(a_im * dt)
    den = a_re * a_re + a_im * a_im
    k_re = ((lb_re - 1.0) * a_re + lb_im * a_im) / den
    k_im = (lb_im * a_re - (lb_re - 1.0) * a_im) / den
    bb_re = k_re[..., None] * b_re - k_im[..., None] * b_im
    bb_im = k_re[..., None] * b_im + k_im[..., None] * b_re
    eye = jnp.eye(gpt, dtype=F32)

    def in_blk(bb):
        t = bb.reshape(nt, gpt, p, S5_GROUP).transpose(0, 1, 3, 2)
        return t[:, :, :, None, :] * eye[None, :, None, :, None]

    def out_blk(cc):
        t = cc.reshape(nt, gpt, S5_GROUP, p).transpose(0, 1, 3, 2)
        return t[:, :, :, None, :] * eye[None, :, None, :, None]

    bre = in_blk(bb_re).reshape(nt, S5_TILE, 2, 1, gpt // 2 * p)
    bim = in_blk(bb_im).reshape(nt, S5_TILE, 2, 1, gpt // 2 * p)
    bblk = jnp.concatenate([bre, bim], axis=3).reshape(nt, S5_TILE, 2 * gpt * p)
    cre = out_blk(c_re).reshape(nt, 2, 1, gpt // 2 * p, S5_TILE)
    cim = out_blk(-c_im).reshape(nt, 2, 1, gpt // 2 * p, S5_TILE)
    cblk = jnp.concatenate([cre, cim], axis=2).reshape(nt, 2 * gpt * p, S5_TILE)

    def a_slabs(lb):
        t = lb.reshape(nt, 2, 4, LANES)
        t = jnp.broadcast_to(t[:, :, None], (nt, 2, nb, 4, LANES))
        return t.transpose(0, 3, 1, 2, 4).reshape(nt, 4, 2 * nb, LANES)

    a = jnp.concatenate([a_slabs(lb_re), a_slabs(lb_im)], axis=1)
    return bblk.astype(BF16), cblk.astype(BF16), a


def _s5_core(v, bblk, cblk, a, d_skip, *, tc=128):
    nb, seq, d = v.shape
    assert 2 * nb == SUBLANES
    nt = d // S5_TILE
    pitch = tc + 4
    nstate = bblk.shape[2]
    return pl.pallas_call(
        functools.partial(_s5_kernel, tc=tc, pitch=pitch, nb=nb),
        grid=(nt, seq // tc),
        in_specs=[pl.BlockSpec((nb, tc, S5_TILE), lambda j, c: (0, c, j)),
                  pl.BlockSpec((None, S5_TILE, nstate), lambda j, c: (j, 0, 0)),
                  pl.BlockSpec((None, nstate, S5_TILE), lambda j, c: (j, 0, 0)),
                  pl.BlockSpec((None, 8, SUBLANES, LANES), lambda j, c: (j, 0, 0, 0)),
                  pl.BlockSpec((1, S5_TILE), lambda j, c: (0, j))],
        out_specs=pl.BlockSpec((nb, tc, S5_TILE), lambda j, c: (0, c, j)),
        out_shape=jax.ShapeDtypeStruct((nb, seq, d), BF16),
        scratch_shapes=[pltpu.VMEM((8, SUBLANES * pitch, LANES), F32),
                        pltpu.VMEM((8, SUBLANES * pitch, LANES), F32),
                        pltpu.VMEM((8, SUBLANES, LANES), F32)],
        compiler_params=_params(("parallel", "arbitrary")),
        name="s5_core",
    )(v, bblk, cblk, a, d_skip.reshape(1, d))


def _attn_kernel(sink_ref, q_ref, kp_ref, kc_ref, vp_ref, vc_ref, o_ref, *, n_kv):
    blk = q_ref.shape[0]
    lo = jnp.where(pl.program_id(1) == 0, blk, 0)
    qi = lax.broadcasted_iota(jnp.int32, (blk, 2 * blk), 0)
    kj = lax.broadcasted_iota(jnp.int32, (blk, 2 * blk), 1)
    valid = (kj > qi) & (kj <= qi + blk) & (kj >= lo)
    lane = lax.broadcasted_iota(jnp.int32, (2 * blk, LANES), 1)
    half = HEAD_DIM // 2
    pairs = Q_PER_KV // 2

    for kvh in range(n_kv):
        grp, slot = kvh // 2, kvh % 2
        kcat = jnp.concatenate([kp_ref[:, grp * LANES:(grp + 1) * LANES],
                                kc_ref[:, grp * LANES:(grp + 1) * LANES]], axis=0)
        vcat = jnp.concatenate([vp_ref[:, grp * LANES:(grp + 1) * LANES],
                                vc_ref[:, grp * LANES:(grp + 1) * LANES]], axis=0)
        kexp, vexp = [], []
        for s in range(2):
            kk = kcat if s == slot else pltpu.roll(kcat, (half * (s - slot)) % LANES, axis=1)
            kexp.append(jnp.where((lane // half) % 2 == s, kk, 0.0).astype(BF16))
            vv = vcat if s == slot else pltpu.roll(vcat, (HEAD_DIM * (s - slot)) % LANES, axis=1)
            vexp.append(jnp.where(lane // HEAD_DIM == s, vv, 0.0).astype(BF16))
        base = kvh * pairs
        qg = jnp.concatenate([q_ref[:, (base + p) * LANES:(base + p + 1) * LANES]
                              for p in range(pairs)], axis=0)
        out = None
        for s in range(2):
            sc = lax.dot_general(qg, kexp[s], (((1,), (1,)), ((), ())),
                                 preferred_element_type=F32)
            probs = []
            for p in range(pairs):
                sink = sink_ref[kvh * Q_PER_KV + 2 * p + s]
                sp = jnp.where(valid, sc[p * blk:(p + 1) * blk], -1e30)
                mx = jnp.maximum(jnp.max(sp, axis=-1, keepdims=True), sink)
                e = jnp.exp(sp - mx)
                den = jnp.sum(e, axis=-1, keepdims=True) + jnp.exp(sink - mx)
                probs.append((e / den).astype(BF16))
            pm = jnp.concatenate(probs, axis=0)
            part = jnp.dot(pm, vexp[s], preferred_element_type=F32)
            out = part if out is None else out + part
        for p in range(pairs):
            o_ref[:, (base + p) * LANES:(base + p + 1) * LANES] = (
                out[p * blk:(p + 1) * blk].astype(o_ref.dtype))


def _attention(q, k, v, sinks, *, nbatch, blk=WINDOW):
    m, dq = q.shape
    dkv = k.shape[1]
    nblk = m // nbatch // blk
    cur = lambda b, n: (b * nblk + n, 0)
    prev = lambda b, n: (b * nblk + jnp.maximum(n - 1, 0), 0)
    return pl.pallas_call(
        functools.partial(_attn_kernel, n_kv=dkv // HEAD_DIM),
        grid=(nbatch, nblk),
        in_specs=[pl.BlockSpec(memory_space=pltpu.SMEM),
                  pl.BlockSpec((blk, dq), cur),
                  pl.BlockSpec((blk, dkv), prev),
                  pl.BlockSpec((blk, dkv), cur),
                  pl.BlockSpec((blk, dkv), prev),
                  pl.BlockSpec((blk, dkv), cur)],
        out_specs=pl.BlockSpec((blk, dq), cur),
        out_shape=jax.ShapeDtypeStruct((m, dq), BF16),
        compiler_params=_params(("parallel", "arbitrary")),
        name="attention",
    )(sinks, q, k, k, v, v)


def kernel(x, c, positions, norm_g, w_ada, b_ada, w_ff_in, w_ff_out, s5_w_in, s5_a_re, s5_a_im, s5_b_re, s5_b_im, s5_c_re, s5_c_im, s5_d, s5_log_dt, s5_w_glu, s5_b_glu, s5_w_out, kv_norm_g, w_ada_kv, b_ada_kv, w_kv, attn_w_q, attn_sinks, attn_w_o, final_norm_g):
    nbatch, seq, d = x.shape
    depth = norm_g.shape[0]
    n_s5 = s5_w_in.shape[0]
    m = nbatch * seq
    kvw = w_kv.shape[1] // 2

    c_pad = jnp.zeros((SUBLANES, d), F32).at[:nbatch].set(c)
    mods = _ada(c_pad, w_ada, b_ada)[:, :nbatch]
    mod_kv = _ada(c_pad, w_ada_kv[None], b_ada_kv[None])[0, :nbatch]
    cos, sin = _rope_tables(positions)

    norm = functools.partial(_norm, rows_per_batch=seq)
    resid = functools.partial(_mm_resid, rows_per_batch=seq)

    def ffn(xs, layer, which, g, shift, scale, gate):
        h = norm(xs, g, shift, scale, out_dtype=BF16)
        act = _mm_swiglu(h, w_ff_in[layer, which].astype(BF16))
        return resid(act, w_ff_out[layer, which].astype(BF16), xs, gate, weight=0.5,
                     bk=w_ff_out.shape[2] // 2)

    xs = x.reshape(m, d)
    k_rot = v_kv = None
    for layer in range(depth):
        mod = mods[layer].reshape(nbatch, N_SUBLAYERS, 3, d)
        shift, scale, gate = mod[:, :, 0], mod[:, :, 1], mod[:, :, 2]
        g = norm_g[layer]
        xs = ffn(xs, layer, 0, g[0], shift[:, 0], scale[:, 0], gate[:, 0])

        h = norm(xs, g[1], shift[:, 1], scale[:, 1], out_dtype=BF16)
        if layer < n_s5:
            i = layer
            v = _mm_plain(h, s5_w_in[i].astype(BF16), out_dtype=F32)
            bblk, cblk, a = _s5_discretise(s5_a_re[i], s5_a_im[i], s5_b_re[i], s5_b_im[i],
                                           s5_c_re[i], s5_c_im[i], s5_log_dt[i], nbatch)
            y = _s5_core(v.reshape(nbatch, seq, d), bblk, cblk, a, s5_d[i].reshape(-1)).reshape(m, d)
            z = _mm_glu(y, s5_w_glu[i].astype(BF16), s5_b_glu[i])
            xs = resid(z, s5_w_out[i].astype(BF16), xs, gate[:, 1], weight=1.0)
        else:
            j = layer - n_s5
            wq = attn_w_q[j][:, _pair_perm(attn_w_q.shape[2] // HEAD_DIM)].astype(BF16)
            q = _mm_rope(h, wq, cos, sin, out_scale=HEAD_DIM ** -0.5, out_dtype=BF16)
            o = _attention(q, k_rot, v_kv, attn_sinks[j], nbatch=nbatch)
            xs = resid(o, attn_w_o[j].astype(BF16), xs, gate[:, 1], weight=1.0)

        xs = ffn(xs, layer, 1, g[2], shift[:, 2], scale[:, 2], gate[:, 2])

        if layer == n_s5 - 1:
            hn = norm(xs, kv_norm_g, mod_kv[:, :d], mod_kv[:, d:], out_dtype=BF16)
            wk = w_kv[:, :kvw][:, _pair_perm(kvw // HEAD_DIM)].astype(BF16)
            k_rot = _mm_rope(hn, wk, cos, sin, out_scale=1.0, out_dtype=F32)
            v_kv = _mm_plain(hn, w_kv[:, kvw:].astype(BF16), out_dtype=F32)

    out = norm(xs, final_norm_g, out_dtype=F32)
    return out.reshape(nbatch, seq, d)
```

```python
import functools
import math

import numpy as np
import jax
import jax.numpy as jnp
from jax import lax
from jax.experimental import pallas as pl
from jax.experimental.pallas import tpu as pltpu

F32 = jnp.float32
BF16 = jnp.bfloat16

RMS_EPS = 1e-6
N_SUBLAYERS = 3
S5_GROUP = 16
S5_STATE = 64
HEAD_DIM = 64
Q_PER_KV = 8
WINDOW = 128
ROPE_THETA = 10000.0

LANES = 128
SUBLANES = 8
S5_TILE = 256
VMEM_LIMIT = 52 * 1024 * 1024


def _params(sem):
    return pltpu.CompilerParams(dimension_semantics=sem, vmem_limit_bytes=VMEM_LIMIT)


def _sigmoid(x):
    return 1.0 / (1.0 + jnp.exp(-x))


def _gelu_tanh(x):
    return 0.5 * x * (1.0 + jnp.tanh(math.sqrt(2.0 / math.pi) * (x + 0.044715 * (x * x * x))))


def _ada_kernel(c_ref, w_ref, b_ref, o_ref):
    c = c_ref[...]
    ca = (c * _sigmoid(c)).astype(BF16)
    o_ref[...] = jnp.dot(ca, w_ref[...].astype(BF16), preferred_element_type=F32) + b_ref[...]


def _ada(c_pad, w, b, bn=1024):
    s, d, n = w.shape
    return pl.pallas_call(
        _ada_kernel,
        grid=(s, n // bn),
        in_specs=[pl.BlockSpec((SUBLANES, d), lambda i, j: (0, 0)),
                  pl.BlockSpec((None, d, bn), lambda i, j: (i, 0, j)),
                  pl.BlockSpec((None, 1, bn), lambda i, j: (i, 0, j))],
        out_specs=pl.BlockSpec((None, SUBLANES, bn), lambda i, j: (i, 0, j)),
        out_shape=jax.ShapeDtypeStruct((s, SUBLANES, n), F32),
        compiler_params=_params(("parallel", "parallel")),
        name="ada",
    )(c_pad, w, b.reshape(s, 1, n))


def _norm_kernel(x_ref, g_ref, *rest, modulate):
    if modulate:
        sh_ref, sc_ref, o_ref = rest
    else:
        (o_ref,) = rest
    x = x_ref[...]
    ms = jnp.mean(x * x, axis=-1, keepdims=True)
    y = x * lax.rsqrt(ms + RMS_EPS) * g_ref[...]
    if modulate:
        y = y * (1.0 + sc_ref[...]) + sh_ref[...]
    o_ref[...] = y.astype(o_ref.dtype)


def _norm(x, g, shift=None, scale=None, *, rows_per_batch, out_dtype, bl=512):
    m, d = x.shape
    per = rows_per_batch // bl
    modulate = shift is not None
    in_specs = [pl.BlockSpec((bl, d), lambda i: (i, 0)),
                pl.BlockSpec((1, d), lambda i: (0, 0))]
    args = [x, g.reshape(1, d)]
    if modulate:
        nb = shift.shape[0]
        in_specs += [pl.BlockSpec((None, 1, d), lambda i: (i // per, 0, 0))] * 2
        args += [shift.reshape(nb, 1, d), scale.reshape(nb, 1, d)]
    return pl.pallas_call(
        functools.partial(_norm_kernel, modulate=modulate),
        grid=(m // bl,),
        in_specs=in_specs,
        out_specs=pl.BlockSpec((bl, d), lambda i: (i, 0)),
        out_shape=jax.ShapeDtypeStruct((m, d), out_dtype),
        compiler_params=_params(("parallel",)),
        name="norm",
    )(*args)


def _weight_spec(w, lead, bn, col0=0):
    k = w.shape[-2]
    return pl.BlockSpec((None,) * len(lead) + (k, bn), lambda j, i: tuple(lead) + (0, col0 + j))


def _weight_scratch(w, bn, count=1):
    return [pltpu.VMEM((w.shape[-2], bn), BF16)] * count


def _bf16_weight(w_ref, wb_ref):
    @pl.when(pl.program_id(1) == 0)
    def _():
        wb_ref[...] = w_ref[...].astype(BF16)

    return wb_ref


def _mm_plain_kernel(x_ref, w_ref, o_ref, wb_ref):
    wb_ref = _bf16_weight(w_ref, wb_ref)
    o_ref[...] = jnp.dot(x_ref[...], wb_ref[...], preferred_element_type=F32).astype(o_ref.dtype)


def _mm_swiglu_kernel(x_ref, wg_ref, wu_ref, o_ref, wgb_ref, wub_ref):
    wgb_ref = _bf16_weight(wg_ref, wgb_ref)
    wub_ref = _bf16_weight(wu_ref, wub_ref)
    x = x_ref[...]
    g = jnp.dot(x, wgb_ref[...], preferred_element_type=F32)
    u = jnp.dot(x, wub_ref[...], preferred_element_type=F32)
    o_ref[...] = (g * _sigmoid(g) * u).astype(o_ref.dtype)


def _mm_glu_kernel(x_ref, w_ref, y_ref, b_ref, o_ref, wb_ref):
    wb_ref = _bf16_weight(w_ref, wb_ref)
    acc = jnp.dot(x_ref[...], wb_ref[...], preferred_element_type=F32)
    o_ref[...] = (y_ref[...].astype(F32) * _sigmoid(acc + b_ref[...])).astype(o_ref.dtype)


def _mm_rope_kernel(x_ref, w_ref, cos_ref, sin_ref, o_ref, wb_ref, *, out_scale):
    wb_ref = _bf16_weight(w_ref, wb_ref)
    acc = jnp.dot(x_ref[...], wb_ref[...], preferred_element_type=F32)
    cos = cos_ref[...]
    sin = sin_ref[...]
    for j in range(acc.shape[1] // LANES):
        a = acc[:, j * LANES:(j + 1) * LANES]
        r = a * cos + pltpu.roll(a, LANES // 2, axis=1) * sin
        o_ref[:, j * LANES:(j + 1) * LANES] = (r * out_scale).astype(o_ref.dtype)


def _mm_resid_kernel(x_ref, w_ref, r_ref, gate_ref, o_ref, wb_ref, *, weight):
    wb_ref = _bf16_weight(w_ref, wb_ref)
    acc = jnp.dot(x_ref[...], wb_ref[...], preferred_element_type=F32)
    o_ref[...] = r_ref[...] + (weight * (1.0 + gate_ref[...])) * acc


_MM_SEM = ("arbitrary", "arbitrary")


def _mm_plain(x, w, lead=(), *, out_dtype, ncols=None, col0=0, bm=1024, bn=512):
    m, k = x.shape
    n = w.shape[-1] if ncols is None else ncols
    bn = min(bn, n)
    return pl.pallas_call(
        _mm_plain_kernel,
        grid=(n // bn, m // bm),
        in_specs=[pl.BlockSpec((bm, k), lambda j, i: (i, 0)),
                  _weight_spec(w, lead, bn, col0 // bn)],
        out_specs=pl.BlockSpec((bm, bn), lambda j, i: (i, j)),
        out_shape=jax.ShapeDtypeStruct((m, n), out_dtype),
        scratch_shapes=_weight_scratch(w, bn),
        compiler_params=_params(_MM_SEM),
        name="mm_plain",
    )(x, w)


def _mm_swiglu(x, w_in, lead=(), *, bm=1024, bn=512):
    m, k = x.shape
    f = w_in.shape[-1] // 2
    nf = f // bn
    return pl.pallas_call(
        _mm_swiglu_kernel,
        grid=(nf, m // bm),
        in_specs=[pl.BlockSpec((bm, k), lambda j, i: (i, 0)),
                  _weight_spec(w_in, lead, bn),
                  _weight_spec(w_in, lead, bn, nf)],
        out_specs=pl.BlockSpec((bm, bn), lambda j, i: (i, j)),
        out_shape=jax.ShapeDtypeStruct((m, f), BF16),
        scratch_shapes=_weight_scratch(w_in, bn, 2),
        compiler_params=_params(_MM_SEM),
        name="mm_swiglu",
    )(x, w_in, w_in)


def _mm_glu(y, w, b, lead=(), *, bm=1024, bn=512):
    m, k = y.shape
    n = w.shape[-1]
    return pl.pallas_call(
        _mm_glu_kernel,
        grid=(n // bn, m // bm),
        in_specs=[pl.BlockSpec((bm, k), lambda j, i: (i, 0)),
                  _weight_spec(w, lead, bn),
                  pl.BlockSpec((bm, bn), lambda j, i: (i, j)),
                  pl.BlockSpec((1, bn), lambda j, i: (0, j))],
        out_specs=pl.BlockSpec((bm, bn), lambda j, i: (i, j)),
        out_shape=jax.ShapeDtypeStruct((m, n), BF16),
        scratch_shapes=_weight_scratch(w, bn),
        compiler_params=_params(_MM_SEM),
        name="mm_glu",
    )(y, w, y, b.reshape(1, n))


def _mm_rope(x, w, cos, sin, *, out_scale, out_dtype, bm=1024, bn=512):
    m, k = x.shape
    n = w.shape[-1]
    bn = min(bn, n)
    return pl.pallas_call(
        functools.partial(_mm_rope_kernel, out_scale=out_scale),
        grid=(n // bn, m // bm),
        in_specs=[pl.BlockSpec((bm, k), lambda j, i: (i, 0)),
                  _weight_spec(w, (), bn),
                  pl.BlockSpec((bm, LANES), lambda j, i: (i, 0)),
                  pl.BlockSpec((bm, LANES), lambda j, i: (i, 0))],
        out_specs=pl.BlockSpec((bm, bn), lambda j, i: (i, j)),
        out_shape=jax.ShapeDtypeStruct((m, n), out_dtype),
        scratch_shapes=_weight_scratch(w, bn),
        compiler_params=_params(_MM_SEM),
        name="mm_rope",
    )(x, w, cos, sin)


def _mm_resid(x, w, lead, resid, gate, *, weight, rows_per_batch, bm=1024, bn=512):
    m, k = x.shape
    n = w.shape[-1]
    per = rows_per_batch // bm
    nb = gate.shape[0]
    return pl.pallas_call(
        functools.partial(_mm_resid_kernel, weight=weight),
        grid=(n // bn, m // bm),
        in_specs=[pl.BlockSpec((bm, k), lambda j, i: (i, 0)),
                  _weight_spec(w, lead, bn),
                  pl.BlockSpec((bm, bn), lambda j, i: (i, j)),
                  pl.BlockSpec((None, 1, bn), lambda j, i: (i // per, 0, j))],
        out_specs=pl.BlockSpec((bm, bn), lambda j, i: (i, j)),
        out_shape=jax.ShapeDtypeStruct((m, n), F32),
        scratch_shapes=_weight_scratch(w, bn),
        compiler_params=_params(_MM_SEM),
        name="mm_resid",
    )(x, w, resid, gate.reshape(nb, 1, n))


def _rope_kernel(pos_ref, invf_ref, sign_ref, cos_ref, sin_ref):
    ang = pos_ref[...].astype(F32) * invf_ref[...]
    cos_ref[...] = jnp.cos(ang)
    sin_ref[...] = jnp.sin(ang) * sign_ref[...]


def _rope_tables(positions, bl=1024):
    m = positions.size
    half = HEAD_DIM // 2
    inv_freq = 1.0 / (ROPE_THETA ** (jnp.arange(0, HEAD_DIM, 2, dtype=F32) / HEAD_DIM))
    invf = jnp.tile(inv_freq, LANES // half).reshape(1, LANES)
    sign = jnp.asarray(np.where(np.arange(LANES) < LANES // 2, -1.0, 1.0), F32).reshape(1, LANES)
    return pl.pallas_call(
        _rope_kernel,
        grid=(m // bl,),
        in_specs=[pl.BlockSpec((bl, 1), lambda i: (i, 0)),
                  pl.BlockSpec((1, LANES), lambda i: (0, 0)),
                  pl.BlockSpec((1, LANES), lambda i: (0, 0))],
        out_specs=[pl.BlockSpec((bl, LANES), lambda i: (i, 0))] * 2,
        out_shape=[jax.ShapeDtypeStruct((m, LANES), F32)] * 2,
        compiler_params=_params(("parallel",)),
        name="rope_tables",
    )(positions.reshape(m, 1), invf, sign)


def _pair_layout(w):
    k, n = w.shape
    half = HEAD_DIM // 2
    return w.reshape(k, n // LANES, 2, 2, half).swapaxes(2, 3).reshape(k, n)


def _s5_kernel(v_ref, bblk_ref, cblk_ref, a_ref, d_ref, y_ref, bu_ref, st_ref, x_ref, *, tc, pitch, nb):
    half_w = 4 * LANES
    rows = nb * tc

    @pl.when(pl.program_id(1) == 0)
    def _():
        x_ref[...] = jnp.zeros_like(x_ref)

    v2 = v_ref[...].reshape(rows, S5_TILE)
    vb = v2.astype(BF16)
    for half in range(2):
        r = jnp.dot(vb, bblk_ref[:, half * 2 * half_w:(half + 1) * 2 * half_w],
                    preferred_element_type=F32)
        for b in range(nb):
            for l in range(8):
                bu_ref[l, pl.ds((half * nb + b) * pitch, tc), :] = (
                    r[b * tc:(b + 1) * tc, l * LANES:(l + 1) * LANES])

    a = [a_ref[l] for l in range(8)]

    def step(t, xs):
        new_r, new_i = [], []
        for l in range(4):
            bur = bu_ref[l, pl.ds(t, SUBLANES, stride=pitch), :]
            bui = bu_ref[4 + l, pl.ds(t, SUBLANES, stride=pitch), :]
            xr, xi = xs[l], xs[4 + l]
            nr = a[l] * xr - a[4 + l] * xi + bur
            ni = a[l] * xi + a[4 + l] * xr + bui
            st_ref[l, pl.ds(t, SUBLANES, stride=pitch), :] = nr
            st_ref[4 + l, pl.ds(t, SUBLANES, stride=pitch), :] = ni
            new_r.append(nr)
            new_i.append(ni)
        return tuple(new_r + new_i)

    xs = lax.fori_loop(0, tc, step, tuple(x_ref[l] for l in range(8)), unroll=8)
    for l in range(8):
        x_ref[l] = xs[l]

    acc = None
    for half in range(2):
        blocks = []
        for b in range(nb):
            blocks.append(jnp.concatenate(
                [st_ref[l, pl.ds((half * nb + b) * pitch, tc), :] for l in range(8)], axis=1))
        s = jnp.concatenate(blocks, axis=0).astype(BF16)
        part = jnp.dot(s, cblk_ref[half * 2 * half_w:(half + 1) * 2 * half_w, :],
                       preferred_element_type=F32)
        acc = part if acc is None else acc + part
    y = _gelu_tanh(acc + d_ref[...] * v2)
    y_ref[...] = y.reshape(nb, tc, S5_TILE).astype(y_ref.dtype)


def _s5_discretise(a_re, a_im, b_re, b_im, c_re, c_im, log_dt, nb):
    g, p = a_re.shape
    gpt = S5_TILE // S5_GROUP
    nt = g // gpt
    dt = jnp.exp(log_dt)[:, None]
    mag = jnp.exp(a_re * dt)
    lb_re, lb_im = mag * jnp.cos(a_im * dt), mag * jnp.sin(a_im * dt)
    den = a_re * a_re + a_im * a_im
    k_re = ((lb_re - 1.0) * a_re + lb_im * a_im) / den
    k_im = (lb_im * a_re - (lb_re - 1.0) * a_im) / den
    bb_re = k_re[..., None] * b_re - k_im[..., None] * b_im
    bb_im = k_re[..., None] * b_im + k_im[..., None] * b_re
    g8 = gpt // 2
    nstate = 2 * gpt * p

    def state_cols(t):
        x = t.shape[-1]
        return t.reshape(nt, 2, g8, p, x).transpose(0, 4, 1, 2, 3).reshape(nt, x, 2, g8 * p)

    col = np.arange(nstate)
    col_group = col // (2 * g8 * p) * g8 + col % (g8 * p) // p
    own = jnp.asarray(np.arange(S5_TILE)[:, None] // S5_GROUP == col_group[None, :], F32)
    m_in = jnp.stack([state_cols(bb_re), state_cols(bb_im)], axis=3).reshape(nt, S5_GROUP, nstate)
    bblk = jnp.tile(m_in, (1, gpt, 1)) * own
    m_out = jnp.stack([state_cols(c_re.swapaxes(1, 2)), state_cols(-c_im.swapaxes(1, 2))],
                      axis=3).reshape(nt, S5_GROUP, nstate).swapaxes(1, 2)
    cblk = jnp.tile(m_out, (1, 1, gpt)) * own.T

    def a_slabs(lb):
        t = lb.reshape(nt, 2, 4, LANES)
        t = jnp.broadcast_to(t[:, :, None], (nt, 2, nb, 4, LANES))
        return t.transpose(0, 3, 1, 2, 4).reshape(nt, 4, 2 * nb, LANES)

    a = jnp.concatenate([a_slabs(lb_re), a_slabs(lb_im)], axis=1)
    return bblk.astype(BF16), cblk.astype(BF16), a


def _s5_core(v, bblk, cblk, a, d_skip, *, tc=128):
    nb, seq, d = v.shape
    assert 2 * nb == SUBLANES
    nt = d // S5_TILE
    pitch = tc + 4
    nstate = bblk.shape[2]
    return pl.pallas_call(
        functools.partial(_s5_kernel, tc=tc, pitch=pitch, nb=nb),
        grid=(nt, seq // tc),
        in_specs=[pl.BlockSpec((nb, tc, S5_TILE), lambda j, c: (0, c, j)),
                  pl.BlockSpec((None, S5_TILE, nstate), lambda j, c: (j, 0, 0)),
                  pl.BlockSpec((None, nstate, S5_TILE), lambda j, c: (j, 0, 0)),
                  pl.BlockSpec((None, 8, SUBLANES, LANES), lambda j, c: (j, 0, 0, 0)),
                  pl.BlockSpec((1, S5_TILE), lambda j, c: (0, j))],
        out_specs=pl.BlockSpec((nb, tc, S5_TILE), lambda j, c: (0, c, j)),
        out_shape=jax.ShapeDtypeStruct((nb, seq, d), BF16),
        scratch_shapes=[pltpu.VMEM((8, SUBLANES * pitch, LANES), F32),
                        pltpu.VMEM((8, SUBLANES * pitch, LANES), F32),
                        pltpu.VMEM((8, SUBLANES, LANES), F32)],
        compiler_params=_params(("parallel", "arbitrary")),
        name="s5_core",
    )(v, bblk, cblk, a, d_skip.reshape(1, d))


def _attn_kernel(sink_ref, q_ref, kp_ref, kc_ref, vp_ref, vc_ref, o_ref, *, n_kv):
    blk = q_ref.shape[0]
    lo = jnp.where(pl.program_id(1) == 0, blk, 0)
    qi = lax.broadcasted_iota(jnp.int32, (blk, 2 * blk), 0)
    kj = lax.broadcasted_iota(jnp.int32, (blk, 2 * blk), 1)
    valid = (kj > qi) & (kj <= qi + blk) & (kj >= lo)
    lane = lax.broadcasted_iota(jnp.int32, (2 * blk, LANES), 1)
    half = HEAD_DIM // 2
    pairs = Q_PER_KV // 2

    for kvh in range(n_kv):
        grp, slot = kvh // 2, kvh % 2
        kcat = jnp.concatenate([kp_ref[:, grp * LANES:(grp + 1) * LANES],
                                kc_ref[:, grp * LANES:(grp + 1) * LANES]], axis=0)
        vcat = jnp.concatenate([vp_ref[:, grp * LANES:(grp + 1) * LANES],
                                vc_ref[:, grp * LANES:(grp + 1) * LANES]], axis=0)
        kexp, vexp = [], []
        for s in range(2):
            kk = kcat if s == slot else pltpu.roll(kcat, (half * (s - slot)) % LANES, axis=1)
            kexp.append(jnp.where((lane // half) % 2 == s, kk, 0.0).astype(BF16))
            vv = vcat if s == slot else pltpu.roll(vcat, (HEAD_DIM * (s - slot)) % LANES, axis=1)
            vexp.append(jnp.where(lane // HEAD_DIM == s, vv, 0.0).astype(BF16))
        base = kvh * pairs
        qg = jnp.concatenate([q_ref[:, (base + p) * LANES:(base + p + 1) * LANES]
                              for p in range(pairs)], axis=0)
        out = None
        for s in range(2):
            sc = lax.dot_general(qg, kexp[s], (((1,), (1,)), ((), ())),
                                 preferred_element_type=F32)
            probs = []
            for p in range(pairs):
                sink = sink_ref[kvh * Q_PER_KV + 2 * p + s]
                sp = jnp.where(valid, sc[p * blk:(p + 1) * blk], -1e30)
                mx = jnp.maximum(jnp.max(sp, axis=-1, keepdims=True), sink)
                e = jnp.exp(sp - mx)
                den = jnp.sum(e, axis=-1, keepdims=True) + jnp.exp(sink - mx)
                probs.append((e / den).astype(BF16))
            pm = jnp.concatenate(probs, axis=0)
            part = jnp.dot(pm, vexp[s], preferred_element_type=F32)
            out = part if out is None else out + part
        for p in range(pairs):
            o_ref[:, (base + p) * LANES:(base + p + 1) * LANES] = (
                out[p * blk:(p + 1) * blk].astype(o_ref.dtype))


def _attention(q, k, v, sinks, *, nbatch, blk=WINDOW):
    m, dq = q.shape
    dkv = k.shape[1]
    nblk = m // nbatch // blk
    cur = lambda b, n: (b * nblk + n, 0)
    prev = lambda b, n: (b * nblk + jnp.maximum(n - 1, 0), 0)
    return pl.pallas_call(
        functools.partial(_attn_kernel, n_kv=dkv // HEAD_DIM),
        grid=(nbatch, nblk),
        in_specs=[pl.BlockSpec(memory_space=pltpu.SMEM),
                  pl.BlockSpec((blk, dq), cur),
                  pl.BlockSpec((blk, dkv), prev),
                  pl.BlockSpec((blk, dkv), cur),
                  pl.BlockSpec((blk, dkv), prev),
                  pl.BlockSpec((blk, dkv), cur)],
        out_specs=pl.BlockSpec((blk, dq), cur),
        out_shape=jax.ShapeDtypeStruct((m, dq), BF16),
        compiler_params=_params(("parallel", "arbitrary")),
        name="attention",
    )(sinks, q, k, k, v, v)


def kernel(x, c, positions, norm_g, w_ada, b_ada, w_ff_in, w_ff_out, s5_w_in, s5_a_re, s5_a_im, s5_b_re, s5_b_im, s5_c_re, s5_c_im, s5_d, s5_log_dt, s5_w_glu, s5_b_glu, s5_w_out, kv_norm_g, w_ada_kv, b_ada_kv, w_kv, attn_w_q, attn_sinks, attn_w_o, final_norm_g):
    nbatch, seq, d = x.shape
    depth = norm_g.shape[0]
    n_s5 = s5_w_in.shape[0]
    m = nbatch * seq
    kvw = w_kv.shape[1] // 2

    c_pad = jnp.zeros((SUBLANES, d), F32).at[:nbatch].set(c)
    mods = _ada(c_pad, w_ada, b_ada)[:, :nbatch]
    mod_kv = _ada(c_pad, w_ada_kv[None], b_ada_kv[None])[0, :nbatch]
    cos, sin = _rope_tables(positions)

    norm = functools.partial(_norm, rows_per_batch=seq)
    resid = functools.partial(_mm_resid, rows_per_batch=seq)

    def ffn(xs, layer, which, g, shift, scale, gate):
        h = norm(xs, g, shift, scale, out_dtype=BF16)
        act = _mm_swiglu(h, w_ff_in, (layer, which))
        return resid(act, w_ff_out, (layer, which), xs, gate, weight=0.5, bm=512)

    xs = x.reshape(m, d)
    k_rot = v_kv = None
    for layer in range(depth):
        mod = mods[layer].reshape(nbatch, N_SUBLAYERS, 3, d)
        shift, scale, gate = mod[:, :, 0], mod[:, :, 1], mod[:, :, 2]
        g = norm_g[layer]
        xs = ffn(xs, layer, 0, g[0], shift[:, 0], scale[:, 0], gate[:, 0])

        h = norm(xs, g[1], shift[:, 1], scale[:, 1], out_dtype=BF16)
        if layer < n_s5:
            i = layer
            v = _mm_plain(h, s5_w_in, (i,), out_dtype=F32)
            bblk, cblk, a = _s5_discretise(s5_a_re[i], s5_a_im[i], s5_b_re[i], s5_b_im[i],
                                           s5_c_re[i], s5_c_im[i], s5_log_dt[i], nbatch)
            y = _s5_core(v.reshape(nbatch, seq, d), bblk, cblk, a, s5_d[i].reshape(-1)).reshape(m, d)
            z = _mm_glu(y, s5_w_glu, s5_b_glu[i], (i,))
            xs = resid(z, s5_w_out, (i,), xs, gate[:, 1], weight=1.0)
        else:
            j = layer - n_s5
            q = _mm_rope(h, _pair_layout(attn_w_q[j]), cos, sin, out_scale=HEAD_DIM ** -0.5,
                         out_dtype=BF16)
            o = _attention(q, k_rot, v_kv, attn_sinks[j], nbatch=nbatch)
            xs = resid(o, attn_w_o, (j,), xs, gate[:, 1], weight=1.0)

        xs = ffn(xs, layer, 1, g[2], shift[:, 2], scale[:, 2], gate[:, 2])

        if layer == n_s5 - 1:
            hn = norm(xs, kv_norm_g, mod_kv[:, :d], mod_kv[:, d:], out_dtype=BF16)
            k_rot = _mm_rope(hn, _pair_layout(w_kv[:, :kvw]), cos, sin, out_scale=1.0, out_dtype=F32)
            v_kv = _mm_plain(hn, w_kv, (), out_dtype=F32, ncols=kvw, col0=kvw)

    out = norm(xs, final_norm_g, out_dtype=F32)
    return out.reshape(nbatch, seq, d)
```

```python
import functools
import math

import numpy as np
import jax
import jax.numpy as jnp
from jax import lax
from jax.experimental import pallas as pl
from jax.experimental.pallas import tpu as pltpu

F32 = jnp.float32
BF16 = jnp.bfloat16

RMS_EPS = 1e-6
N_SUBLAYERS = 3
S5_GROUP = 16
S5_STATE = 64
HEAD_DIM = 64
Q_PER_KV = 8
WINDOW = 128
ROPE_THETA = 10000.0

LANES = 128
SUBLANES = 8
S5_TILE = 256
VMEM_LIMIT = 52 * 1024 * 1024


def _params(sem):
    return pltpu.CompilerParams(dimension_semantics=sem, vmem_limit_bytes=VMEM_LIMIT)


def _sigmoid(x):
    return 1.0 / (1.0 + jnp.exp(-x))


def _gelu_tanh(x):
    return 0.5 * x * (1.0 + jnp.tanh(math.sqrt(2.0 / math.pi) * (x + 0.044715 * (x * x * x))))


def _ada_kernel(c_ref, w_ref, b_ref, o_ref):
    c = c_ref[...]
    ca = (c * _sigmoid(c)).astype(BF16)
    o_ref[...] = jnp.dot(ca, w_ref[...].astype(BF16), preferred_element_type=F32) + b_ref[...]


def _ada(c_pad, w, b, bn=1024):
    s, d, n = w.shape
    return pl.pallas_call(
        _ada_kernel,
        grid=(s, n // bn),
        in_specs=[pl.BlockSpec((SUBLANES, d), lambda i, j: (0, 0)),
                  pl.BlockSpec((None, d, bn), lambda i, j: (i, 0, j)),
                  pl.BlockSpec((None, 1, bn), lambda i, j: (i, 0, j))],
        out_specs=pl.BlockSpec((None, SUBLANES, bn), lambda i, j: (i, 0, j)),
        out_shape=jax.ShapeDtypeStruct((s, SUBLANES, n), F32),
        compiler_params=_params(("parallel", "parallel")),
        name="ada",
    )(c_pad, w, b.reshape(s, 1, n))


def _lane_block_sums(sq):
    acc = sq[:, :LANES]
    for j in range(1, sq.shape[1] // LANES):
        acc = acc + sq[:, j * LANES:(j + 1) * LANES]
    return acc


def _row_rsqrt(ssq_ref, width):
    part = ssq_ref[0]
    for p in range(1, ssq_ref.shape[0]):
        part = part + ssq_ref[p]
    ms = jnp.sum(part, axis=-1, keepdims=True) * (1.0 / width)
    return lax.rsqrt(ms + RMS_EPS)


def _prep_kernel(x_ref, gs_ref, xg_ref, ssq_ref):
    x = x_ref[...]
    xg_ref[...] = (x * gs_ref[...]).astype(xg_ref.dtype)
    ssq_ref[0] = _lane_block_sums(x * x)


def _prep(x, gs, *, rows_per_batch, bl=512):
    m, d = x.shape
    per = rows_per_batch // bl
    nb = gs.shape[0]
    return pl.pallas_call(
        _prep_kernel,
        grid=(m // bl,),
        in_specs=[pl.BlockSpec((bl, d), lambda i: (i, 0)),
                  pl.BlockSpec((None, 1, d), lambda i: (i // per, 0, 0))],
        out_specs=[pl.BlockSpec((bl, d), lambda i: (i, 0)),
                   pl.BlockSpec((1, bl, LANES), lambda i: (0, i, 0))],
        out_shape=[jax.ShapeDtypeStruct((m, d), BF16), jax.ShapeDtypeStruct((1, m, LANES), F32)],
        compiler_params=_params(("parallel",)),
        name="prep",
    )(x, gs.reshape(nb, 1, d))


def _final_norm_kernel(x_ref, g_ref, o_ref):
    x = x_ref[...]
    ms = jnp.mean(x * x, axis=-1, keepdims=True)
    o_ref[...] = x * lax.rsqrt(ms + RMS_EPS) * g_ref[...]


def _final_norm(x, g, bl=512):
    m, d = x.shape
    return pl.pallas_call(
        _final_norm_kernel,
        grid=(m // bl,),
        in_specs=[pl.BlockSpec((bl, d), lambda i: (i, 0)),
                  pl.BlockSpec((1, d), lambda i: (0, 0))],
        out_specs=pl.BlockSpec((bl, d), lambda i: (i, 0)),
        out_shape=jax.ShapeDtypeStruct((m, d), F32),
        compiler_params=_params(("parallel",)),
        name="final_norm",
    )(x, g.reshape(1, d))


def _weight_spec(w, lead, bn, col0=0):
    k = w.shape[-2]
    return pl.BlockSpec((None,) * len(lead) + (k, bn), lambda j, i: tuple(lead) + (0, col0 + j))


def _weight_scratch(w, bn, count=1):
    return [pltpu.VMEM((w.shape[-2], bn), BF16)] * count


def _stage_weight(w_ref, wb_ref, shift_ref=None, sw_ref=None):
    @pl.when(pl.program_id(1) == 0)
    def _():
        wb_ref[...] = w_ref[...].astype(BF16)
        if sw_ref is not None:
            sw_ref[...] = jnp.dot(shift_ref[...], wb_ref[...], preferred_element_type=F32)


def _modulated_dot(xg_ref, wb_ref, ssq_ref, sw_ref, per):
    b = pl.program_id(1) // per
    acc = jnp.dot(xg_ref[...], wb_ref[...], preferred_element_type=F32)
    return _row_rsqrt(ssq_ref, xg_ref.shape[1]) * acc + sw_ref[pl.ds(b, 1), :]


def _mm_plain_kernel(x_ref, ssq_ref, shift_ref, w_ref, o_ref, wb_ref, sw_ref, *, per):
    _stage_weight(w_ref, wb_ref, shift_ref, sw_ref)
    o_ref[...] = _modulated_dot(x_ref, wb_ref, ssq_ref, sw_ref, per).astype(o_ref.dtype)


def _mm_swiglu_kernel(x_ref, ssq_ref, shift_ref, wg_ref, wu_ref, o_ref,
                      wgb_ref, wub_ref, swg_ref, swu_ref, *, per):
    _stage_weight(wg_ref, wgb_ref, shift_ref, swg_ref)
    _stage_weight(wu_ref, wub_ref, shift_ref, swu_ref)
    g = _modulated_dot(x_ref, wgb_ref, ssq_ref, swg_ref, per)
    u = _modulated_dot(x_ref, wub_ref, ssq_ref, swu_ref, per)
    o_ref[...] = (g * _sigmoid(g) * u).astype(o_ref.dtype)


def _mm_glu_kernel(x_ref, w_ref, y_ref, b_ref, o_ref, wb_ref):
    _stage_weight(w_ref, wb_ref)
    acc = jnp.dot(x_ref[...], wb_ref[...], preferred_element_type=F32)
    o_ref[...] = (y_ref[...].astype(F32) * _sigmoid(acc + b_ref[...])).astype(o_ref.dtype)


def _mm_rope_kernel(x_ref, ssq_ref, shift_ref, w_ref, cos_ref, sin_ref, o_ref, wb_ref, sw_ref,
                    *, per, out_scale):
    _stage_weight(w_ref, wb_ref, shift_ref, sw_ref)
    acc = _modulated_dot(x_ref, wb_ref, ssq_ref, sw_ref, per)
    cos = cos_ref[...]
    sin = sin_ref[...]
    for j in range(acc.shape[1] // LANES):
        a = acc[:, j * LANES:(j + 1) * LANES]
        r = a * cos + pltpu.roll(a, LANES // 2, axis=1) * sin
        o_ref[:, j * LANES:(j + 1) * LANES] = (r * out_scale).astype(o_ref.dtype)


def _mm_resid_kernel(x_ref, w_ref, r_ref, gate_ref, *rest, weight, n_next):
    gs_refs, o_ref, xg_refs = rest[:n_next], rest[n_next], rest[n_next + 1:2 * n_next + 1]
    wb_ref = rest[-1]
    _stage_weight(w_ref, wb_ref)
    acc = jnp.dot(x_ref[...], wb_ref[...], preferred_element_type=F32)
    x_new = r_ref[...] + (weight * (1.0 + gate_ref[...])) * acc
    o_ref[...] = x_new
    if not n_next:
        return
    for gs_ref, xg_ref in zip(gs_refs, xg_refs):
        xg_ref[...] = (x_new * gs_ref[...]).astype(xg_ref.dtype)
    ssq_ref = rest[-2]
    ssq_ref[...] = _lane_block_sums(x_new * x_new)


_MM_SEM = ("arbitrary", "arbitrary")


def _modulated_specs(mod_in, bm):
    xg, ssq, _ = mod_in
    k = xg.shape[1]
    return [pl.BlockSpec((bm, k), lambda j, i: (i, 0)),
            pl.BlockSpec((ssq.shape[0], bm, LANES), lambda j, i: (0, i, 0)),
            pl.BlockSpec((SUBLANES, k), lambda j, i: (0, 0))]


def _mm_plain(mod_in, w, lead=(), *, out_dtype, rows_per_batch, ncols=None, col0=0, bm=1024, bn=512):
    xg, ssq, shift = mod_in
    m, k = xg.shape
    n = w.shape[-1] if ncols is None else ncols
    bn = min(bn, n)
    return pl.pallas_call(
        functools.partial(_mm_plain_kernel, per=rows_per_batch // bm),
        grid=(n // bn, m // bm),
        in_specs=_modulated_specs(mod_in, bm) + [_weight_spec(w, lead, bn, col0 // bn)],
        out_specs=pl.BlockSpec((bm, bn), lambda j, i: (i, j)),
        out_shape=jax.ShapeDtypeStruct((m, n), out_dtype),
        scratch_shapes=_weight_scratch(w, bn) + [pltpu.VMEM((SUBLANES, bn), F32)],
        compiler_params=_params(_MM_SEM),
        name="mm_plain",
    )(xg, ssq, shift, w)


def _mm_swiglu(mod_in, w_in, lead=(), *, rows_per_batch, bm=1024, bn=512):
    xg, ssq, shift = mod_in
    m, k = xg.shape
    f = w_in.shape[-1] // 2
    nf = f // bn
    return pl.pallas_call(
        functools.partial(_mm_swiglu_kernel, per=rows_per_batch // bm),
        grid=(nf, m // bm),
        in_specs=_modulated_specs(mod_in, bm) + [_weight_spec(w_in, lead, bn),
                                             _weight_spec(w_in, lead, bn, nf)],
        out_specs=pl.BlockSpec((bm, bn), lambda j, i: (i, j)),
        out_shape=jax.ShapeDtypeStruct((m, f), BF16),
        scratch_shapes=_weight_scratch(w_in, bn, 2) + [pltpu.VMEM((SUBLANES, bn), F32)] * 2,
        compiler_params=_params(_MM_SEM),
        name="mm_swiglu",
    )(xg, ssq, shift, w_in, w_in)


def _mm_glu(y, w, b, lead=(), *, bm=1024, bn=512):
    m, k = y.shape
    n = w.shape[-1]
    return pl.pallas_call(
        _mm_glu_kernel,
        grid=(n // bn, m // bm),
        in_specs=[pl.BlockSpec((bm, k), lambda j, i: (i, 0)),
                  _weight_spec(w, lead, bn),
                  pl.BlockSpec((bm, bn), lambda j, i: (i, j)),
                  pl.BlockSpec((1, bn), lambda j, i: (0, j))],
        out_specs=pl.BlockSpec((bm, bn), lambda j, i: (i, j)),
        out_shape=jax.ShapeDtypeStruct((m, n), BF16),
        scratch_shapes=_weight_scratch(w, bn),
        compiler_params=_params(_MM_SEM),
        name="mm_glu",
    )(y, w, y, b.reshape(1, n))


def _mm_rope(mod_in, w, cos, sin, *, out_scale, out_dtype, rows_per_batch, bm=1024, bn=512):
    xg, ssq, shift = mod_in
    m, k = xg.shape
    n = w.shape[-1]
    bn = min(bn, n)
    return pl.pallas_call(
        functools.partial(_mm_rope_kernel, per=rows_per_batch // bm, out_scale=out_scale),
        grid=(n // bn, m // bm),
        in_specs=_modulated_specs(mod_in, bm) + [_weight_spec(w, (), bn),
                                             pl.BlockSpec((bm, LANES), lambda j, i: (i, 0)),
                                             pl.BlockSpec((bm, LANES), lambda j, i: (i, 0))],
        out_specs=pl.BlockSpec((bm, bn), lambda j, i: (i, j)),
        out_shape=jax.ShapeDtypeStruct((m, n), out_dtype),
        scratch_shapes=_weight_scratch(w, bn) + [pltpu.VMEM((SUBLANES, bn), F32)],
        compiler_params=_params(_MM_SEM),
        name="mm_rope",
    )(xg, ssq, shift, w, cos, sin)


def _mm_resid(x, w, lead, resid, gate, next_gs=(), *, weight, rows_per_batch, bm=1024, bn=512):
    m, k = x.shape
    n = w.shape[-1]
    per = rows_per_batch // bm
    nb = gate.shape[0]
    n_next = len(next_gs)
    tile = pl.BlockSpec((bm, bn), lambda j, i: (i, j))
    per_batch = pl.BlockSpec((None, 1, bn), lambda j, i: (i // per, 0, j))
    out_specs = [tile] * (1 + n_next)
    out_shape = [jax.ShapeDtypeStruct((m, n), F32)] + [jax.ShapeDtypeStruct((m, n), BF16)] * n_next
    if n_next:
        out_specs.append(pl.BlockSpec((None, bm, LANES), lambda j, i: (j, i, 0)))
        out_shape.append(jax.ShapeDtypeStruct((n // bn, m, LANES), F32))
    outs = pl.pallas_call(
        functools.partial(_mm_resid_kernel, weight=weight, n_next=n_next),
        grid=(n // bn, m // bm),
        in_specs=[pl.BlockSpec((bm, k), lambda j, i: (i, 0)),
                  _weight_spec(w, lead, bn), tile, per_batch] + [per_batch] * n_next,
        out_specs=out_specs,
        out_shape=out_shape,
        scratch_shapes=_weight_scratch(w, bn),
        compiler_params=_params(_MM_SEM),
        name="mm_resid",
    )(x, w, resid, gate.reshape(nb, 1, n), *[gs.reshape(nb, 1, n) for gs in next_gs])
    return outs[0], list(outs[1:1 + n_next]), (outs[-1] if n_next else None)


def _rope_kernel(pos_ref, invf_ref, sign_ref, cos_ref, sin_ref):
    ang = pos_ref[...].astype(F32) * invf_ref[...]
    cos_ref[...] = jnp.cos(ang)
    sin_ref[...] = jnp.sin(ang) * sign_ref[...]


def _rope_tables(positions, bl=1024):
    m = positions.size
    half = HEAD_DIM // 2
    inv_freq = 1.0 / (ROPE_THETA ** (jnp.arange(0, HEAD_DIM, 2, dtype=F32) / HEAD_DIM))
    invf = jnp.tile(inv_freq, LANES // half).reshape(1, LANES)
    sign = jnp.asarray(np.where(np.arange(LANES) < LANES // 2, -1.0, 1.0), F32).reshape(1, LANES)
    return pl.pallas_call(
        _rope_kernel,
        grid=(m // bl,),
        in_specs=[pl.BlockSpec((bl, 1), lambda i: (i, 0)),
                  pl.BlockSpec((1, LANES), lambda i: (0, 0)),
                  pl.BlockSpec((1, LANES), lambda i: (0, 0))],
        out_specs=[pl.BlockSpec((bl, LANES), lambda i: (i, 0))] * 2,
        out_shape=[jax.ShapeDtypeStruct((m, LANES), F32)] * 2,
        compiler_params=_params(("parallel",)),
        name="rope_tables",
    )(positions.reshape(m, 1), invf, sign)


def _pair_layout(w):
    k, n = w.shape
    half = HEAD_DIM // 2
    return w.reshape(k, n // LANES, 2, 2, half).swapaxes(2, 3).reshape(k, n)


def _s5_input_map(v, bblk_ref, bu_ref, *, tc, pitch, nb):
    half_w = 4 * LANES
    vb = v.reshape(nb * tc, S5_TILE).astype(BF16)
    for half in range(2):
        r = jnp.dot(vb, bblk_ref[:, half * 2 * half_w:(half + 1) * 2 * half_w],
                    preferred_element_type=F32)
        for b in range(nb):
            for l in range(8):
                bu_ref[l, pl.ds((half * nb + b) * pitch, tc), :] = (
                    r[b * tc:(b + 1) * tc, l * LANES:(l + 1) * LANES])


def _s5_scan(bu_ref, st_ref, a, xs, *, tc, pitch):
    xs = list(xs)
    for t in range(tc):
        for l in range(4):
            bur = bu_ref[l, pl.ds(t, SUBLANES, stride=pitch), :]
            bui = bu_ref[4 + l, pl.ds(t, SUBLANES, stride=pitch), :]
            xr, xi = xs[l], xs[4 + l]
            nr = a[l] * xr - a[4 + l] * xi + bur
            ni = a[l] * xi + a[4 + l] * xr + bui
            st_ref[l, pl.ds(t, SUBLANES, stride=pitch), :] = nr
            st_ref[4 + l, pl.ds(t, SUBLANES, stride=pitch), :] = ni
            xs[l], xs[4 + l] = nr, ni
    return xs


def _s5_output_map(st_ref, cblk_ref, v, d, *, tc, pitch, nb):
    half_w = 4 * LANES
    acc = None
    for half in range(2):
        blocks = []
        for b in range(nb):
            blocks.append(jnp.concatenate(
                [st_ref[l, pl.ds((half * nb + b) * pitch, tc), :] for l in range(8)], axis=1))
        s = jnp.concatenate(blocks, axis=0).astype(BF16)
        part = jnp.dot(s, cblk_ref[half * 2 * half_w:(half + 1) * 2 * half_w, :],
                       preferred_element_type=F32)
        acc = part if acc is None else acc + part
    y = _gelu_tanh(acc + d * v.reshape(nb * tc, S5_TILE))
    return y.reshape(nb, tc, S5_TILE)


def _s5_kernel(va_ref, vc_ref, bblk_ref, cblk_ref, a_ref, d_ref, y_ref,
               bu0_ref, bu1_ref, st0_ref, st1_ref, x_ref, *, tc, pitch, nb):
    s = pl.program_id(1)
    kw = dict(tc=tc, pitch=pitch, nb=nb)

    @pl.when(s == 0)
    def _():
        x_ref[...] = jnp.zeros_like(x_ref)
        bu1_ref[...] = jnp.zeros_like(bu1_ref)
        st0_ref[...] = jnp.zeros_like(st0_ref)

    bus, sts = (bu0_ref, bu1_ref), (st0_ref, st1_ref)
    a = [a_ref[l] for l in range(8)]
    d = d_ref[...]
    xs = [x_ref[l] for l in range(8)]
    for h in range(2):
        _s5_input_map(va_ref[:, h * tc:(h + 1) * tc, :], bblk_ref, bus[h], **kw)
        xs = _s5_scan(bus[1 - h], sts[1 - h], a, xs, tc=tc, pitch=pitch)
        y = _s5_output_map(sts[h], cblk_ref, vc_ref[:, h * tc:(h + 1) * tc, :], d, **kw)
        y_ref[:, h * tc:(h + 1) * tc, :] = y.astype(y_ref.dtype)
    for l in range(8):
        x_ref[l] = xs[l]


def _s5_discretise(a_re, a_im, b_re, b_im, c_re, c_im, log_dt, nb):
    g, p = a_re.shape
    gpt = S5_TILE // S5_GROUP
    nt = g // gpt
    dt = jnp.exp(log_dt)[:, None]
    mag = jnp.exp(a_re * dt)
    lb_re, lb_im = mag * jnp.cos(a_im * dt), mag * jnp.sin(a_im * dt)
    den = a_re * a_re + a_im * a_im
    k_re = ((lb_re - 1.0) * a_re + lb_im * a_im) / den
    k_im = (lb_im * a_re - (lb_re - 1.0) * a_im) / den
    bb_re = k_re[..., None] * b_re - k_im[..., None] * b_im
    bb_im = k_re[..., None] * b_im + k_im[..., None] * b_re
    g8 = gpt // 2
    nstate = 2 * gpt * p

    def state_cols(t):
        x = t.shape[-1]
        return t.reshape(nt, 2, g8, p, x).transpose(0, 4, 1, 2, 3).reshape(nt, x, 2, g8 * p)

    col = np.arange(nstate)
    col_group = col // (2 * g8 * p) * g8 + col % (g8 * p) // p
    own = jnp.asarray(np.arange(S5_TILE)[:, None] // S5_GROUP == col_group[None, :], F32)
    m_in = jnp.stack([state_cols(bb_re), state_cols(bb_im)], axis=3).reshape(nt, S5_GROUP, nstate)
    bblk = jnp.tile(m_in, (1, gpt, 1)) * own
    m_out = jnp.stack([state_cols(c_re.swapaxes(1, 2)), state_cols(-c_im.swapaxes(1, 2))],
                      axis=3).reshape(nt, S5_GROUP, nstate).swapaxes(1, 2)
    cblk = jnp.tile(m_out, (1, 1, gpt)) * own.T

    def a_slabs(lb):
        t = lb.reshape(nt, 2, 4, LANES)
        t = jnp.broadcast_to(t[:, :, None], (nt, 2, nb, 4, LANES))
        return t.transpose(0, 3, 1, 2, 4).reshape(nt, 4, 2 * nb, LANES)

    a = jnp.concatenate([a_slabs(lb_re), a_slabs(lb_im)], axis=1)
    return bblk.astype(BF16), cblk.astype(BF16), a


def _s5_core(v, bblk, cblk, a, d_skip, *, tc=128):
    nb, seq, d = v.shape
    assert 2 * nb == SUBLANES
    nt = d // S5_TILE
    steps = seq // (2 * tc) + 1
    pitch = tc + 4
    nstate = bblk.shape[2]
    slab = pltpu.VMEM((8, SUBLANES * pitch, LANES), F32)
    return pl.pallas_call(
        functools.partial(_s5_kernel, tc=tc, pitch=pitch, nb=nb),
        grid=(nt, steps),
        in_specs=[pl.BlockSpec((nb, 2 * tc, S5_TILE), lambda j, s: (0, jnp.minimum(s, steps - 2), j)),
                  pl.BlockSpec((nb, 2 * tc, S5_TILE), lambda j, s: (0, jnp.maximum(s - 1, 0), j)),
                  pl.BlockSpec((None, S5_TILE, nstate), lambda j, s: (j, 0, 0)),
                  pl.BlockSpec((None, nstate, S5_TILE), lambda j, s: (j, 0, 0)),
                  pl.BlockSpec((None, 8, SUBLANES, LANES), lambda j, s: (j, 0, 0, 0)),
                  pl.BlockSpec((1, S5_TILE), lambda j, s: (0, j))],
        out_specs=pl.BlockSpec((nb, 2 * tc, S5_TILE), lambda j, s: (0, jnp.maximum(s - 1, 0), j)),
        out_shape=jax.ShapeDtypeStruct((nb, seq, d), BF16),
        scratch_shapes=[slab, slab, slab, slab, pltpu.VMEM((8, SUBLANES, LANES), F32)],
        compiler_params=_params(("arbitrary", "arbitrary")),
        name="s5_core",
    )(v, v, bblk, cblk, a, d_skip.reshape(1, d))


def _attn_kernel(sink_ref, q_ref, kp_ref, kc_ref, vp_ref, vc_ref, o_ref, *, n_kv):
    blk = q_ref.shape[0]
    lo = jnp.where(pl.program_id(1) == 0, blk, 0)
    qi = lax.broadcasted_iota(jnp.int32, (blk, 2 * blk), 0)
    kj = lax.broadcasted_iota(jnp.int32, (blk, 2 * blk), 1)
    valid = (kj > qi) & (kj <= qi + blk) & (kj >= lo)
    lane = lax.broadcasted_iota(jnp.int32, (2 * blk, LANES), 1)
    half = HEAD_DIM // 2
    pairs = Q_PER_KV // 2

    for kvh in range(n_kv):
        grp, slot = kvh // 2, kvh % 2
        kcat = jnp.concatenate([kp_ref[:, grp * LANES:(grp + 1) * LANES],
                                kc_ref[:, grp * LANES:(grp + 1) * LANES]], axis=0)
        vcat = jnp.concatenate([vp_ref[:, grp * LANES:(grp + 1) * LANES],
                                vc_ref[:, grp * LANES:(grp + 1) * LANES]], axis=0)
        kexp, vexp = [], []
        for s in range(2):
            kk = kcat if s == slot else pltpu.roll(kcat, (half * (s - slot)) % LANES, axis=1)
            kexp.append(jnp.where((lane // half) % 2 == s, kk, 0.0).astype(BF16))
            vv = vcat if s == slot else pltpu.roll(vcat, (HEAD_DIM * (s - slot)) % LANES, axis=1)
            vexp.append(jnp.where(lane // HEAD_DIM == s, vv, 0.0).astype(BF16))
        base = kvh * pairs
        qg = jnp.concatenate([q_ref[:, (base + p) * LANES:(base + p + 1) * LANES]
                              for p in range(pairs)], axis=0)
        out = None
        for s in range(2):
            sc = lax.dot_general(qg, kexp[s], (((1,), (1,)), ((), ())),
                                 preferred_element_type=F32)
            probs = []
            for p in range(pairs):
                sink = sink_ref[kvh * Q_PER_KV + 2 * p + s]
                sp = jnp.where(valid, sc[p * blk:(p + 1) * blk], -1e30)
                mx = jnp.maximum(jnp.max(sp, axis=-1, keepdims=True), sink)
                e = jnp.exp(sp - mx)
                den = jnp.sum(e, axis=-1, keepdims=True) + jnp.exp(sink - mx)
                probs.append((e / den).astype(BF16))
            pm = jnp.concatenate(probs, axis=0)
            part = jnp.dot(pm, vexp[s], preferred_element_type=F32)
            out = part if out is None else out + part
        for p in range(pairs):
            o_ref[:, (base + p) * LANES:(base + p + 1) * LANES] = (
                out[p * blk:(p + 1) * blk].astype(o_ref.dtype))


def _attention(q, k, v, sinks, *, nbatch, blk=WINDOW):
    m, dq = q.shape
    dkv = k.shape[1]
    nblk = m // nbatch // blk
    cur = lambda b, n: (b * nblk + n, 0)
    prev = lambda b, n: (b * nblk + jnp.maximum(n - 1, 0), 0)
    return pl.pallas_call(
        functools.partial(_attn_kernel, n_kv=dkv // HEAD_DIM),
        grid=(nbatch, nblk),
        in_specs=[pl.BlockSpec(memory_space=pltpu.SMEM),
                  pl.BlockSpec((blk, dq), cur),
                  pl.BlockSpec((blk, dkv), prev),
                  pl.BlockSpec((blk, dkv), cur),
                  pl.BlockSpec((blk, dkv), prev),
                  pl.BlockSpec((blk, dkv), cur)],
        out_specs=pl.BlockSpec((blk, dq), cur),
        out_shape=jax.ShapeDtypeStruct((m, dq), BF16),
        compiler_params=_params(("parallel", "arbitrary")),
        name="attention",
    )(sinks, q, k, k, v, v)


def kernel(x, c, positions, norm_g, w_ada, b_ada, w_ff_in, w_ff_out, s5_w_in, s5_a_re, s5_a_im, s5_b_re, s5_b_im, s5_c_re, s5_c_im, s5_d, s5_log_dt, s5_w_glu, s5_b_glu, s5_w_out, kv_norm_g, w_ada_kv, b_ada_kv, w_kv, attn_w_q, attn_sinks, attn_w_o, final_norm_g):
    nbatch, seq, d = x.shape
    depth = norm_g.shape[0]
    n_s5 = s5_w_in.shape[0]
    m = nbatch * seq
    kvw = w_kv.shape[1] // 2

    c_pad = jnp.zeros((SUBLANES, d), F32).at[:nbatch].set(c)
    mods = _ada(c_pad, w_ada, b_ada)[:, :nbatch]
    mod_kv = _ada(c_pad, w_ada_kv[None], b_ada_kv[None])[0, :nbatch]
    cos, sin = _rope_tables(positions)

    per_batch = dict(rows_per_batch=seq)
    mod = mods.reshape(depth, nbatch, N_SUBLAYERS, 3, d)
    shift, scale, gate = mod[:, :, :, 0], mod[:, :, :, 1], mod[:, :, :, 2]

    def gs_of(layer, sub):
        return norm_g[layer, sub][None, :] * (1.0 + scale[layer, :, sub])

    def shift_rows(sh):
        return jnp.zeros((SUBLANES, d), F32).at[:nbatch].set(sh).astype(BF16)

    def ffn(mod_in, xs, layer, which, next_gs):
        act = _mm_swiglu(mod_in, w_ff_in, (layer, which), **per_batch)
        return _mm_resid(act, w_ff_out, (layer, which), xs, gate[layer, :, 2 * which], next_gs,
                         weight=0.5, bm=512, **per_batch)

    xs = x.reshape(m, d)
    xg, ssq = _prep(xs, gs_of(0, 0), **per_batch)
    k_rot = v_kv = None
    for layer in range(depth):
        xs, (xg,), ssq = ffn((xg, ssq, shift_rows(shift[layer, :, 0])), xs, layer, 0, [gs_of(layer, 1)])

        mix_in = (xg, ssq, shift_rows(shift[layer, :, 1]))
        if layer < n_s5:
            i = layer
            v = _mm_plain(mix_in, s5_w_in, (i,), out_dtype=F32, **per_batch)
            bblk, cblk, a = _s5_discretise(s5_a_re[i], s5_a_im[i], s5_b_re[i], s5_b_im[i],
                                           s5_c_re[i], s5_c_im[i], s5_log_dt[i], nbatch)
            y = _s5_core(v.reshape(nbatch, seq, d), bblk, cblk, a, s5_d[i].reshape(-1)).reshape(m, d)
            mixed = _mm_glu(y, s5_w_glu, s5_b_glu[i], (i,))
            w_mix_out, lead = s5_w_out, (i,)
        else:
            j = layer - n_s5
            q = _mm_rope(mix_in, _pair_layout(attn_w_q[j]), cos, sin, out_scale=HEAD_DIM ** -0.5,
                         out_dtype=BF16, **per_batch)
            mixed = _attention(q, k_rot, v_kv, attn_sinks[j], nbatch=nbatch)
            w_mix_out, lead = attn_w_o, (j,)
        xs, (xg,), ssq = _mm_resid(mixed, w_mix_out, lead, xs, gate[layer, :, 1], [gs_of(layer, 2)],
                                   weight=1.0, **per_batch)

        next_gs = [gs_of(layer + 1, 0)] if layer + 1 < depth else []
        if layer == n_s5 - 1:
            next_gs.append(kv_norm_g[None, :] * (1.0 + mod_kv[:, d:]))
        xs, xgs, ssq = ffn((xg, ssq, shift_rows(shift[layer, :, 2])), xs, layer, 1, next_gs)
        if layer == n_s5 - 1:
            kv_in = (xgs[-1], ssq, shift_rows(mod_kv[:, :d]))
            k_rot = _mm_rope(kv_in, _pair_layout(w_kv[:, :kvw]), cos, sin, out_scale=1.0,
                             out_dtype=F32, **per_batch)
            v_kv = _mm_plain(kv_in, w_kv, (), out_dtype=F32, ncols=kvw, col0=kvw, **per_batch)
        if layer + 1 < depth:
            xg = xgs[0]

    return _final_norm(xs, final_norm_g).reshape(nbatch, seq, d)
```

```python
import functools
import math

import numpy as np
import jax
import jax.numpy as jnp
from jax import lax
from jax.experimental import pallas as pl
from jax.experimental.pallas import tpu as pltpu

F32 = jnp.float32
BF16 = jnp.bfloat16

RMS_EPS = 1e-6
N_SUBLAYERS = 3
S5_GROUP = 16
S5_STATE = 64
HEAD_DIM = 64
Q_PER_KV = 8
WINDOW = 128
ROPE_THETA = 10000.0

LANES = 128
SUBLANES = 8
S5_TILE = 256
VMEM_LIMIT = 52 * 1024 * 1024


def _params(sem):
    return pltpu.CompilerParams(dimension_semantics=sem, vmem_limit_bytes=VMEM_LIMIT)


def _sigmoid(x):
    return 1.0 / (1.0 + jnp.exp(-x))


def _gelu_tanh(x):
    return 0.5 * x * (1.0 + jnp.tanh(math.sqrt(2.0 / math.pi) * (x + 0.044715 * (x * x * x))))


def _ada_kernel(c_ref, w_ref, b_ref, o_ref):
    c = c_ref[...]
    ca = (c * _sigmoid(c)).astype(BF16)
    o_ref[...] = jnp.dot(ca, w_ref[...].astype(BF16), preferred_element_type=F32) + b_ref[...]


def _ada(c_pad, w, b, bn=1024):
    s, d, n = w.shape
    return pl.pallas_call(
        _ada_kernel,
        grid=(s, n // bn),
        in_specs=[pl.BlockSpec((SUBLANES, d), lambda i, j: (0, 0)),
                  pl.BlockSpec((None, d, bn), lambda i, j: (i, 0, j)),
                  pl.BlockSpec((None, 1, bn), lambda i, j: (i, 0, j))],
        out_specs=pl.BlockSpec((None, SUBLANES, bn), lambda i, j: (i, 0, j)),
        out_shape=jax.ShapeDtypeStruct((s, SUBLANES, n), F32),
        compiler_params=_params(("parallel", "parallel")),
        name="ada",
    )(c_pad, w, b.reshape(s, 1, n))


def _lane_block_sums(sq):
    acc = sq[:, :LANES]
    for j in range(1, sq.shape[1] // LANES):
        acc = acc + sq[:, j * LANES:(j + 1) * LANES]
    return acc


def _row_rsqrt(ssq_ref, width):
    part = ssq_ref[0]
    for p in range(1, ssq_ref.shape[0]):
        part = part + ssq_ref[p]
    ms = jnp.sum(part, axis=-1, keepdims=True) * (1.0 / width)
    return lax.rsqrt(ms + RMS_EPS)


def _prep_kernel(x_ref, gs_ref, xg_ref, ssq_ref):
    x = x_ref[...]
    xg_ref[...] = (x * gs_ref[...]).astype(xg_ref.dtype)
    ssq_ref[0] = _lane_block_sums(x * x)


def _prep(x, gs, *, rows_per_batch, bl=512):
    m, d = x.shape
    per = rows_per_batch // bl
    nb = gs.shape[0]
    return pl.pallas_call(
        _prep_kernel,
        grid=(m // bl,),
        in_specs=[pl.BlockSpec((bl, d), lambda i: (i, 0)),
                  pl.BlockSpec((None, 1, d), lambda i: (i // per, 0, 0))],
        out_specs=[pl.BlockSpec((bl, d), lambda i: (i, 0)),
                   pl.BlockSpec((1, bl, LANES), lambda i: (0, i, 0))],
        out_shape=[jax.ShapeDtypeStruct((m, d), BF16), jax.ShapeDtypeStruct((1, m, LANES), F32)],
        compiler_params=_params(("parallel",)),
        name="prep",
    )(x, gs.reshape(nb, 1, d))


def _final_norm_kernel(x_ref, g_ref, o_ref):
    x = x_ref[...]
    ms = jnp.mean(x * x, axis=-1, keepdims=True)
    o_ref[...] = x * lax.rsqrt(ms + RMS_EPS) * g_ref[...]


def _final_norm(x, g, bl=512):
    m, d = x.shape
    return pl.pallas_call(
        _final_norm_kernel,
        grid=(m // bl,),
        in_specs=[pl.BlockSpec((bl, d), lambda i: (i, 0)),
                  pl.BlockSpec((1, d), lambda i: (0, 0))],
        out_specs=pl.BlockSpec((bl, d), lambda i: (i, 0)),
        out_shape=jax.ShapeDtypeStruct((m, d), F32),
        compiler_params=_params(("parallel",)),
        name="final_norm",
    )(x, g.reshape(1, d))


def _weight_spec(w, lead, bn, col0=0):
    k = w.shape[-2]
    return pl.BlockSpec((None,) * len(lead) + (k, bn), lambda j, i: tuple(lead) + (0, col0 + j))


def _weight_scratch(w, bn, count=1):
    return [pltpu.VMEM((w.shape[-2], bn), BF16)] * count


def _stage_weight(w_ref, wb_ref, shift_ref=None, sw_ref=None):
    @pl.when(pl.program_id(1) == 0)
    def _():
        wb_ref[...] = w_ref[...].astype(BF16)
        if sw_ref is not None:
            sw_ref[...] = jnp.dot(shift_ref[...], wb_ref[...], preferred_element_type=F32)


def _modulated_dot(xg_ref, wb_ref, ssq_ref, sw_ref, per):
    b = pl.program_id(1) // per
    acc = jnp.dot(xg_ref[...], wb_ref[...], preferred_element_type=F32)
    return _row_rsqrt(ssq_ref, xg_ref.shape[1]) * acc + sw_ref[pl.ds(b, 1), :]


def _mm_plain_kernel(x_ref, ssq_ref, shift_ref, w_ref, o_ref, wb_ref, sw_ref, *, per):
    _stage_weight(w_ref, wb_ref, shift_ref, sw_ref)
    o_ref[...] = _modulated_dot(x_ref, wb_ref, ssq_ref, sw_ref, per).astype(o_ref.dtype)


def _mm_swiglu_kernel(x_ref, ssq_ref, shift_ref, wg_ref, wu_ref, wo_ref, o_ref, wob_ref,
                      wgb_ref, wub_ref, swg_ref, swu_ref, *, per):
    _stage_weight(wg_ref, wgb_ref, shift_ref, swg_ref)
    _stage_weight(wu_ref, wub_ref, shift_ref, swu_ref)
    g = _modulated_dot(x_ref, wgb_ref, ssq_ref, swg_ref, per)
    u = _modulated_dot(x_ref, wub_ref, ssq_ref, swu_ref, per)
    o_ref[...] = (g * _sigmoid(g) * u).astype(o_ref.dtype)
    wob_ref[...] = wo_ref[...].astype(wob_ref.dtype)


def _mm_glu_kernel(x_ref, w_ref, y_ref, b_ref, o_ref, wb_ref):
    _stage_weight(w_ref, wb_ref)
    acc = jnp.dot(x_ref[...], wb_ref[...], preferred_element_type=F32)
    o_ref[...] = (y_ref[...].astype(F32) * _sigmoid(acc + b_ref[...])).astype(o_ref.dtype)


def _mm_rope_kernel(x_ref, ssq_ref, shift_ref, w_ref, cos_ref, sin_ref, o_ref, wb_ref, sw_ref,
                    *, per, out_scale):
    _stage_weight(w_ref, wb_ref, shift_ref, sw_ref)
    acc = _modulated_dot(x_ref, wb_ref, ssq_ref, sw_ref, per)
    cos = cos_ref[...]
    sin = sin_ref[...]
    for j in range(acc.shape[1] // LANES):
        a = acc[:, j * LANES:(j + 1) * LANES]
        r = a * cos + pltpu.roll(a, LANES // 2, axis=1) * sin
        o_ref[:, j * LANES:(j + 1) * LANES] = (r * out_scale).astype(o_ref.dtype)


def _mm_resid_kernel(x_ref, w_ref, r_ref, gate_ref, *rest, weight, n_next, staged):
    gs_refs, o_ref, xg_refs = rest[:n_next], rest[n_next], rest[n_next + 1:2 * n_next + 1]
    if staged:
        wb_ref = rest[-1]
        _stage_weight(w_ref, wb_ref)
    else:
        wb_ref = w_ref
    acc = jnp.dot(x_ref[...], wb_ref[...], preferred_element_type=F32)
    x_new = r_ref[...] + (weight * (1.0 + gate_ref[...])) * acc
    o_ref[...] = x_new
    if not n_next:
        return
    for gs_ref, xg_ref in zip(gs_refs, xg_refs):
        xg_ref[...] = (x_new * gs_ref[...]).astype(xg_ref.dtype)
    ssq_ref = rest[2 * n_next + 1]
    ssq_ref[...] = _lane_block_sums(x_new * x_new)


_MM_SEM = ("arbitrary", "arbitrary")


def _modulated_specs(mod_in, bm):
    xg, ssq, _ = mod_in
    k = xg.shape[1]
    return [pl.BlockSpec((bm, k), lambda j, i: (i, 0)),
            pl.BlockSpec((ssq.shape[0], bm, LANES), lambda j, i: (0, i, 0)),
            pl.BlockSpec((SUBLANES, k), lambda j, i: (0, 0))]


def _mm_plain(mod_in, w, lead=(), *, out_dtype, rows_per_batch, ncols=None, col0=0, bm=1024, bn=1024):
    xg, ssq, shift = mod_in
    m, k = xg.shape
    n = w.shape[-1] if ncols is None else ncols
    bn = min(bn, n)
    return pl.pallas_call(
        functools.partial(_mm_plain_kernel, per=rows_per_batch // bm),
        grid=(n // bn, m // bm),
        in_specs=_modulated_specs(mod_in, bm) + [_weight_spec(w, lead, bn, col0 // bn)],
        out_specs=pl.BlockSpec((bm, bn), lambda j, i: (i, j)),
        out_shape=jax.ShapeDtypeStruct((m, n), out_dtype),
        scratch_shapes=_weight_scratch(w, bn) + [pltpu.VMEM((SUBLANES, bn), F32)],
        compiler_params=_params(_MM_SEM),
        name="mm_plain",
    )(xg, ssq, shift, w)


def _mm_swiglu(mod_in, w_in, w_out, lead=(), *, rows_per_batch, bm=1024, bn=512):
    xg, ssq, shift = mod_in
    m, k = xg.shape
    f = w_in.shape[-1] // 2
    nf, nm = f // bn, m // bm
    slab = f // (nf * nm)
    n_out = w_out.shape[-1]
    nl = len(lead)
    return pl.pallas_call(
        functools.partial(_mm_swiglu_kernel, per=rows_per_batch // bm),
        grid=(nf, nm),
        in_specs=_modulated_specs(mod_in, bm) + [
            _weight_spec(w_in, lead, bn),
            _weight_spec(w_in, lead, bn, nf),
            pl.BlockSpec((None,) * nl + (slab, n_out), lambda j, i: tuple(lead) + (j * nm + i, 0))],
        out_specs=[pl.BlockSpec((bm, bn), lambda j, i: (i, j)),
                   pl.BlockSpec((slab, n_out), lambda j, i: (j * nm + i, 0))],
        out_shape=[jax.ShapeDtypeStruct((m, f), BF16), jax.ShapeDtypeStruct((f, n_out), BF16)],
        scratch_shapes=_weight_scratch(w_in, bn, 2) + [pltpu.VMEM((SUBLANES, bn), F32)] * 2,
        compiler_params=_params(_MM_SEM),
        name="mm_swiglu",
    )(xg, ssq, shift, w_in, w_in, w_out)


def _mm_glu(y, w, b, lead=(), *, bm=1024, bn=1024):
    m, k = y.shape
    n = w.shape[-1]
    return pl.pallas_call(
        _mm_glu_kernel,
        grid=(n // bn, m // bm),
        in_specs=[pl.BlockSpec((bm, k), lambda j, i: (i, 0)),
                  _weight_spec(w, lead, bn),
                  pl.BlockSpec((bm, bn), lambda j, i: (i, j)),
                  pl.BlockSpec((1, bn), lambda j, i: (0, j))],
        out_specs=pl.BlockSpec((bm, bn), lambda j, i: (i, j)),
        out_shape=jax.ShapeDtypeStruct((m, n), BF16),
        scratch_shapes=_weight_scratch(w, bn),
        compiler_params=_params(_MM_SEM),
        name="mm_glu",
    )(y, w, y, b.reshape(1, n))


def _mm_rope(mod_in, w, cos, sin, *, out_scale, out_dtype, rows_per_batch, bm=1024, bn=1024):
    xg, ssq, shift = mod_in
    m, k = xg.shape
    n = w.shape[-1]
    bn = min(bn, n)
    return pl.pallas_call(
        functools.partial(_mm_rope_kernel, per=rows_per_batch // bm, out_scale=out_scale),
        grid=(n // bn, m // bm),
        in_specs=_modulated_specs(mod_in, bm) + [_weight_spec(w, (), bn),
                                             pl.BlockSpec((bm, LANES), lambda j, i: (i, 0)),
                                             pl.BlockSpec((bm, LANES), lambda j, i: (i, 0))],
        out_specs=pl.BlockSpec((bm, bn), lambda j, i: (i, j)),
        out_shape=jax.ShapeDtypeStruct((m, n), out_dtype),
        scratch_shapes=_weight_scratch(w, bn) + [pltpu.VMEM((SUBLANES, bn), F32)],
        compiler_params=_params(_MM_SEM),
        name="mm_rope",
    )(xg, ssq, shift, w, cos, sin)


def _mm_resid(x, w, lead, resid, gate, next_gs=(), *, weight, rows_per_batch, bm=512, bn=1024):
    m, k = x.shape
    n = w.shape[-1]
    per = rows_per_batch // bm
    nb = gate.shape[0]
    n_next = len(next_gs)
    staged = w.dtype != BF16
    tile = pl.BlockSpec((bm, bn), lambda j, i: (i, j))
    per_batch = pl.BlockSpec((None, 1, bn), lambda j, i: (i // per, 0, j))
    out_specs = [tile] * (1 + n_next)
    out_shape = [jax.ShapeDtypeStruct((m, n), F32)] + [jax.ShapeDtypeStruct((m, n), BF16)] * n_next
    if n_next:
        out_specs.append(pl.BlockSpec((None, bm, LANES), lambda j, i: (j, i, 0)))
        out_shape.append(jax.ShapeDtypeStruct((n // bn, m, LANES), F32))
    outs = pl.pallas_call(
        functools.partial(_mm_resid_kernel, weight=weight, n_next=n_next, staged=staged),
        grid=(n // bn, m // bm),
        in_specs=[pl.BlockSpec((bm, k), lambda j, i: (i, 0)),
                  _weight_spec(w, lead, bn), tile, per_batch] + [per_batch] * n_next,
        out_specs=out_specs,
        out_shape=out_shape,
        scratch_shapes=_weight_scratch(w, bn) if staged else [],
        compiler_params=_params(_MM_SEM),
        name="mm_resid",
    )(x, w, resid, gate.reshape(nb, 1, n), *[gs.reshape(nb, 1, n) for gs in next_gs])
    return outs[0], list(outs[1:1 + n_next]), (outs[-1] if n_next else None)


def _rope_kernel(pos_ref, invf_ref, sign_ref, cos_ref, sin_ref):
    ang = pos_ref[...].astype(F32) * invf_ref[...]
    cos_ref[...] = jnp.cos(ang)
    sin_ref[...] = jnp.sin(ang) * sign_ref[...]


def _rope_tables(positions, bl=1024):
    m = positions.size
    half = HEAD_DIM // 2
    inv_freq = 1.0 / (ROPE_THETA ** (jnp.arange(0, HEAD_DIM, 2, dtype=F32) / HEAD_DIM))
    invf = jnp.tile(inv_freq, LANES // half).reshape(1, LANES)
    sign = jnp.asarray(np.where(np.arange(LANES) < LANES // 2, -1.0, 1.0), F32).reshape(1, LANES)
    return pl.pallas_call(
        _rope_kernel,
        grid=(m // bl,),
        in_specs=[pl.BlockSpec((bl, 1), lambda i: (i, 0)),
                  pl.BlockSpec((1, LANES), lambda i: (0, 0)),
                  pl.BlockSpec((1, LANES), lambda i: (0, 0))],
        out_specs=[pl.BlockSpec((bl, LANES), lambda i: (i, 0))] * 2,
        out_shape=[jax.ShapeDtypeStruct((m, LANES), F32)] * 2,
        compiler_params=_params(("parallel",)),
        name="rope_tables",
    )(positions.reshape(m, 1), invf, sign)


def _pair_layout(w):
    k, n = w.shape
    half = HEAD_DIM // 2
    return w.reshape(k, n // LANES, 2, 2, half).swapaxes(2, 3).reshape(k, n)


def _s5_input_map(v, bblk_ref, bu_ref, *, tc, pitch, nb):
    half_w = 4 * LANES
    vb = v.reshape(nb * tc, S5_TILE).astype(BF16)
    for half in range(2):
        r = jnp.dot(vb, bblk_ref[:, half * 2 * half_w:(half + 1) * 2 * half_w],
                    preferred_element_type=F32)
        for b in range(nb):
            for l in range(8):
                bu_ref[l, pl.ds((half * nb + b) * pitch, tc), :] = (
                    r[b * tc:(b + 1) * tc, l * LANES:(l + 1) * LANES])


def _s5_scan(bu_ref, st_ref, a, xs, *, tc, pitch):
    xs = list(xs)
    for t in range(tc):
        for l in range(4):
            bur = bu_ref[l, pl.ds(t, SUBLANES, stride=pitch), :]
            bui = bu_ref[4 + l, pl.ds(t, SUBLANES, stride=pitch), :]
            xr, xi = xs[l], xs[4 + l]
            nr = a[l] * xr - a[4 + l] * xi + bur
            ni = a[l] * xi + a[4 + l] * xr + bui
            st_ref[l, pl.ds(t, SUBLANES, stride=pitch), :] = nr
            st_ref[4 + l, pl.ds(t, SUBLANES, stride=pitch), :] = ni
            xs[l], xs[4 + l] = nr, ni
    return xs


def _s5_output_map(st_ref, cblk_ref, v, d, *, tc, pitch, nb):
    half_w = 4 * LANES
    acc = None
    for half in range(2):
        blocks = []
        for b in range(nb):
            blocks.append(jnp.concatenate(
                [st_ref[l, pl.ds((half * nb + b) * pitch, tc), :] for l in range(8)], axis=1))
        s = jnp.concatenate(blocks, axis=0).astype(BF16)
        part = jnp.dot(s, cblk_ref[half * 2 * half_w:(half + 1) * 2 * half_w, :],
                       preferred_element_type=F32)
        acc = part if acc is None else acc + part
    y = _gelu_tanh(acc + d * v.reshape(nb * tc, S5_TILE))
    return y.reshape(nb, tc, S5_TILE)


def _s5_kernel(va_ref, vc_ref, bblk_ref, cblk_ref, a_ref, d_ref, y_ref,
               bu0_ref, bu1_ref, st0_ref, st1_ref, x_ref, *, tc, pitch, nb):
    s = pl.program_id(1)
    kw = dict(tc=tc, pitch=pitch, nb=nb)

    @pl.when(s == 0)
    def _():
        x_ref[...] = jnp.zeros_like(x_ref)
        bu1_ref[...] = jnp.zeros_like(bu1_ref)
        st0_ref[...] = jnp.zeros_like(st0_ref)

    bus, sts = (bu0_ref, bu1_ref), (st0_ref, st1_ref)
    a = [a_ref[l] for l in range(8)]
    d = d_ref[...]
    xs = [x_ref[l] for l in range(8)]
    for h in range(2):
        _s5_input_map(va_ref[:, h * tc:(h + 1) * tc, :], bblk_ref, bus[h], **kw)
        xs = _s5_scan(bus[1 - h], sts[1 - h], a, xs, tc=tc, pitch=pitch)
        y = _s5_output_map(sts[h], cblk_ref, vc_ref[:, h * tc:(h + 1) * tc, :], d, **kw)
        y_ref[:, h * tc:(h + 1) * tc, :] = y.astype(y_ref.dtype)
    for l in range(8):
        x_ref[l] = xs[l]


def _s5_discretise(a_re, a_im, b_re, b_im, c_re, c_im, log_dt, nb):
    g, p = a_re.shape
    gpt = S5_TILE // S5_GROUP
    nt = g // gpt
    dt = jnp.exp(log_dt)[:, None]
    mag = jnp.exp(a_re * dt)
    lb_re, lb_im = mag * jnp.cos(a_im * dt), mag * jnp.sin(a_im * dt)
    den = a_re * a_re + a_im * a_im
    k_re = ((lb_re - 1.0) * a_re + lb_im * a_im) / den
    k_im = (lb_im * a_re - (lb_re - 1.0) * a_im) / den
    bb_re = k_re[..., None] * b_re - k_im[..., None] * b_im
    bb_im = k_re[..., None] * b_im + k_im[..., None] * b_re
    g8 = gpt // 2
    nstate = 2 * gpt * p

    def state_cols(t):
        x = t.shape[-1]
        return t.reshape(nt, 2, g8, p, x).transpose(0, 4, 1, 2, 3).reshape(nt, x, 2, g8 * p)

    col = np.arange(nstate)
    col_group = col // (2 * g8 * p) * g8 + col % (g8 * p) // p
    own = jnp.asarray(np.arange(S5_TILE)[:, None] // S5_GROUP == col_group[None, :], F32)
    m_in = jnp.stack([state_cols(bb_re), state_cols(bb_im)], axis=3).reshape(nt, S5_GROUP, nstate)
    bblk = jnp.tile(m_in, (1, gpt, 1)) * own
    m_out = jnp.stack([state_cols(c_re.swapaxes(1, 2)), state_cols(-c_im.swapaxes(1, 2))],
                      axis=3).reshape(nt, S5_GROUP, nstate).swapaxes(1, 2)
    cblk = jnp.tile(m_out, (1, 1, gpt)) * own.T

    def a_slabs(lb):
        t = lb.reshape(nt, 2, 4, LANES)
        t = jnp.broadcast_to(t[:, :, None], (nt, 2, nb, 4, LANES))
        return t.transpose(0, 3, 1, 2, 4).reshape(nt, 4, 2 * nb, LANES)

    a = jnp.concatenate([a_slabs(lb_re), a_slabs(lb_im)], axis=1)
    return bblk.astype(BF16), cblk.astype(BF16), a


def _s5_core(v, bblk, cblk, a, d_skip, *, tc=128):
    nb, seq, d = v.shape
    assert 2 * nb == SUBLANES
    nt = d // S5_TILE
    steps = seq // (2 * tc) + 1
    pitch = tc + 4
    nstate = bblk.shape[2]
    slab = pltpu.VMEM((8, SUBLANES * pitch, LANES), F32)
    return pl.pallas_call(
        functools.partial(_s5_kernel, tc=tc, pitch=pitch, nb=nb),
        grid=(nt, steps),
        in_specs=[pl.BlockSpec((nb, 2 * tc, S5_TILE), lambda j, s: (0, jnp.minimum(s, steps - 2), j)),
                  pl.BlockSpec((nb, 2 * tc, S5_TILE), lambda j, s: (0, jnp.maximum(s - 1, 0), j)),
                  pl.BlockSpec((None, S5_TILE, nstate), lambda j, s: (j, 0, 0)),
                  pl.BlockSpec((None, nstate, S5_TILE), lambda j, s: (j, 0, 0)),
                  pl.BlockSpec((None, 8, SUBLANES, LANES), lambda j, s: (j, 0, 0, 0)),
                  pl.BlockSpec((1, S5_TILE), lambda j, s: (0, j))],
        out_specs=pl.BlockSpec((nb, 2 * tc, S5_TILE), lambda j, s: (0, jnp.maximum(s - 1, 0), j)),
        out_shape=jax.ShapeDtypeStruct((nb, seq, d), BF16),
        scratch_shapes=[slab, slab, slab, slab, pltpu.VMEM((8, SUBLANES, LANES), F32)],
        compiler_params=_params(("arbitrary", "arbitrary")),
        name="s5_core",
    )(v, v, bblk, cblk, a, d_skip.reshape(1, d))


def _attn_kernel(sink_ref, q_ref, kp_ref, kc_ref, vp_ref, vc_ref, o_ref, *, n_kv):
    blk = q_ref.shape[0]
    lo = jnp.where(pl.program_id(1) == 0, blk, 0)
    qi = lax.broadcasted_iota(jnp.int32, (blk, 2 * blk), 0)
    kj = lax.broadcasted_iota(jnp.int32, (blk, 2 * blk), 1)
    valid = (kj > qi) & (kj <= qi + blk) & (kj >= lo)
    lane = lax.broadcasted_iota(jnp.int32, (2 * blk, LANES), 1)
    ones = jnp.ones((2 * blk, LANES), BF16)
    half = HEAD_DIM // 2
    pairs = Q_PER_KV // 2

    for kvh in range(n_kv):
        grp, slot = kvh // 2, kvh % 2
        kcat = jnp.concatenate([kp_ref[:, grp * LANES:(grp + 1) * LANES],
                                kc_ref[:, grp * LANES:(grp + 1) * LANES]], axis=0)
        vcat = jnp.concatenate([vp_ref[:, grp * LANES:(grp + 1) * LANES],
                                vc_ref[:, grp * LANES:(grp + 1) * LANES]], axis=0)
        kexp, vexp = [], []
        for s in range(2):
            kk = kcat if s == slot else pltpu.roll(kcat, (half * (s - slot)) % LANES, axis=1)
            kexp.append(jnp.where((lane // half) % 2 == s, kk, 0.0).astype(BF16))
            vv = vcat if s == slot else pltpu.roll(vcat, (HEAD_DIM * (s - slot)) % LANES, axis=1)
            vexp.append(jnp.where(lane // HEAD_DIM == s, vv, 0.0).astype(BF16))
        base = kvh * pairs
        qg = jnp.concatenate([q_ref[:, (base + p) * LANES:(base + p + 1) * LANES]
                              for p in range(pairs)], axis=0)
        out = None
        for s in range(2):
            sc = lax.dot_general(qg, kexp[s], (((1,), (1,)), ((), ())),
                                 preferred_element_type=F32)
            es, sink_terms = [], []
            for p in range(pairs):
                sink = sink_ref[kvh * Q_PER_KV + 2 * p + s]
                sp = jnp.where(valid, sc[p * blk:(p + 1) * blk], -1e30)
                mx = jnp.maximum(jnp.max(sp, axis=-1, keepdims=True), sink)
                es.append(jnp.exp(sp - mx).astype(BF16))
                sink_terms.append(jnp.exp(sink - mx))
            em = jnp.concatenate(es, axis=0)
            den = jnp.dot(em, ones, preferred_element_type=F32) + jnp.concatenate(sink_terms, axis=0)
            part = jnp.dot(em, vexp[s], preferred_element_type=F32) / den
            out = part if out is None else out + part
        for p in range(pairs):
            o_ref[:, (base + p) * LANES:(base + p + 1) * LANES] = (
                out[p * blk:(p + 1) * blk].astype(o_ref.dtype))


def _attention(q, k, v, sinks, *, nbatch, blk=WINDOW):
    m, dq = q.shape
    dkv = k.shape[1]
    nblk = m // nbatch // blk
    cur = lambda b, n: (b * nblk + n, 0)
    prev = lambda b, n: (b * nblk + jnp.maximum(n - 1, 0), 0)
    return pl.pallas_call(
        functools.partial(_attn_kernel, n_kv=dkv // HEAD_DIM),
        grid=(nbatch, nblk),
        in_specs=[pl.BlockSpec(memory_space=pltpu.SMEM),
                  pl.BlockSpec((blk, dq), cur),
                  pl.BlockSpec((blk, dkv), prev),
                  pl.BlockSpec((blk, dkv), cur),
                  pl.BlockSpec((blk, dkv), prev),
                  pl.BlockSpec((blk, dkv), cur)],
        out_specs=pl.BlockSpec((blk, dq), cur),
        out_shape=jax.ShapeDtypeStruct((m, dq), BF16),
        compiler_params=_params(("parallel", "arbitrary")),
        name="attention",
    )(sinks, q, k, k, v, v)


def kernel(x, c, positions, norm_g, w_ada, b_ada, w_ff_in, w_ff_out, s5_w_in, s5_a_re, s5_a_im, s5_b_re, s5_b_im, s5_c_re, s5_c_im, s5_d, s5_log_dt, s5_w_glu, s5_b_glu, s5_w_out, kv_norm_g, w_ada_kv, b_ada_kv, w_kv, attn_w_q, attn_sinks, attn_w_o, final_norm_g):
    nbatch, seq, d = x.shape
    depth = norm_g.shape[0]
    n_s5 = s5_w_in.shape[0]
    m = nbatch * seq
    kvw = w_kv.shape[1] // 2

    c_pad = jnp.zeros((SUBLANES, d), F32).at[:nbatch].set(c)
    mods = _ada(c_pad, w_ada, b_ada)[:, :nbatch]
    mod_kv = _ada(c_pad, w_ada_kv[None], b_ada_kv[None])[0, :nbatch]
    cos, sin = _rope_tables(positions)

    per_batch = dict(rows_per_batch=seq)
    mod = mods.reshape(depth, nbatch, N_SUBLAYERS, 3, d)
    shift, scale, gate = mod[:, :, :, 0], mod[:, :, :, 1], mod[:, :, :, 2]

    def gs_of(layer, sub):
        return norm_g[layer, sub][None, :] * (1.0 + scale[layer, :, sub])

    def shift_rows(sh):
        return jnp.zeros((SUBLANES, d), F32).at[:nbatch].set(sh).astype(BF16)

    def ffn(mod_in, xs, layer, which, next_gs):
        act, w_out = _mm_swiglu(mod_in, w_ff_in, w_ff_out, (layer, which), **per_batch)
        return _mm_resid(act, w_out, (), xs, gate[layer, :, 2 * which], next_gs,
                         weight=0.5, **per_batch)

    xs = x.reshape(m, d)
    xg, ssq = _prep(xs, gs_of(0, 0), **per_batch)
    k_rot = v_kv = None
    for layer in range(depth):
        xs, (xg,), ssq = ffn((xg, ssq, shift_rows(shift[layer, :, 0])), xs, layer, 0, [gs_of(layer, 1)])

        mix_in = (xg, ssq, shift_rows(shift[layer, :, 1]))
        if layer < n_s5:
            i = layer
            v = _mm_plain(mix_in, s5_w_in, (i,), out_dtype=F32, **per_batch)
            bblk, cblk, a = _s5_discretise(s5_a_re[i], s5_a_im[i], s5_b_re[i], s5_b_im[i],
                                           s5_c_re[i], s5_c_im[i], s5_log_dt[i], nbatch)
            y = _s5_core(v.reshape(nbatch, seq, d), bblk, cblk, a, s5_d[i].reshape(-1)).reshape(m, d)
            mixed = _mm_glu(y, s5_w_glu, s5_b_glu[i], (i,))
            w_mix_out, lead = s5_w_out, (i,)
        else:
            j = layer - n_s5
            q = _mm_rope(mix_in, _pair_layout(attn_w_q[j]), cos, sin, out_scale=HEAD_DIM ** -0.5,
                         out_dtype=BF16, **per_batch)
            mixed = _attention(q, k_rot, v_kv, attn_sinks[j], nbatch=nbatch)
            w_mix_out, lead = attn_w_o, (j,)
        xs, (xg,), ssq = _mm_resid(mixed, w_mix_out, lead, xs, gate[layer, :, 1], [gs_of(layer, 2)],
                                   weight=1.0, **per_batch)

        next_gs = [gs_of(layer + 1, 0)] if layer + 1 < depth else []
        if layer == n_s5 - 1:
            next_gs.append(kv_norm_g[None, :] * (1.0 + mod_kv[:, d:]))
        xs, xgs, ssq = ffn((xg, ssq, shift_rows(shift[layer, :, 2])), xs, layer, 1, next_gs)
        if layer == n_s5 - 1:
            kv_in = (xgs[-1], ssq, shift_rows(mod_kv[:, :d]))
            k_rot = _mm_rope(kv_in, _pair_layout(w_kv[:, :kvw]), cos, sin, out_scale=1.0,
                             out_dtype=F32, **per_batch)
            v_kv = _mm_plain(kv_in, w_kv, (), out_dtype=F32, ncols=kvw, col0=kvw, **per_batch)
        if layer + 1 < depth:
            xg = xgs[0]

    return _final_norm(xs, final_norm_g).reshape(nbatch, seq, d)
```

```python
import functools
import math

import numpy as np
import jax
import jax.numpy as jnp
from jax import lax
from jax.experimental import pallas as pl
from jax.experimental.pallas import tpu as pltpu

F32 = jnp.float32
BF16 = jnp.bfloat16

RMS_EPS = 1e-6
N_SUBLAYERS = 3
S5_GROUP = 16
S5_STATE = 64
HEAD_DIM = 64
Q_PER_KV = 8
WINDOW = 128
ROPE_THETA = 10000.0

LANES = 128
SUBLANES = 8
S5_TILE = 256
VMEM_LIMIT = 52 * 1024 * 1024


def _params(sem):
    return pltpu.CompilerParams(dimension_semantics=sem, vmem_limit_bytes=VMEM_LIMIT)


def _sigmoid(x):
    return 1.0 / (1.0 + jnp.exp(-x))


def _gelu_tanh(x):
    return 0.5 * x * (1.0 + jnp.tanh(math.sqrt(2.0 / math.pi) * (x + 0.044715 * (x * x * x))))


def _ada_kernel(c_ref, w_ref, b_ref, o_ref):
    c = c_ref[...]
    ca = (c * _sigmoid(c)).astype(BF16)
    o_ref[...] = jnp.dot(ca, w_ref[...].astype(BF16), preferred_element_type=F32) + b_ref[...]


def _ada(c_pad, w, b, bn=1024):
    s, d, n = w.shape
    return pl.pallas_call(
        _ada_kernel,
        grid=(s, n // bn),
        in_specs=[pl.BlockSpec((SUBLANES, d), lambda i, j: (0, 0)),
                  pl.BlockSpec((None, d, bn), lambda i, j: (i, 0, j)),
                  pl.BlockSpec((None, 1, bn), lambda i, j: (i, 0, j))],
        out_specs=pl.BlockSpec((None, SUBLANES, bn), lambda i, j: (i, 0, j)),
        out_shape=jax.ShapeDtypeStruct((s, SUBLANES, n), F32),
        compiler_params=_params(("parallel", "parallel")),
        name="ada",
    )(c_pad, w, b.reshape(s, 1, n))


def _lane_block_sums(sq):
    acc = sq[:, :LANES]
    for j in range(1, sq.shape[1] // LANES):
        acc = acc + sq[:, j * LANES:(j + 1) * LANES]
    return acc


def _row_rsqrt(ssq_ref, width):
    part = ssq_ref[0]
    for p in range(1, ssq_ref.shape[0]):
        part = part + ssq_ref[p]
    ms = jnp.sum(part, axis=-1, keepdims=True) * (1.0 / width)
    return lax.rsqrt(ms + RMS_EPS)


def _prep_kernel(x_ref, gs_ref, xg_ref, ssq_ref):
    x = x_ref[...]
    xg_ref[...] = (x * gs_ref[...]).astype(xg_ref.dtype)
    ssq_ref[0] = _lane_block_sums(x * x)


def _prep(x, gs, *, rows_per_batch, bl=512):
    m, d = x.shape
    per = rows_per_batch // bl
    nb = gs.shape[0]
    return pl.pallas_call(
        _prep_kernel,
        grid=(m // bl,),
        in_specs=[pl.BlockSpec((bl, d), lambda i: (i, 0)),
                  pl.BlockSpec((None, 1, d), lambda i: (i // per, 0, 0))],
        out_specs=[pl.BlockSpec((bl, d), lambda i: (i, 0)),
                   pl.BlockSpec((1, bl, LANES), lambda i: (0, i, 0))],
        out_shape=[jax.ShapeDtypeStruct((m, d), BF16), jax.ShapeDtypeStruct((1, m, LANES), F32)],
        compiler_params=_params(("parallel",)),
        name="prep",
    )(x, gs.reshape(nb, 1, d))


def _final_norm_kernel(x_ref, g_ref, o_ref):
    x = x_ref[...]
    ms = jnp.mean(x * x, axis=-1, keepdims=True)
    o_ref[...] = x * lax.rsqrt(ms + RMS_EPS) * g_ref[...]


def _final_norm(x, g, bl=512):
    m, d = x.shape
    return pl.pallas_call(
        _final_norm_kernel,
        grid=(m // bl,),
        in_specs=[pl.BlockSpec((bl, d), lambda i: (i, 0)),
                  pl.BlockSpec((1, d), lambda i: (0, 0))],
        out_specs=pl.BlockSpec((bl, d), lambda i: (i, 0)),
        out_shape=jax.ShapeDtypeStruct((m, d), F32),
        compiler_params=_params(("parallel",)),
        name="final_norm",
    )(x, g.reshape(1, d))


def _weight_spec(w, lead, bn, col0=0):
    k = w.shape[-2]
    return pl.BlockSpec((None,) * len(lead) + (k, bn), lambda j, i: tuple(lead) + (0, col0 + j))


def _weight_scratch(w, bn, count=1):
    return [pltpu.VMEM((w.shape[-2], bn), BF16)] * count


def _stage_weight(w_ref, wb_ref, shift_ref=None, sw_ref=None):
    @pl.when(pl.program_id(1) == 0)
    def _():
        wb_ref[...] = w_ref[...].astype(BF16)
        if sw_ref is not None:
            sw_ref[...] = jnp.dot(shift_ref[...], wb_ref[...], preferred_element_type=F32)


def _modulated_dot(xg_ref, wb_ref, ssq_ref, sw_ref, per):
    b = pl.program_id(1) // per
    acc = jnp.dot(xg_ref[...], wb_ref[...], preferred_element_type=F32)
    return _row_rsqrt(ssq_ref, xg_ref.shape[1]) * acc + sw_ref[pl.ds(b, 1), :]


def _mm_plain_kernel(x_ref, ssq_ref, shift_ref, w_ref, o_ref, wb_ref, sw_ref, *, per):
    _stage_weight(w_ref, wb_ref, shift_ref, sw_ref)
    o_ref[...] = _modulated_dot(x_ref, wb_ref, ssq_ref, sw_ref, per).astype(o_ref.dtype)


def _mm_swiglu_kernel(x_ref, ssq_ref, shift_ref, wg_ref, wu_ref, wo_ref, o_ref, wob_ref,
                      wgb_ref, wub_ref, swg_ref, swu_ref, *, per):
    _stage_weight(wg_ref, wgb_ref, shift_ref, swg_ref)
    _stage_weight(wu_ref, wub_ref, shift_ref, swu_ref)
    g = _modulated_dot(x_ref, wgb_ref, ssq_ref, swg_ref, per)
    u = _modulated_dot(x_ref, wub_ref, ssq_ref, swu_ref, per)
    o_ref[...] = (g * _sigmoid(g) * u).astype(o_ref.dtype)
    wob_ref[...] = wo_ref[...].astype(wob_ref.dtype)


def _mm_glu_kernel(x_ref, w_ref, y_ref, b_ref, o_ref, wb_ref):
    _stage_weight(w_ref, wb_ref)
    acc = jnp.dot(x_ref[...], wb_ref[...], preferred_element_type=F32)
    o_ref[...] = (y_ref[...].astype(F32) * _sigmoid(acc + b_ref[...])).astype(o_ref.dtype)


def _mm_rope_kernel(x_ref, ssq_ref, shift_ref, w_ref, cos_ref, sin_lo_ref, sin_hi_ref, o_ref,
                    wb_ref, sw_ref, *, per, out_scale):
    _stage_weight(w_ref, wb_ref, shift_ref, sw_ref)
    acc = _modulated_dot(x_ref, wb_ref, ssq_ref, sw_ref, per)
    cos, sin_lo, sin_hi = cos_ref[...], sin_lo_ref[...], sin_hi_ref[...]
    half = HEAD_DIM // 2
    for j in range(acc.shape[1] // LANES):
        a = acc[:, j * LANES:(j + 1) * LANES]
        r = (a * cos + pltpu.roll(a, LANES - half, axis=1) * sin_lo
             + pltpu.roll(a, half, axis=1) * sin_hi)
        o_ref[:, j * LANES:(j + 1) * LANES] = (r * out_scale).astype(o_ref.dtype)


def _mm_resid_kernel(x_ref, w_ref, r_ref, gate_ref, *rest, weight, n_next, staged):
    gs_refs, o_ref, xg_refs = rest[:n_next], rest[n_next], rest[n_next + 1:2 * n_next + 1]
    if staged:
        wb_ref = rest[-1]
        _stage_weight(w_ref, wb_ref)
    else:
        wb_ref = w_ref
    acc = jnp.dot(x_ref[...], wb_ref[...], preferred_element_type=F32)
    x_new = r_ref[...] + (weight * (1.0 + gate_ref[...])) * acc
    o_ref[...] = x_new
    if not n_next:
        return
    for gs_ref, xg_ref in zip(gs_refs, xg_refs):
        xg_ref[...] = (x_new * gs_ref[...]).astype(xg_ref.dtype)
    ssq_ref = rest[2 * n_next + 1]
    ssq_ref[...] = _lane_block_sums(x_new * x_new)


_MM_SEM = ("arbitrary", "arbitrary")


def _modulated_specs(mod_in, bm):
    xg, ssq, _ = mod_in
    k = xg.shape[1]
    return [pl.BlockSpec((bm, k), lambda j, i: (i, 0)),
            pl.BlockSpec((ssq.shape[0], bm, LANES), lambda j, i: (0, i, 0)),
            pl.BlockSpec((SUBLANES, k), lambda j, i: (0, 0))]


def _mm_plain(mod_in, w, lead=(), *, out_dtype, rows_per_batch, ncols=None, col0=0, bm=1024, bn=1024):
    xg, ssq, shift = mod_in
    m, k = xg.shape
    n = w.shape[-1] if ncols is None else ncols
    bn = min(bn, n)
    return pl.pallas_call(
        functools.partial(_mm_plain_kernel, per=rows_per_batch // bm),
        grid=(n // bn, m // bm),
        in_specs=_modulated_specs(mod_in, bm) + [_weight_spec(w, lead, bn, col0 // bn)],
        out_specs=pl.BlockSpec((bm, bn), lambda j, i: (i, j)),
        out_shape=jax.ShapeDtypeStruct((m, n), out_dtype),
        scratch_shapes=_weight_scratch(w, bn) + [pltpu.VMEM((SUBLANES, bn), F32)],
        compiler_params=_params(_MM_SEM),
        name="mm_plain",
    )(xg, ssq, shift, w)


def _mm_swiglu(mod_in, w_in, w_out, lead=(), *, rows_per_batch, bm=1024, bn=512):
    xg, ssq, shift = mod_in
    m, k = xg.shape
    f = w_in.shape[-1] // 2
    nf, nm = f // bn, m // bm
    slab = f // (nf * nm)
    n_out = w_out.shape[-1]
    nl = len(lead)
    return pl.pallas_call(
        functools.partial(_mm_swiglu_kernel, per=rows_per_batch // bm),
        grid=(nf, nm),
        in_specs=_modulated_specs(mod_in, bm) + [
            _weight_spec(w_in, lead, bn),
            _weight_spec(w_in, lead, bn, nf),
            pl.BlockSpec((None,) * nl + (slab, n_out), lambda j, i: tuple(lead) + (j * nm + i, 0))],
        out_specs=[pl.BlockSpec((bm, bn), lambda j, i: (i, j)),
                   pl.BlockSpec((slab, n_out), lambda j, i: (j * nm + i, 0))],
        out_shape=[jax.ShapeDtypeStruct((m, f), BF16), jax.ShapeDtypeStruct((f, n_out), BF16)],
        scratch_shapes=_weight_scratch(w_in, bn, 2) + [pltpu.VMEM((SUBLANES, bn), F32)] * 2,
        compiler_params=_params(_MM_SEM),
        name="mm_swiglu",
    )(xg, ssq, shift, w_in, w_in, w_out)


def _mm_glu(y, w, b, lead=(), *, bm=1024, bn=1024):
    m, k = y.shape
    n = w.shape[-1]
    return pl.pallas_call(
        _mm_glu_kernel,
        grid=(n // bn, m // bm),
        in_specs=[pl.BlockSpec((bm, k), lambda j, i: (i, 0)),
                  _weight_spec(w, lead, bn),
                  pl.BlockSpec((bm, bn), lambda j, i: (i, j)),
                  pl.BlockSpec((1, bn), lambda j, i: (0, j))],
        out_specs=pl.BlockSpec((bm, bn), lambda j, i: (i, j)),
        out_shape=jax.ShapeDtypeStruct((m, n), BF16),
        scratch_shapes=_weight_scratch(w, bn),
        compiler_params=_params(_MM_SEM),
        name="mm_glu",
    )(y, w, y, b.reshape(1, n))


def _mm_rope(mod_in, w, lead, rope, *, out_scale, out_dtype, rows_per_batch, ncols=None,
             bm=1024, bn=1024):
    xg, ssq, shift = mod_in
    m, k = xg.shape
    n = w.shape[-1] if ncols is None else ncols
    bn = min(bn, n)
    table = pl.BlockSpec((bm, LANES), lambda j, i: (i, 0))
    return pl.pallas_call(
        functools.partial(_mm_rope_kernel, per=rows_per_batch // bm, out_scale=out_scale),
        grid=(n // bn, m // bm),
        in_specs=_modulated_specs(mod_in, bm) + [_weight_spec(w, lead, bn), table, table, table],
        out_specs=pl.BlockSpec((bm, bn), lambda j, i: (i, j)),
        out_shape=jax.ShapeDtypeStruct((m, n), out_dtype),
        scratch_shapes=_weight_scratch(w, bn) + [pltpu.VMEM((SUBLANES, bn), F32)],
        compiler_params=_params(_MM_SEM),
        name="mm_rope",
    )(xg, ssq, shift, w, *rope)


def _mm_resid(x, w, lead, resid, gate, next_gs=(), *, weight, rows_per_batch, bm=512, bn=1024):
    m, k = x.shape
    n = w.shape[-1]
    per = rows_per_batch // bm
    nb = gate.shape[0]
    n_next = len(next_gs)
    staged = w.dtype != BF16
    tile = pl.BlockSpec((bm, bn), lambda j, i: (i, j))
    per_batch = pl.BlockSpec((None, 1, bn), lambda j, i: (i // per, 0, j))
    out_specs = [tile] * (1 + n_next)
    out_shape = [jax.ShapeDtypeStruct((m, n), F32)] + [jax.ShapeDtypeStruct((m, n), BF16)] * n_next
    if n_next:
        out_specs.append(pl.BlockSpec((None, bm, LANES), lambda j, i: (j, i, 0)))
        out_shape.append(jax.ShapeDtypeStruct((n // bn, m, LANES), F32))
    outs = pl.pallas_call(
        functools.partial(_mm_resid_kernel, weight=weight, n_next=n_next, staged=staged),
        grid=(n // bn, m // bm),
        in_specs=[pl.BlockSpec((bm, k), lambda j, i: (i, 0)),
                  _weight_spec(w, lead, bn), tile, per_batch] + [per_batch] * n_next,
        out_specs=out_specs,
        out_shape=out_shape,
        scratch_shapes=_weight_scratch(w, bn) if staged else [],
        compiler_params=_params(_MM_SEM),
        name="mm_resid",
    )(x, w, resid, gate.reshape(nb, 1, n), *[gs.reshape(nb, 1, n) for gs in next_gs])
    return outs[0], list(outs[1:1 + n_next]), (outs[-1] if n_next else None)


def _rope_kernel(pos_ref, invf_ref, lo_ref, cos_ref, sin_lo_ref, sin_hi_ref):
    ang = pos_ref[...].astype(F32) * invf_ref[...]
    sin = jnp.sin(ang)
    lo = lo_ref[...]
    cos_ref[...] = jnp.cos(ang)
    sin_lo_ref[...] = -sin * lo
    sin_hi_ref[...] = sin * (1.0 - lo)


def _rope_tables(positions, bl=1024):
    m = positions.size
    half = HEAD_DIM // 2
    inv_freq = 1.0 / (ROPE_THETA ** (jnp.arange(0, HEAD_DIM, 2, dtype=F32) / HEAD_DIM))
    invf = jnp.tile(inv_freq, LANES // half).reshape(1, LANES)
    lo = jnp.asarray(np.arange(LANES) % HEAD_DIM < half, F32).reshape(1, LANES)
    return pl.pallas_call(
        _rope_kernel,
        grid=(m // bl,),
        in_specs=[pl.BlockSpec((bl, 1), lambda i: (i, 0)),
                  pl.BlockSpec((1, LANES), lambda i: (0, 0)),
                  pl.BlockSpec((1, LANES), lambda i: (0, 0))],
        out_specs=[pl.BlockSpec((bl, LANES), lambda i: (i, 0))] * 3,
        out_shape=[jax.ShapeDtypeStruct((m, LANES), F32)] * 3,
        compiler_params=_params(("parallel",)),
        name="rope_tables",
    )(positions.reshape(m, 1), invf, lo)


def _s5_input_map(v, bblk_ref, bu_ref, *, tc, pitch, nb):
    half_w = 4 * LANES
    vb = v.reshape(nb * tc, S5_TILE).astype(BF16)
    for half in range(2):
        r = jnp.dot(vb, bblk_ref[:, half * 2 * half_w:(half + 1) * 2 * half_w],
                    preferred_element_type=F32)
        for b in range(nb):
            for l in range(8):
                bu_ref[l, pl.ds((half * nb + b) * pitch, tc), :] = (
                    r[b * tc:(b + 1) * tc, l * LANES:(l + 1) * LANES])


def _s5_scan(bu_ref, st_ref, a, xs, *, tc, pitch):
    xs = list(xs)
    for t in range(tc):
        for l in range(4):
            bur = bu_ref[l, pl.ds(t, SUBLANES, stride=pitch), :]
            bui = bu_ref[4 + l, pl.ds(t, SUBLANES, stride=pitch), :]
            xr, xi = xs[l], xs[4 + l]
            nr = a[l] * xr - a[4 + l] * xi + bur
            ni = a[l] * xi + a[4 + l] * xr + bui
            st_ref[l, pl.ds(t, SUBLANES, stride=pitch), :] = nr
            st_ref[4 + l, pl.ds(t, SUBLANES, stride=pitch), :] = ni
            xs[l], xs[4 + l] = nr, ni
    return xs


def _s5_output_map(st_ref, cblk_ref, v, d, *, tc, pitch, nb):
    half_w = 4 * LANES
    acc = None
    for half in range(2):
        blocks = []
        for b in range(nb):
            blocks.append(jnp.concatenate(
                [st_ref[l, pl.ds((half * nb + b) * pitch, tc), :] for l in range(8)], axis=1))
        s = jnp.concatenate(blocks, axis=0).astype(BF16)
        part = jnp.dot(s, cblk_ref[half * 2 * half_w:(half + 1) * 2 * half_w, :],
                       preferred_element_type=F32)
        acc = part if acc is None else acc + part
    y = _gelu_tanh(acc + d * v.reshape(nb * tc, S5_TILE))
    return y.reshape(nb, tc, S5_TILE)


def _s5_kernel(va_ref, vc_ref, bblk_ref, cblk_ref, a_ref, d_ref, y_ref,
               bu0_ref, bu1_ref, st0_ref, st1_ref, x_ref, *, tc, pitch, nb):
    s = pl.program_id(1)
    kw = dict(tc=tc, pitch=pitch, nb=nb)

    @pl.when(s == 0)
    def _():
        x_ref[...] = jnp.zeros_like(x_ref)
        bu1_ref[...] = jnp.zeros_like(bu1_ref)
        st0_ref[...] = jnp.zeros_like(st0_ref)

    bus, sts = (bu0_ref, bu1_ref), (st0_ref, st1_ref)
    a = [a_ref[l] for l in range(8)]
    d = d_ref[...]
    xs = [x_ref[l] for l in range(8)]
    for h in range(2):
        _s5_input_map(va_ref[:, h * tc:(h + 1) * tc, :], bblk_ref, bus[h], **kw)
        xs = _s5_scan(bus[1 - h], sts[1 - h], a, xs, tc=tc, pitch=pitch)
        y = _s5_output_map(sts[h], cblk_ref, vc_ref[:, h * tc:(h + 1) * tc, :], d, **kw)
        y_ref[:, h * tc:(h + 1) * tc, :] = y.astype(y_ref.dtype)
    for l in range(8):
        x_ref[l] = xs[l]


def _s5_discretise(a_re, a_im, b_re, b_im, c_re, c_im, log_dt, nb):
    g, p = a_re.shape
    gpt = S5_TILE // S5_GROUP
    nt = g // gpt
    dt = jnp.exp(log_dt)[:, None]
    mag = jnp.exp(a_re * dt)
    lb_re, lb_im = mag * jnp.cos(a_im * dt), mag * jnp.sin(a_im * dt)
    den = a_re * a_re + a_im * a_im
    k_re = ((lb_re - 1.0) * a_re + lb_im * a_im) / den
    k_im = (lb_im * a_re - (lb_re - 1.0) * a_im) / den
    bb_re = k_re[..., None] * b_re - k_im[..., None] * b_im
    bb_im = k_re[..., None] * b_im + k_im[..., None] * b_re
    g8 = gpt // 2
    nstate = 2 * gpt * p

    def state_cols(t):
        x = t.shape[-1]
        return t.reshape(nt, 2, g8, p, x).transpose(0, 4, 1, 2, 3).reshape(nt, x, 2, g8 * p)

    col = np.arange(nstate)
    col_group = col // (2 * g8 * p) * g8 + col % (g8 * p) // p
    own = jnp.asarray(np.arange(S5_TILE)[:, None] // S5_GROUP == col_group[None, :], F32)
    m_in = jnp.stack([state_cols(bb_re), state_cols(bb_im)], axis=3).reshape(nt, S5_GROUP, nstate)
    bblk = jnp.tile(m_in, (1, gpt, 1)) * own
    m_out = jnp.stack([state_cols(c_re.swapaxes(1, 2)), state_cols(-c_im.swapaxes(1, 2))],
                      axis=3).reshape(nt, S5_GROUP, nstate).swapaxes(1, 2)
    cblk = jnp.tile(m_out, (1, 1, gpt)) * own.T

    def a_slabs(lb):
        t = lb.reshape(nt, 2, 4, LANES)
        t = jnp.broadcast_to(t[:, :, None], (nt, 2, nb, 4, LANES))
        return t.transpose(0, 3, 1, 2, 4).reshape(nt, 4, 2 * nb, LANES)

    a = jnp.concatenate([a_slabs(lb_re), a_slabs(lb_im)], axis=1)
    return bblk.astype(BF16), cblk.astype(BF16), a


def _s5_core(v, bblk, cblk, a, d_skip, *, tc=128):
    nb, seq, d = v.shape
    assert 2 * nb == SUBLANES
    nt = d // S5_TILE
    steps = seq // (2 * tc) + 1
    pitch = tc + 4
    nstate = bblk.shape[2]
    slab = pltpu.VMEM((8, SUBLANES * pitch, LANES), F32)
    return pl.pallas_call(
        functools.partial(_s5_kernel, tc=tc, pitch=pitch, nb=nb),
        grid=(nt, steps),
        in_specs=[pl.BlockSpec((nb, 2 * tc, S5_TILE), lambda j, s: (0, jnp.minimum(s, steps - 2), j)),
                  pl.BlockSpec((nb, 2 * tc, S5_TILE), lambda j, s: (0, jnp.maximum(s - 1, 0), j)),
                  pl.BlockSpec((None, S5_TILE, nstate), lambda j, s: (j, 0, 0)),
                  pl.BlockSpec((None, nstate, S5_TILE), lambda j, s: (j, 0, 0)),
                  pl.BlockSpec((None, 8, SUBLANES, LANES), lambda j, s: (j, 0, 0, 0)),
                  pl.BlockSpec((1, S5_TILE), lambda j, s: (0, j))],
        out_specs=pl.BlockSpec((nb, 2 * tc, S5_TILE), lambda j, s: (0, jnp.maximum(s - 1, 0), j)),
        out_shape=jax.ShapeDtypeStruct((nb, seq, d), BF16),
        scratch_shapes=[slab, slab, slab, slab, pltpu.VMEM((8, SUBLANES, LANES), F32)],
        compiler_params=_params(("arbitrary", "arbitrary")),
        name="s5_core",
    )(v, v, bblk, cblk, a, d_skip.reshape(1, d))


def _attn_kernel(sink_ref, bias_ref, q_ref, kp_ref, kc_ref, vp_ref, vc_ref, o_ref, *, n_kv):
    blk = q_ref.shape[0]
    lane = lax.broadcasted_iota(jnp.int32, (2 * blk, LANES), 1)
    pairs = Q_PER_KV // 2
    nt_dims = (((1,), (1,)), ((), ()))

    def on_slot(x, slot, s):
        x = x if s == slot else pltpu.roll(x, HEAD_DIM, axis=1)
        return jnp.where(lane // HEAD_DIM == s, x, 0.0)

    for kvh in range(n_kv):
        grp, slot = kvh // 2, kvh % 2
        cols = slice(grp * LANES, (grp + 1) * LANES)
        kcat = jnp.concatenate([kp_ref[:, cols], kc_ref[:, cols]], axis=0)
        vcat = jnp.concatenate([vp_ref[:, cols], vc_ref[:, cols]], axis=0)
        base = kvh * pairs
        qg = jnp.concatenate([q_ref[:, (base + p) * LANES:(base + p + 1) * LANES]
                              for p in range(pairs)], axis=0)
        out_t = None
        for s in range(2):
            kexp = on_slot(kcat, slot, s).astype(BF16)
            vexp_t = on_slot(vcat, slot, s).T.astype(BF16)
            scores = lax.dot_general(kexp, qg, nt_dims, preferred_element_type=F32)
            es, rdens = [], []
            for p in range(pairs):
                sink = sink_ref[kvh * Q_PER_KV + 2 * p + s]
                sc = scores[:, p * blk:(p + 1) * blk] + bias_ref[...]
                mx = jnp.maximum(jnp.max(sc, axis=0, keepdims=True), sink)
                e = jnp.exp2(sc - mx)
                den = jnp.sum(e, axis=0, keepdims=True) + jnp.exp2(sink - mx)
                es.append(e.astype(BF16))
                rdens.append(1.0 / den)
            part = jnp.dot(vexp_t, jnp.concatenate(es, axis=1),
                           preferred_element_type=F32) * jnp.concatenate(rdens, axis=1)
            out_t = part if out_t is None else out_t + part
        out = out_t.T
        for p in range(pairs):
            o_ref[:, (base + p) * LANES:(base + p + 1) * LANES] = (
                out[p * blk:(p + 1) * blk].astype(o_ref.dtype))


def _attention(q, k, v, sinks, *, nbatch, blk=WINDOW):
    m, dq = q.shape
    dkv = k.shape[1]
    nblk = m // nbatch // blk
    kj, qi = np.arange(2 * blk)[:, None], np.arange(blk)[None, :]
    window = (kj > qi) & (kj <= qi + blk)
    bias = np.where(np.stack([window & (kj >= blk), window]), 0.0, -1e30).astype(np.float32)
    cur = lambda b, n: (b * nblk + n, 0)
    prev = lambda b, n: (b * nblk + jnp.maximum(n - 1, 0), 0)
    return pl.pallas_call(
        functools.partial(_attn_kernel, n_kv=dkv // HEAD_DIM),
        grid=(nbatch, nblk),
        in_specs=[pl.BlockSpec(memory_space=pltpu.SMEM),
                  pl.BlockSpec((None, 2 * blk, blk), lambda b, n: (jnp.minimum(n, 1), 0, 0)),
                  pl.BlockSpec((blk, dq), cur),
                  pl.BlockSpec((blk, dkv), prev),
                  pl.BlockSpec((blk, dkv), cur),
                  pl.BlockSpec((blk, dkv), prev),
                  pl.BlockSpec((blk, dkv), cur)],
        out_specs=pl.BlockSpec((blk, dq), cur),
        out_shape=jax.ShapeDtypeStruct((m, dq), BF16),
        compiler_params=_params(("parallel", "arbitrary")),
        name="attention",
    )(sinks * math.log2(math.e), jnp.asarray(bias), q, k, k, v, v)


def kernel(x, c, positions, norm_g, w_ada, b_ada, w_ff_in, w_ff_out, s5_w_in, s5_a_re, s5_a_im, s5_b_re, s5_b_im, s5_c_re, s5_c_im, s5_d, s5_log_dt, s5_w_glu, s5_b_glu, s5_w_out, kv_norm_g, w_ada_kv, b_ada_kv, w_kv, attn_w_q, attn_sinks, attn_w_o, final_norm_g):
    nbatch, seq, d = x.shape
    depth = norm_g.shape[0]
    n_s5 = s5_w_in.shape[0]
    m = nbatch * seq
    kvw = w_kv.shape[1] // 2

    c_pad = jnp.zeros((SUBLANES, d), F32).at[:nbatch].set(c)
    mods = _ada(c_pad, w_ada, b_ada)[:, :nbatch]
    mod_kv = _ada(c_pad, w_ada_kv[None], b_ada_kv[None])[0, :nbatch]
    rope = _rope_tables(positions)

    per_batch = dict(rows_per_batch=seq)
    mod = mods.reshape(depth, nbatch, N_SUBLAYERS, 3, d)
    shift, scale, gate = mod[:, :, :, 0], mod[:, :, :, 1], mod[:, :, :, 2]

    def gs_of(layer, sub):
        return norm_g[layer, sub][None, :] * (1.0 + scale[layer, :, sub])

    def shift_rows(sh):
        return jnp.zeros((SUBLANES, d), F32).at[:nbatch].set(sh).astype(BF16)

    def ffn(mod_in, xs, layer, which, next_gs):
        act, w_out = _mm_swiglu(mod_in, w_ff_in, w_ff_out, (layer, which), **per_batch)
        return _mm_resid(act, w_out, (), xs, gate[layer, :, 2 * which], next_gs,
                         weight=0.5, **per_batch)

    xs = x.reshape(m, d)
    xg, ssq = _prep(xs, gs_of(0, 0), **per_batch)
    k_rot = v_kv = None
    for layer in range(depth):
        xs, (xg,), ssq = ffn((xg, ssq, shift_rows(shift[layer, :, 0])), xs, layer, 0, [gs_of(layer, 1)])

        mix_in = (xg, ssq, shift_rows(shift[layer, :, 1]))
        if layer < n_s5:
            i = layer
            v = _mm_plain(mix_in, s5_w_in, (i,), out_dtype=F32, **per_batch)
            bblk, cblk, a = _s5_discretise(s5_a_re[i], s5_a_im[i], s5_b_re[i], s5_b_im[i],
                                           s5_c_re[i], s5_c_im[i], s5_log_dt[i], nbatch)
            y = _s5_core(v.reshape(nbatch, seq, d), bblk, cblk, a, s5_d[i].reshape(-1)).reshape(m, d)
            mixed = _mm_glu(y, s5_w_glu, s5_b_glu[i], (i,))
            w_mix_out, lead = s5_w_out, (i,)
        else:
            j = layer - n_s5
            q = _mm_rope(mix_in, attn_w_q, (j,), rope, out_dtype=BF16,
                         out_scale=HEAD_DIM ** -0.5 * math.log2(math.e), **per_batch)
            mixed = _attention(q, k_rot, v_kv, attn_sinks[j], nbatch=nbatch)
            w_mix_out, lead = attn_w_o, (j,)
        xs, (xg,), ssq = _mm_resid(mixed, w_mix_out, lead, xs, gate[layer, :, 1], [gs_of(layer, 2)],
                                   weight=1.0, **per_batch)

        next_gs = [gs_of(layer + 1, 0)] if layer + 1 < depth else []
        if layer == n_s5 - 1:
            next_gs.append(kv_norm_g[None, :] * (1.0 + mod_kv[:, d:]))
        xs, xgs, ssq = ffn((xg, ssq, shift_rows(shift[layer, :, 2])), xs, layer, 1, next_gs)
        if layer == n_s5 - 1:
            kv_in = (xgs[-1], ssq, shift_rows(mod_kv[:, :d]))
            k_rot = _mm_rope(kv_in, w_kv, (), rope, out_scale=1.0, out_dtype=F32, ncols=kvw,
                             **per_batch)
            v_kv = _mm_plain(kv_in, w_kv, (), out_dtype=F32, ncols=kvw, col0=kvw, **per_batch)
        if layer + 1 < depth:
            xg = xgs[0]

    return _final_norm(xs, final_norm_g).reshape(nbatch, seq, d)
```

```python
import functools
import math

import numpy as np
import jax
import jax.numpy as jnp
from jax import lax
from jax.experimental import pallas as pl
from jax.experimental.pallas import tpu as pltpu

F32 = jnp.float32
BF16 = jnp.bfloat16

RMS_EPS = 1e-6
N_SUBLAYERS = 3
S5_GROUP = 16
S5_STATE = 64
HEAD_DIM = 64
Q_PER_KV = 8
WINDOW = 128
ROPE_THETA = 10000.0

LANES = 128
SUBLANES = 8
S5_TILE = 256
VMEM_LIMIT = 56 * 1024 * 1024


def _params(sem):
    return pltpu.CompilerParams(dimension_semantics=sem, vmem_limit_bytes=VMEM_LIMIT)


def _sigmoid(x):
    return 1.0 / (1.0 + jnp.exp(-x))


def _gelu_tanh(x):
    return 0.5 * x * (1.0 + jnp.tanh(math.sqrt(2.0 / math.pi) * (x + 0.044715 * (x * x * x))))


def _ada_kernel(c_ref, w_ref, b_ref, o_ref):
    c = c_ref[...]
    ca = (c * _sigmoid(c)).astype(BF16)
    o_ref[...] = jnp.dot(ca, w_ref[...].astype(BF16), preferred_element_type=F32) + b_ref[...]


def _ada(c_pad, w, b, bn=1024):
    s, d, n = w.shape
    return pl.pallas_call(
        _ada_kernel,
        grid=(s, n // bn),
        in_specs=[pl.BlockSpec((SUBLANES, d), lambda i, j: (0, 0)),
                  pl.BlockSpec((None, d, bn), lambda i, j: (i, 0, j)),
                  pl.BlockSpec((None, 1, bn), lambda i, j: (i, 0, j))],
        out_specs=pl.BlockSpec((None, SUBLANES, bn), lambda i, j: (i, 0, j)),
        out_shape=jax.ShapeDtypeStruct((s, SUBLANES, n), F32),
        compiler_params=_params(("parallel", "parallel")),
        name="ada",
    )(c_pad, w, b.reshape(s, 1, n))


def _lane_block_sums(sq):
    acc = sq[:, :LANES]
    for j in range(1, sq.shape[1] // LANES):
        acc = acc + sq[:, j * LANES:(j + 1) * LANES]
    return acc


def _row_rsqrt(ssq_ref, rows, width):
    part = ssq_ref[0, rows, :]
    for p in range(1, ssq_ref.shape[0]):
        part = part + ssq_ref[p, rows, :]
    ms = jnp.sum(part, axis=-1, keepdims=True) * (1.0 / width)
    return lax.rsqrt(ms + RMS_EPS)


def _prep_kernel(x_ref, gs_ref, xg_ref, ssq_ref):
    x = x_ref[...]
    xg_ref[...] = (x * gs_ref[...]).astype(xg_ref.dtype)
    ssq_ref[0] = _lane_block_sums(x * x)


def _prep(x, gs, *, rows_per_batch, bl=512):
    m, d = x.shape
    per = rows_per_batch // bl
    nb = gs.shape[0]
    return pl.pallas_call(
        _prep_kernel,
        grid=(m // bl,),
        in_specs=[pl.BlockSpec((bl, d), lambda i: (i, 0)),
                  pl.BlockSpec((None, 1, d), lambda i: (i // per, 0, 0))],
        out_specs=[pl.BlockSpec((bl, d), lambda i: (i, 0)),
                   pl.BlockSpec((1, bl, LANES), lambda i: (0, i, 0))],
        out_shape=[jax.ShapeDtypeStruct((m, d), BF16), jax.ShapeDtypeStruct((1, m, LANES), F32)],
        compiler_params=_params(("parallel",)),
        name="prep",
    )(x, gs.reshape(nb, 1, d))


def _final_norm_kernel(x_ref, g_ref, o_ref):
    x = x_ref[...]
    ms = jnp.mean(x * x, axis=-1, keepdims=True)
    o_ref[...] = x * lax.rsqrt(ms + RMS_EPS) * g_ref[...]


def _final_norm(x, g, bl=512):
    m, d = x.shape
    return pl.pallas_call(
        _final_norm_kernel,
        grid=(m // bl,),
        in_specs=[pl.BlockSpec((bl, d), lambda i: (i, 0)),
                  pl.BlockSpec((1, d), lambda i: (0, 0))],
        out_specs=pl.BlockSpec((bl, d), lambda i: (i, 0)),
        out_shape=jax.ShapeDtypeStruct((m, d), F32),
        compiler_params=_params(("parallel",)),
        name="final_norm",
    )(x, g.reshape(1, d))


def _weight_spec(w, lead, bn, col0=0):
    k = w.shape[-2]
    return pl.BlockSpec((None,) * len(lead) + (k, bn), lambda j, i: tuple(lead) + (0, col0 + j))


def _weight_scratch(w, bn, count=1):
    return [pltpu.VMEM((w.shape[-2], bn), BF16)] * count


def _stage_weight(w_ref, wb_ref, shift_ref=None, sw_ref=None):
    @pl.when(pl.program_id(1) == 0)
    def _():
        wb_ref[...] = w_ref[...].astype(BF16)
        if sw_ref is not None:
            sw_ref[...] = jnp.dot(shift_ref[...], wb_ref[...], preferred_element_type=F32)


def _modulated_dot(xg_ref, wb_ref, ssq_ref, sw_ref, rows_per_batch, rows=None):
    bm = xg_ref.shape[0]
    rows = slice(0, bm) if rows is None else rows
    b = (pl.program_id(1) * bm + rows.start) // rows_per_batch
    acc = jnp.dot(xg_ref[rows, :], wb_ref[...], preferred_element_type=F32)
    return _row_rsqrt(ssq_ref, rows, xg_ref.shape[1]) * acc + sw_ref[pl.ds(b, 1), :]


def _mm_plain_kernel(x_ref, ssq_ref, shift_ref, w_ref, o_ref, wb_ref, sw_ref, *, per):
    _stage_weight(w_ref, wb_ref, shift_ref, sw_ref)
    o_ref[...] = _modulated_dot(x_ref, wb_ref, ssq_ref, sw_ref, per).astype(o_ref.dtype)


def _mm_swiglu_kernel(x_ref, ssq_ref, shift_ref, wg_ref, wu_ref, wo_ref, o_ref, wob_ref,
                      wgb_ref, wub_ref, swg_ref, swu_ref, *, per, sub):
    _stage_weight(wg_ref, wgb_ref, shift_ref, swg_ref)
    _stage_weight(wu_ref, wub_ref, shift_ref, swu_ref)
    step = x_ref.shape[0] // sub
    for r in range(sub):
        rows = slice(r * step, (r + 1) * step)
        g = _modulated_dot(x_ref, wgb_ref, ssq_ref, swg_ref, per, rows)
        u = _modulated_dot(x_ref, wub_ref, ssq_ref, swu_ref, per, rows)
        o_ref[rows, :] = (g * _sigmoid(g) * u).astype(o_ref.dtype)
    wob_ref[...] = wo_ref[...].astype(wob_ref.dtype)


def _mm_glu_kernel(x_ref, w_ref, y_ref, b_ref, o_ref, wb_ref):
    _stage_weight(w_ref, wb_ref)
    acc = jnp.dot(x_ref[...], wb_ref[...], preferred_element_type=F32)
    o_ref[...] = (y_ref[...].astype(F32) * _sigmoid(acc + b_ref[...])).astype(o_ref.dtype)


def _mm_rope_kernel(x_ref, ssq_ref, shift_ref, w_ref, cos_ref, sin_lo_ref, sin_hi_ref, o_ref,
                    wb_ref, sw_ref, *, per, out_scale):
    _stage_weight(w_ref, wb_ref, shift_ref, sw_ref)
    acc = _modulated_dot(x_ref, wb_ref, ssq_ref, sw_ref, per)
    cos, sin_lo, sin_hi = cos_ref[...], sin_lo_ref[...], sin_hi_ref[...]
    half = HEAD_DIM // 2
    for j in range(acc.shape[1] // LANES):
        a = acc[:, j * LANES:(j + 1) * LANES]
        r = (a * cos + pltpu.roll(a, LANES - half, axis=1) * sin_lo
             + pltpu.roll(a, half, axis=1) * sin_hi)
        o_ref[:, j * LANES:(j + 1) * LANES] = (r * out_scale).astype(o_ref.dtype)


def _mm_resid_kernel(x_ref, w_ref, r_ref, gate_ref, *rest, weight, n_next, staged):
    gs_refs, o_ref, xg_refs = rest[:n_next], rest[n_next], rest[n_next + 1:2 * n_next + 1]
    if staged:
        wb_ref = rest[-1]
        _stage_weight(w_ref, wb_ref)
    else:
        wb_ref = w_ref
    acc = jnp.dot(x_ref[...], wb_ref[...], preferred_element_type=F32)
    x_new = r_ref[...] + (weight * (1.0 + gate_ref[...])) * acc
    o_ref[...] = x_new
    if not n_next:
        return
    for gs_ref, xg_ref in zip(gs_refs, xg_refs):
        xg_ref[...] = (x_new * gs_ref[...]).astype(xg_ref.dtype)
    ssq_ref = rest[2 * n_next + 1]
    ssq_ref[...] = _lane_block_sums(x_new * x_new)


_MM_SEM = ("arbitrary", "arbitrary")


def _modulated_specs(mod_in, bm):
    xg, ssq, _ = mod_in
    k = xg.shape[1]
    return [pl.BlockSpec((bm, k), lambda j, i: (i, 0)),
            pl.BlockSpec((ssq.shape[0], bm, LANES), lambda j, i: (0, i, 0)),
            pl.BlockSpec((SUBLANES, k), lambda j, i: (0, 0))]


def _mm_plain(mod_in, w, lead=(), *, out_dtype, rows_per_batch, ncols=None, col0=0, bm=1024, bn=1024):
    xg, ssq, shift = mod_in
    m, k = xg.shape
    n = w.shape[-1] if ncols is None else ncols
    bn = min(bn, n)
    return pl.pallas_call(
        functools.partial(_mm_plain_kernel, per=rows_per_batch),
        grid=(n // bn, m // bm),
        in_specs=_modulated_specs(mod_in, bm) + [_weight_spec(w, lead, bn, col0 // bn)],
        out_specs=pl.BlockSpec((bm, bn), lambda j, i: (i, j)),
        out_shape=jax.ShapeDtypeStruct((m, n), out_dtype),
        scratch_shapes=_weight_scratch(w, bn) + [pltpu.VMEM((SUBLANES, bn), F32)],
        compiler_params=_params(_MM_SEM),
        name="mm_plain",
    )(xg, ssq, shift, w)


def _mm_swiglu(mod_in, w_in, w_out, lead=(), *, rows_per_batch, bm=2048, bn=512, sub=2):
    xg, ssq, shift = mod_in
    m, k = xg.shape
    f = w_in.shape[-1] // 2
    nf, nm = f // bn, m // bm
    slab = f // (nf * nm)
    n_out = w_out.shape[-1]
    nl = len(lead)
    return pl.pallas_call(
        functools.partial(_mm_swiglu_kernel, per=rows_per_batch, sub=sub),
        grid=(nf, nm),
        in_specs=_modulated_specs(mod_in, bm) + [
            _weight_spec(w_in, lead, bn),
            _weight_spec(w_in, lead, bn, nf),
            pl.BlockSpec((None,) * nl + (slab, n_out), lambda j, i: tuple(lead) + (j * nm + i, 0))],
        out_specs=[pl.BlockSpec((bm, bn), lambda j, i: (i, j)),
                   pl.BlockSpec((slab, n_out), lambda j, i: (j * nm + i, 0))],
        out_shape=[jax.ShapeDtypeStruct((m, f), BF16), jax.ShapeDtypeStruct((f, n_out), BF16)],
        scratch_shapes=_weight_scratch(w_in, bn, 2) + [pltpu.VMEM((SUBLANES, bn), F32)] * 2,
        compiler_params=_params(_MM_SEM),
        name="mm_swiglu",
    )(xg, ssq, shift, w_in, w_in, w_out)


def _mm_glu(y, w, b, lead=(), *, bm=1024, bn=1024):
    m, k = y.shape
    n = w.shape[-1]
    return pl.pallas_call(
        _mm_glu_kernel,
        grid=(n // bn, m // bm),
        in_specs=[pl.BlockSpec((bm, k), lambda j, i: (i, 0)),
                  _weight_spec(w, lead, bn),
                  pl.BlockSpec((bm, bn), lambda j, i: (i, j)),
                  pl.BlockSpec((1, bn), lambda j, i: (0, j))],
        out_specs=pl.BlockSpec((bm, bn), lambda j, i: (i, j)),
        out_shape=jax.ShapeDtypeStruct((m, n), BF16),
        scratch_shapes=_weight_scratch(w, bn),
        compiler_params=_params(_MM_SEM),
        name="mm_glu",
    )(y, w, y, b.reshape(1, n))


def _mm_rope(mod_in, w, lead, rope, *, out_scale, out_dtype, rows_per_batch, ncols=None,
             bm=1024, bn=1024):
    xg, ssq, shift = mod_in
    m, k = xg.shape
    n = w.shape[-1] if ncols is None else ncols
    bn = min(bn, n)
    table = pl.BlockSpec((bm, LANES), lambda j, i: (i, 0))
    return pl.pallas_call(
        functools.partial(_mm_rope_kernel, per=rows_per_batch, out_scale=out_scale),
        grid=(n // bn, m // bm),
        in_specs=_modulated_specs(mod_in, bm) + [_weight_spec(w, lead, bn), table, table, table],
        out_specs=pl.BlockSpec((bm, bn), lambda j, i: (i, j)),
        out_shape=jax.ShapeDtypeStruct((m, n), out_dtype),
        scratch_shapes=_weight_scratch(w, bn) + [pltpu.VMEM((SUBLANES, bn), F32)],
        compiler_params=_params(_MM_SEM),
        name="mm_rope",
    )(xg, ssq, shift, w, *rope)


def _mm_resid(x, w, lead, resid, gate, next_gs=(), *, weight, rows_per_batch, bm=512, bn=1024):
    m, k = x.shape
    n = w.shape[-1]
    per = rows_per_batch // bm
    nb = gate.shape[0]
    n_next = len(next_gs)
    staged = w.dtype != BF16
    tile = pl.BlockSpec((bm, bn), lambda j, i: (i, j))
    per_batch = pl.BlockSpec((None, 1, bn), lambda j, i: (i // per, 0, j))
    out_specs = [tile] * (1 + n_next)
    out_shape = [jax.ShapeDtypeStruct((m, n), F32)] + [jax.ShapeDtypeStruct((m, n), BF16)] * n_next
    if n_next:
        out_specs.append(pl.BlockSpec((None, bm, LANES), lambda j, i: (j, i, 0)))
        out_shape.append(jax.ShapeDtypeStruct((n // bn, m, LANES), F32))
    outs = pl.pallas_call(
        functools.partial(_mm_resid_kernel, weight=weight, n_next=n_next, staged=staged),
        grid=(n // bn, m // bm),
        in_specs=[pl.BlockSpec((bm, k), lambda j, i: (i, 0)),
                  _weight_spec(w, lead, bn), tile, per_batch] + [per_batch] * n_next,
        out_specs=out_specs,
        out_shape=out_shape,
        scratch_shapes=_weight_scratch(w, bn) if staged else [],
        compiler_params=_params(_MM_SEM),
        name="mm_resid",
    )(x, w, resid, gate.reshape(nb, 1, n), *[gs.reshape(nb, 1, n) for gs in next_gs])
    return outs[0], list(outs[1:1 + n_next]), (outs[-1] if n_next else None)


def _rope_kernel(pos_ref, invf_ref, lo_ref, cos_ref, sin_lo_ref, sin_hi_ref):
    ang = pos_ref[...].astype(F32) * invf_ref[...]
    sin = jnp.sin(ang)
    lo = lo_ref[...]
    cos_ref[...] = jnp.cos(ang)
    sin_lo_ref[...] = -sin * lo
    sin_hi_ref[...] = sin * (1.0 - lo)


def _rope_tables(positions, bl=1024):
    m = positions.size
    half = HEAD_DIM // 2
    inv_freq = 1.0 / (ROPE_THETA ** (jnp.arange(0, HEAD_DIM, 2, dtype=F32) / HEAD_DIM))
    invf = jnp.tile(inv_freq, LANES // half).reshape(1, LANES)
    lo = jnp.asarray(np.arange(LANES) % HEAD_DIM < half, F32).reshape(1, LANES)
    return pl.pallas_call(
        _rope_kernel,
        grid=(m // bl,),
        in_specs=[pl.BlockSpec((bl, 1), lambda i: (i, 0)),
                  pl.BlockSpec((1, LANES), lambda i: (0, 0)),
                  pl.BlockSpec((1, LANES), lambda i: (0, 0))],
        out_specs=[pl.BlockSpec((bl, LANES), lambda i: (i, 0))] * 3,
        out_shape=[jax.ShapeDtypeStruct((m, LANES), F32)] * 3,
        compiler_params=_params(("parallel",)),
        name="rope_tables",
    )(positions.reshape(m, 1), invf, lo)


def _s5_input_map(v, bblk_ref, bu_ref, *, tc, pitch, nb):
    half_w = 4 * LANES
    vb = v.reshape(nb * tc, S5_TILE).astype(BF16)
    for half in range(2):
        r = jnp.dot(vb, bblk_ref[:, half * 2 * half_w:(half + 1) * 2 * half_w],
                    preferred_element_type=F32)
        for b in range(nb):
            for l in range(8):
                bu_ref[l, pl.ds((half * nb + b) * pitch, tc), :] = (
                    r[b * tc:(b + 1) * tc, l * LANES:(l + 1) * LANES])


def _s5_scan(bu_ref, st_ref, a, xs, *, tc, pitch):
    xs = list(xs)
    for t in range(tc):
        for l in range(4):
            bur = bu_ref[l, pl.ds(t, SUBLANES, stride=pitch), :]
            bui = bu_ref[4 + l, pl.ds(t, SUBLANES, stride=pitch), :]
            xr, xi = xs[l], xs[4 + l]
            nr = a[l] * xr - a[4 + l] * xi + bur
            ni = a[l] * xi + a[4 + l] * xr + bui
            st_ref[l, pl.ds(t, SUBLANES, stride=pitch), :] = nr
            st_ref[4 + l, pl.ds(t, SUBLANES, stride=pitch), :] = ni
            xs[l], xs[4 + l] = nr, ni
    return xs


def _s5_output_map(st_ref, cblk_ref, v, d, *, tc, pitch, nb):
    half_w = 4 * LANES
    acc = None
    for half in range(2):
        blocks = []
        for b in range(nb):
            blocks.append(jnp.concatenate(
                [st_ref[l, pl.ds((half * nb + b) * pitch, tc), :] for l in range(8)], axis=1))
        s = jnp.concatenate(blocks, axis=0).astype(BF16)
        part = jnp.dot(s, cblk_ref[half * 2 * half_w:(half + 1) * 2 * half_w, :],
                       preferred_element_type=F32)
        acc = part if acc is None else acc + part
    y = _gelu_tanh(acc + d * v.reshape(nb * tc, S5_TILE))
    return y.reshape(nb, tc, S5_TILE)


def _s5_kernel(va_ref, vc_ref, bblk_ref, cblk_ref, a_ref, d_ref, y_ref,
               bu0_ref, bu1_ref, st0_ref, st1_ref, x_ref, *, tc, pitch, nb):
    s = pl.program_id(1)
    kw = dict(tc=tc, pitch=pitch, nb=nb)

    @pl.when(s == 0)
    def _():
        x_ref[...] = jnp.zeros_like(x_ref)
        bu1_ref[...] = jnp.zeros_like(bu1_ref)
        st0_ref[...] = jnp.zeros_like(st0_ref)

    bus, sts = (bu0_ref, bu1_ref), (st0_ref, st1_ref)
    a = [a_ref[l] for l in range(8)]
    d = d_ref[...]
    xs = [x_ref[l] for l in range(8)]
    for h in range(2):
        _s5_input_map(va_ref[:, h * tc:(h + 1) * tc, :], bblk_ref, bus[h], **kw)
        xs = _s5_scan(bus[1 - h], sts[1 - h], a, xs, tc=tc, pitch=pitch)
        y = _s5_output_map(sts[h], cblk_ref, vc_ref[:, h * tc:(h + 1) * tc, :], d, **kw)
        y_ref[:, h * tc:(h + 1) * tc, :] = y.astype(y_ref.dtype)
    for l in range(8):
        x_ref[l] = xs[l]


def _s5_discretise(a_re, a_im, b_re, b_im, c_re, c_im, log_dt, nb):
    g, p = a_re.shape
    gpt = S5_TILE // S5_GROUP
    nt = g // gpt
    dt = jnp.exp(log_dt)[:, None]
    mag = jnp.exp(a_re * dt)
    lb_re, lb_im = mag * jnp.cos(a_im * dt), mag * jnp.sin(a_im * dt)
    den = a_re * a_re + a_im * a_im
    k_re = ((lb_re - 1.0) * a_re + lb_im * a_im) / den
    k_im = (lb_im * a_re - (lb_re - 1.0) * a_im) / den
    bb_re = k_re[..., None] * b_re - k_im[..., None] * b_im
    bb_im = k_re[..., None] * b_im + k_im[..., None] * b_re
    g8 = gpt // 2
    nstate = 2 * gpt * p

    def state_cols(t):
        x = t.shape[-1]
        return t.reshape(nt, 2, g8, p, x).transpose(0, 4, 1, 2, 3).reshape(nt, x, 2, g8 * p)

    col = np.arange(nstate)
    col_group = col // (2 * g8 * p) * g8 + col % (g8 * p) // p
    own = jnp.asarray(np.arange(S5_TILE)[:, None] // S5_GROUP == col_group[None, :], F32)
    m_in = jnp.stack([state_cols(bb_re), state_cols(bb_im)], axis=3).reshape(nt, S5_GROUP, nstate)
    bblk = jnp.tile(m_in, (1, gpt, 1)) * own
    m_out = jnp.stack([state_cols(c_re.swapaxes(1, 2)), state_cols(-c_im.swapaxes(1, 2))],
                      axis=3).reshape(nt, S5_GROUP, nstate).swapaxes(1, 2)
    cblk = jnp.tile(m_out, (1, 1, gpt)) * own.T

    def a_slabs(lb):
        t = lb.reshape(nt, 2, 4, LANES)
        t = jnp.broadcast_to(t[:, :, None], (nt, 2, nb, 4, LANES))
        return t.transpose(0, 3, 1, 2, 4).reshape(nt, 4, 2 * nb, LANES)

    a = jnp.concatenate([a_slabs(lb_re), a_slabs(lb_im)], axis=1)
    return bblk.astype(BF16), cblk.astype(BF16), a


def _s5_core(v, bblk, cblk, a, d_skip, *, tc=128):
    nb, seq, d = v.shape
    assert 2 * nb == SUBLANES
    nt = d // S5_TILE
    steps = seq // (2 * tc) + 1
    pitch = tc + 4
    nstate = bblk.shape[2]
    slab = pltpu.VMEM((8, SUBLANES * pitch, LANES), F32)
    return pl.pallas_call(
        functools.partial(_s5_kernel, tc=tc, pitch=pitch, nb=nb),
        grid=(nt, steps),
        in_specs=[pl.BlockSpec((nb, 2 * tc, S5_TILE), lambda j, s: (0, jnp.minimum(s, steps - 2), j)),
                  pl.BlockSpec((nb, 2 * tc, S5_TILE), lambda j, s: (0, jnp.maximum(s - 1, 0), j)),
                  pl.BlockSpec((None, S5_TILE, nstate), lambda j, s: (j, 0, 0)),
                  pl.BlockSpec((None, nstate, S5_TILE), lambda j, s: (j, 0, 0)),
                  pl.BlockSpec((None, 8, SUBLANES, LANES), lambda j, s: (j, 0, 0, 0)),
                  pl.BlockSpec((1, S5_TILE), lambda j, s: (0, j))],
        out_specs=pl.BlockSpec((nb, 2 * tc, S5_TILE), lambda j, s: (0, jnp.maximum(s - 1, 0), j)),
        out_shape=jax.ShapeDtypeStruct((nb, seq, d), BF16),
        scratch_shapes=[slab, slab, slab, slab, pltpu.VMEM((8, SUBLANES, LANES), F32)],
        compiler_params=_params(("arbitrary", "arbitrary")),
        name="s5_core",
    )(v, v, bblk, cblk, a, d_skip.reshape(1, d))


def _attn_kernel(sink_ref, bias_ref, q_ref, kp_ref, kc_ref, vp_ref, vc_ref, o_ref, *, n_kv):
    blk = q_ref.shape[0]
    lane = lax.broadcasted_iota(jnp.int32, (2 * blk, LANES), 1)
    pairs = Q_PER_KV // 2
    nt_dims = (((1,), (1,)), ((), ()))

    def on_slot(x, slot, s):
        x = x if s == slot else pltpu.roll(x, HEAD_DIM, axis=1)
        return jnp.where(lane // HEAD_DIM == s, x, 0.0)

    for kvh in range(n_kv):
        grp, slot = kvh // 2, kvh % 2
        cols = slice(grp * LANES, (grp + 1) * LANES)
        kcat = jnp.concatenate([kp_ref[:, cols], kc_ref[:, cols]], axis=0)
        vcat = jnp.concatenate([vp_ref[:, cols], vc_ref[:, cols]], axis=0)
        base = kvh * pairs
        qg = jnp.concatenate([q_ref[:, (base + p) * LANES:(base + p + 1) * LANES]
                              for p in range(pairs)], axis=0)
        out_t = None
        for s in range(2):
            kexp = on_slot(kcat, slot, s).astype(BF16)
            vexp_t = on_slot(vcat, slot, s).T.astype(BF16)
            scores = lax.dot_general(kexp, qg, nt_dims, preferred_element_type=F32)
            es, rdens = [], []
            for p in range(pairs):
                sink = sink_ref[kvh * Q_PER_KV + 2 * p + s]
                sc = scores[:, p * blk:(p + 1) * blk] + bias_ref[...]
                mx = jnp.maximum(jnp.max(sc, axis=0, keepdims=True), sink)
                e = jnp.exp2(sc - mx)
                den = jnp.sum(e, axis=0, keepdims=True) + jnp.exp2(sink - mx)
                es.append(e.astype(BF16))
                rdens.append(1.0 / den)
            part = jnp.dot(vexp_t, jnp.concatenate(es, axis=1),
                           preferred_element_type=F32) * jnp.concatenate(rdens, axis=1)
            out_t = part if out_t is None else out_t + part
        out = out_t.T
        for p in range(pairs):
            o_ref[:, (base + p) * LANES:(base + p + 1) * LANES] = (
                out[p * blk:(p + 1) * blk].astype(o_ref.dtype))


def _attention(q, k, v, sinks, *, nbatch, blk=WINDOW):
    m, dq = q.shape
    dkv = k.shape[1]
    nblk = m // nbatch // blk
    kj, qi = np.arange(2 * blk)[:, None], np.arange(blk)[None, :]
    window = (kj > qi) & (kj <= qi + blk)
    bias = np.where(np.stack([window & (kj >= blk), window]), 0.0, -1e30).astype(np.float32)
    cur = lambda b, n: (b * nblk + n, 0)
    prev = lambda b, n: (b * nblk + jnp.maximum(n - 1, 0), 0)
    return pl.pallas_call(
        functools.partial(_attn_kernel, n_kv=dkv // HEAD_DIM),
        grid=(nbatch, nblk),
        in_specs=[pl.BlockSpec(memory_space=pltpu.SMEM),
                  pl.BlockSpec((None, 2 * blk, blk), lambda b, n: (jnp.minimum(n, 1), 0, 0)),
                  pl.BlockSpec((blk, dq), cur),
                  pl.BlockSpec((blk, dkv), prev),
                  pl.BlockSpec((blk, dkv), cur),
                  pl.BlockSpec((blk, dkv), prev),
                  pl.BlockSpec((blk, dkv), cur)],
        out_specs=pl.BlockSpec((blk, dq), cur),
        out_shape=jax.ShapeDtypeStruct((m, dq), BF16),
        compiler_params=_params(("parallel", "arbitrary")),
        name="attention",
    )(sinks * math.log2(math.e), jnp.asarray(bias), q, k, k, v, v)


def kernel(x, c, positions, norm_g, w_ada, b_ada, w_ff_in, w_ff_out, s5_w_in, s5_a_re, s5_a_im, s5_b_re, s5_b_im, s5_c_re, s5_c_im, s5_d, s5_log_dt, s5_w_glu, s5_b_glu, s5_w_out, kv_norm_g, w_ada_kv, b_ada_kv, w_kv, attn_w_q, attn_sinks, attn_w_o, final_norm_g):
    nbatch, seq, d = x.shape
    depth = norm_g.shape[0]
    n_s5 = s5_w_in.shape[0]
    m = nbatch * seq
    kvw = w_kv.shape[1] // 2

    c_pad = jnp.zeros((SUBLANES, d), F32).at[:nbatch].set(c)
    mods = _ada(c_pad, w_ada, b_ada)[:, :nbatch]
    mod_kv = _ada(c_pad, w_ada_kv[None], b_ada_kv[None])[0, :nbatch]
    rope = _rope_tables(positions)

    per_batch = dict(rows_per_batch=seq)
    mod = mods.reshape(depth, nbatch, N_SUBLAYERS, 3, d)
    shift, scale, gate = mod[:, :, :, 0], mod[:, :, :, 1], mod[:, :, :, 2]

    def gs_of(layer, sub):
        return norm_g[layer, sub][None, :] * (1.0 + scale[layer, :, sub])

    def shift_rows(sh):
        return jnp.zeros((SUBLANES, d), F32).at[:nbatch].set(sh).astype(BF16)

    def ffn(mod_in, xs, layer, which, next_gs):
        act, w_out = _mm_swiglu(mod_in, w_ff_in, w_ff_out, (layer, which), **per_batch)
        return _mm_resid(act, w_out, (), xs, gate[layer, :, 2 * which], next_gs,
                         weight=0.5, **per_batch)

    xs = x.reshape(m, d)
    xg, ssq = _prep(xs, gs_of(0, 0), **per_batch)
    k_rot = v_kv = None
    for layer in range(depth):
        xs, (xg,), ssq = ffn((xg, ssq, shift_rows(shift[layer, :, 0])), xs, layer, 0, [gs_of(layer, 1)])

        mix_in = (xg, ssq, shift_rows(shift[layer, :, 1]))
        if layer < n_s5:
            i = layer
            v = _mm_plain(mix_in, s5_w_in, (i,), out_dtype=F32, **per_batch)
            bblk, cblk, a = _s5_discretise(s5_a_re[i], s5_a_im[i], s5_b_re[i], s5_b_im[i],
                                           s5_c_re[i], s5_c_im[i], s5_log_dt[i], nbatch)
            y = _s5_core(v.reshape(nbatch, seq, d), bblk, cblk, a, s5_d[i].reshape(-1)).reshape(m, d)
            mixed = _mm_glu(y, s5_w_glu, s5_b_glu[i], (i,))
            w_mix_out, lead = s5_w_out, (i,)
        else:
            j = layer - n_s5
            q = _mm_rope(mix_in, attn_w_q, (j,), rope, out_dtype=BF16,
                         out_scale=HEAD_DIM ** -0.5 * math.log2(math.e), **per_batch)
            mixed = _attention(q, k_rot, v_kv, attn_sinks[j], nbatch=nbatch)
            w_mix_out, lead = attn_w_o, (j,)
        xs, (xg,), ssq = _mm_resid(mixed, w_mix_out, lead, xs, gate[layer, :, 1], [gs_of(layer, 2)],
                                   weight=1.0, **per_batch)

        next_gs = [gs_of(layer + 1, 0)] if layer + 1 < depth else []
        if layer == n_s5 - 1:
            next_gs.append(kv_norm_g[None, :] * (1.0 + mod_kv[:, d:]))
        xs, xgs, ssq = ffn((xg, ssq, shift_rows(shift[layer, :, 2])), xs, layer, 1, next_gs)
        if layer == n_s5 - 1:
            kv_in = (xgs[-1], ssq, shift_rows(mod_kv[:, :d]))
            k_rot = _mm_rope(kv_in, w_kv, (), rope, out_scale=1.0, out_dtype=F32, ncols=kvw,
                             **per_batch)
            v_kv = _mm_plain(kv_in, w_kv, (), out_dtype=F32, ncols=kvw, col0=kvw, **per_batch)
        if layer + 1 < depth:
            xg = xgs[0]

    return _final_norm(xs, final_norm_g).reshape(nbatch, seq, d)
```

```python
import functools
import math

import numpy as np
import jax
import jax.numpy as jnp
from jax import lax
from jax.experimental import pallas as pl
from jax.experimental.pallas import tpu as pltpu

F32 = jnp.float32
BF16 = jnp.bfloat16

RMS_EPS = 1e-6
N_SUBLAYERS = 3
S5_GROUP = 16
S5_STATE = 64
HEAD_DIM = 64
Q_PER_KV = 8
WINDOW = 128
ROPE_THETA = 10000.0

LANES = 128
SUBLANES = 8
S5_TILE = 256
VMEM_LIMIT = 56 * 1024 * 1024


def _params(sem):
    return pltpu.CompilerParams(dimension_semantics=sem, vmem_limit_bytes=VMEM_LIMIT)


def _sigmoid(x):
    return 1.0 / (1.0 + jnp.exp(-x))


def _gelu_tanh(x):
    return 0.5 * x * (1.0 + jnp.tanh(math.sqrt(2.0 / math.pi) * (x + 0.044715 * (x * x * x))))


def _ada_kernel(c_ref, w_ref, b_ref, o_ref):
    c = c_ref[...]
    ca = (c * _sigmoid(c)).astype(BF16)
    o_ref[...] = jnp.dot(ca, w_ref[...].astype(BF16), preferred_element_type=F32) + b_ref[...]


def _ada(c_pad, w, b, bn=2048):
    s, d, n = w.shape
    return pl.pallas_call(
        _ada_kernel,
        grid=(s, n // bn),
        in_specs=[pl.BlockSpec((SUBLANES, d), lambda i, j: (0, 0)),
                  pl.BlockSpec((None, d, bn), lambda i, j: (i, 0, j)),
                  pl.BlockSpec((None, 1, bn), lambda i, j: (i, 0, j))],
        out_specs=pl.BlockSpec((None, SUBLANES, bn), lambda i, j: (i, 0, j)),
        out_shape=jax.ShapeDtypeStruct((s, SUBLANES, n), F32),
        compiler_params=_params(("parallel", "parallel")),
        name="ada",
    )(c_pad, w, b.reshape(s, 1, n))


def _lane_block_sums(sq):
    acc = sq[:, :LANES]
    for j in range(1, sq.shape[1] // LANES):
        acc = acc + sq[:, j * LANES:(j + 1) * LANES]
    return acc


def _row_rsqrt(ssq_ref, rows, width):
    part = ssq_ref[0, rows, :]
    for p in range(1, ssq_ref.shape[0]):
        part = part + ssq_ref[p, rows, :]
    ms = jnp.sum(part, axis=-1, keepdims=True) * (1.0 / width)
    return lax.rsqrt(ms + RMS_EPS)


def _prep_kernel(x_ref, gs_ref, xg_ref, ssq_ref):
    x = x_ref[...]
    xg_ref[...] = (x * gs_ref[...]).astype(xg_ref.dtype)
    ssq_ref[0] = _lane_block_sums(x * x)


def _prep(x, gs, *, rows_per_batch, bl=512):
    m, d = x.shape
    per = rows_per_batch // bl
    nb = gs.shape[0]
    return pl.pallas_call(
        _prep_kernel,
        grid=(m // bl,),
        in_specs=[pl.BlockSpec((bl, d), lambda i: (i, 0)),
                  pl.BlockSpec((None, 1, d), lambda i: (i // per, 0, 0))],
        out_specs=[pl.BlockSpec((bl, d), lambda i: (i, 0)),
                   pl.BlockSpec((1, bl, LANES), lambda i: (0, i, 0))],
        out_shape=[jax.ShapeDtypeStruct((m, d), BF16), jax.ShapeDtypeStruct((1, m, LANES), F32)],
        compiler_params=_params(("parallel",)),
        name="prep",
    )(x, gs.reshape(nb, 1, d))


def _final_norm_kernel(x_ref, g_ref, o_ref):
    x = x_ref[...]
    ms = jnp.mean(x * x, axis=-1, keepdims=True)
    o_ref[...] = x * lax.rsqrt(ms + RMS_EPS) * g_ref[...]


def _final_norm(x, g, bl=512):
    m, d = x.shape
    return pl.pallas_call(
        _final_norm_kernel,
        grid=(m // bl,),
        in_specs=[pl.BlockSpec((bl, d), lambda i: (i, 0)),
                  pl.BlockSpec((1, d), lambda i: (0, 0))],
        out_specs=pl.BlockSpec((bl, d), lambda i: (i, 0)),
        out_shape=jax.ShapeDtypeStruct((m, d), F32),
        compiler_params=_params(("parallel",)),
        name="final_norm",
    )(x, g.reshape(1, d))


def _weight_spec(w, lead, bn, col0=0):
    k = w.shape[-2]
    mode = dict(pipeline_mode=pl.Buffered(1)) if bn == w.shape[-1] else {}
    return pl.BlockSpec((None,) * len(lead) + (k, bn), lambda j, i: tuple(lead) + (0, col0 + j), **mode)


def _weight_scratch(w, bn, count=1):
    return [pltpu.VMEM((w.shape[-2], bn), BF16)] * count


def _stage_weight(w_ref, wb_ref, shift_ref=None, sw_ref=None):
    @pl.when(pl.program_id(1) == 0)
    def _():
        wb_ref[...] = w_ref[...].astype(BF16)
        if sw_ref is not None:
            sw_ref[...] = jnp.dot(shift_ref[...], wb_ref[...], preferred_element_type=F32)


def _modulated_dot(xg_ref, wb_ref, ssq_ref, sw_ref, rows_per_batch, rows=None):
    bm = xg_ref.shape[0]
    rows = slice(0, bm) if rows is None else rows
    b = (pl.program_id(1) * bm + rows.start) // rows_per_batch
    acc = jnp.dot(xg_ref[rows, :], wb_ref[...], preferred_element_type=F32)
    return _row_rsqrt(ssq_ref, rows, xg_ref.shape[1]) * acc + sw_ref[pl.ds(b, 1), :]


def _mm_plain_kernel(x_ref, ssq_ref, shift_ref, w_ref, o_ref, wb_ref, sw_ref, *, per):
    _stage_weight(w_ref, wb_ref, shift_ref, sw_ref)
    o_ref[...] = _modulated_dot(x_ref, wb_ref, ssq_ref, sw_ref, per).astype(o_ref.dtype)


def _mm_swiglu_kernel(x_ref, ssq_ref, shift_ref, wg_ref, wu_ref, wo_ref, o_ref, wob_ref,
                      wgb_ref, wub_ref, swg_ref, swu_ref, *, per, sub):
    _stage_weight(wg_ref, wgb_ref, shift_ref, swg_ref)
    _stage_weight(wu_ref, wub_ref, shift_ref, swu_ref)
    step = x_ref.shape[0] // sub
    for r in range(sub):
        rows = slice(r * step, (r + 1) * step)
        g = _modulated_dot(x_ref, wgb_ref, ssq_ref, swg_ref, per, rows)
        u = _modulated_dot(x_ref, wub_ref, ssq_ref, swu_ref, per, rows)
        o_ref[rows, :] = (g * _sigmoid(g) * u).astype(o_ref.dtype)
    wob_ref[...] = wo_ref[...].astype(wob_ref.dtype)


def _mm_glu_kernel(x_ref, w_ref, y_ref, b_ref, o_ref, wb_ref):
    _stage_weight(w_ref, wb_ref)
    acc = jnp.dot(x_ref[...], wb_ref[...], preferred_element_type=F32)
    o_ref[...] = (y_ref[...].astype(F32) * _sigmoid(acc + b_ref[...])).astype(o_ref.dtype)


def _mm_rope_kernel(x_ref, ssq_ref, shift_ref, w_ref, cos_ref, sin_lo_ref, sin_hi_ref, o_ref,
                    wb_ref, sw_ref, *, per, out_scale, rope_groups):
    _stage_weight(w_ref, wb_ref, shift_ref, sw_ref)
    acc = _modulated_dot(x_ref, wb_ref, ssq_ref, sw_ref, per)
    cos, sin_lo, sin_hi = cos_ref[...], sin_lo_ref[...], sin_hi_ref[...]
    half = HEAD_DIM // 2
    for j in range(acc.shape[1] // LANES):
        a = acc[:, j * LANES:(j + 1) * LANES]
        if j < rope_groups:
            a = (a * cos + pltpu.roll(a, LANES - half, axis=1) * sin_lo
                 + pltpu.roll(a, half, axis=1) * sin_hi)
        o_ref[:, j * LANES:(j + 1) * LANES] = (a * out_scale).astype(o_ref.dtype)


def _mm_resid_kernel(x_ref, w_ref, r_ref, gate_ref, *rest, weight, n_next, staged, final):
    n_extra = 1 if final else n_next
    gs_refs, o_ref, xg_refs = rest[:n_extra], rest[n_extra], rest[n_extra + 1:2 * n_next + 1]
    if staged:
        wb_ref = rest[-1]
        _stage_weight(w_ref, wb_ref)
    else:
        wb_ref = w_ref
    acc = jnp.dot(x_ref[...], wb_ref[...], preferred_element_type=F32)
    x_new = r_ref[...] + (weight * (1.0 + gate_ref[...])) * acc
    if final:
        ms = jnp.mean(x_new * x_new, axis=-1, keepdims=True)
        o_ref[...] = x_new * lax.rsqrt(ms + RMS_EPS) * gs_refs[0][...]
        return
    o_ref[...] = x_new
    if not n_next:
        return
    for gs_ref, xg_ref in zip(gs_refs, xg_refs):
        xg_ref[...] = (x_new * gs_ref[...]).astype(xg_ref.dtype)
    ssq_ref = rest[2 * n_next + 1]
    ssq_ref[...] = _lane_block_sums(x_new * x_new)


_MM_SEM = ("arbitrary", "arbitrary")


def _modulated_specs(mod_in, bm):
    xg, ssq, _ = mod_in
    k = xg.shape[1]
    return [pl.BlockSpec((bm, k), lambda j, i: (i, 0)),
            pl.BlockSpec((ssq.shape[0], bm, LANES), lambda j, i: (0, i, 0)),
            pl.BlockSpec((SUBLANES, k), lambda j, i: (0, 0))]


def _mm_plain(mod_in, w, lead=(), *, out_dtype, rows_per_batch, ncols=None, col0=0, bm=1024, bn=1024):
    xg, ssq, shift = mod_in
    m, k = xg.shape
    n = w.shape[-1] if ncols is None else ncols
    bn = min(bn, n)
    return pl.pallas_call(
        functools.partial(_mm_plain_kernel, per=rows_per_batch),
        grid=(n // bn, m // bm),
        in_specs=_modulated_specs(mod_in, bm) + [_weight_spec(w, lead, bn, col0 // bn)],
        out_specs=pl.BlockSpec((bm, bn), lambda j, i: (i, j)),
        out_shape=jax.ShapeDtypeStruct((m, n), out_dtype),
        scratch_shapes=_weight_scratch(w, bn) + [pltpu.VMEM((SUBLANES, bn), F32)],
        compiler_params=_params(_MM_SEM),
        name="mm_plain",
    )(xg, ssq, shift, w)


def _mm_swiglu(mod_in, w_in, w_out, lead=(), *, rows_per_batch, bm=2048, bn=512, sub=2):
    xg, ssq, shift = mod_in
    m, k = xg.shape
    f = w_in.shape[-1] // 2
    nf, nm = f // bn, m // bm
    slab = f // (nf * nm)
    n_out = w_out.shape[-1]
    nl = len(lead)
    return pl.pallas_call(
        functools.partial(_mm_swiglu_kernel, per=rows_per_batch, sub=sub),
        grid=(nf, nm),
        in_specs=_modulated_specs(mod_in, bm) + [
            _weight_spec(w_in, lead, bn),
            _weight_spec(w_in, lead, bn, nf),
            pl.BlockSpec((None,) * nl + (slab, n_out), lambda j, i: tuple(lead) + (j * nm + i, 0))],
        out_specs=[pl.BlockSpec((bm, bn), lambda j, i: (i, j)),
                   pl.BlockSpec((slab, n_out), lambda j, i: (j * nm + i, 0))],
        out_shape=[jax.ShapeDtypeStruct((m, f), BF16), jax.ShapeDtypeStruct((f, n_out), BF16)],
        scratch_shapes=_weight_scratch(w_in, bn, 2) + [pltpu.VMEM((SUBLANES, bn), F32)] * 2,
        compiler_params=_params(_MM_SEM),
        name="mm_swiglu",
    )(xg, ssq, shift, w_in, w_in, w_out)


def _mm_glu(y, w, b, lead=(), *, bm=1024, bn=1024):
    m, k = y.shape
    n = w.shape[-1]
    return pl.pallas_call(
        _mm_glu_kernel,
        grid=(n // bn, m // bm),
        in_specs=[pl.BlockSpec((bm, k), lambda j, i: (i, 0)),
                  _weight_spec(w, lead, bn),
                  pl.BlockSpec((bm, bn), lambda j, i: (i, j)),
                  pl.BlockSpec((1, bn), lambda j, i: (0, j))],
        out_specs=pl.BlockSpec((bm, bn), lambda j, i: (i, j)),
        out_shape=jax.ShapeDtypeStruct((m, n), BF16),
        scratch_shapes=_weight_scratch(w, bn),
        compiler_params=_params(_MM_SEM),
        name="mm_glu",
    )(y, w, y, b.reshape(1, n))


def _mm_rope(mod_in, w, lead, rope, *, out_scale, out_dtype, rows_per_batch, rope_cols=None,
             bm=1024, bn=1024):
    xg, ssq, shift = mod_in
    m, k = xg.shape
    n = w.shape[-1]
    bn = min(bn, n)
    assert rope_cols is None or bn == n
    rope_groups = (bn if rope_cols is None else rope_cols) // LANES
    table = pl.BlockSpec((bm, LANES), lambda j, i: (i, 0))
    return pl.pallas_call(
        functools.partial(_mm_rope_kernel, per=rows_per_batch, out_scale=out_scale,
                          rope_groups=rope_groups),
        grid=(n // bn, m // bm),
        in_specs=_modulated_specs(mod_in, bm) + [_weight_spec(w, lead, bn), table, table, table],
        out_specs=pl.BlockSpec((bm, bn), lambda j, i: (i, j)),
        out_shape=jax.ShapeDtypeStruct((m, n), out_dtype),
        scratch_shapes=_weight_scratch(w, bn) + [pltpu.VMEM((SUBLANES, bn), F32)],
        compiler_params=_params(_MM_SEM),
        name="mm_rope",
    )(xg, ssq, shift, w, *rope)


def _mm_resid(x, w, lead, resid, gate, next_gs=(), *, weight, rows_per_batch, final_g=None,
              bm=512, bn=1024):
    m, k = x.shape
    n = w.shape[-1]
    final = final_g is not None
    if final:
        bm, bn = bm // 2, n
    per = rows_per_batch // bm
    nb = gate.shape[0]
    n_next = len(next_gs)
    staged = w.dtype != BF16
    tile = pl.BlockSpec((bm, bn), lambda j, i: (i, j))
    per_batch = pl.BlockSpec((None, 1, bn), lambda j, i: (i // per, 0, j))
    out_specs = [tile] * (1 + n_next)
    out_shape = [jax.ShapeDtypeStruct((m, n), F32)] + [jax.ShapeDtypeStruct((m, n), BF16)] * n_next
    if n_next:
        out_specs.append(pl.BlockSpec((None, bm, LANES), lambda j, i: (j, i, 0)))
        out_shape.append(jax.ShapeDtypeStruct((n // bn, m, LANES), F32))
    outs = pl.pallas_call(
        functools.partial(_mm_resid_kernel, weight=weight, n_next=n_next, staged=staged, final=final),
        grid=(n // bn, m // bm),
        in_specs=([pl.BlockSpec((bm, k), lambda j, i: (i, 0)),
                   _weight_spec(w, lead, bn), tile, per_batch] + [per_batch] * n_next
                  + [pl.BlockSpec((1, bn), lambda j, i: (0, j))] * final),
        out_specs=out_specs,
        out_shape=out_shape,
        scratch_shapes=_weight_scratch(w, bn) if staged else [],
        compiler_params=_params(_MM_SEM),
        name="mm_resid",
    )(x, w, resid, gate.reshape(nb, 1, n), *[gs.reshape(nb, 1, n) for gs in next_gs],
      *([final_g.reshape(1, n)] if final else []))
    return outs[0], list(outs[1:1 + n_next]), (outs[-1] if n_next else None)


def _rope_kernel(pos_ref, invf_ref, lo_ref, cos_ref, sin_lo_ref, sin_hi_ref):
    ang = pos_ref[...].astype(F32) * invf_ref[...]
    sin = jnp.sin(ang)
    lo = lo_ref[...]
    cos_ref[...] = jnp.cos(ang)
    sin_lo_ref[...] = -sin * lo
    sin_hi_ref[...] = sin * (1.0 - lo)


def _rope_tables(positions, bl=1024):
    m = positions.size
    half = HEAD_DIM // 2
    inv_freq = 1.0 / (ROPE_THETA ** (jnp.arange(0, HEAD_DIM, 2, dtype=F32) / HEAD_DIM))
    invf = jnp.tile(inv_freq, LANES // half).reshape(1, LANES)
    lo = jnp.asarray(np.arange(LANES) % HEAD_DIM < half, F32).reshape(1, LANES)
    return pl.pallas_call(
        _rope_kernel,
        grid=(m // bl,),
        in_specs=[pl.BlockSpec((bl, 1), lambda i: (i, 0)),
                  pl.BlockSpec((1, LANES), lambda i: (0, 0)),
                  pl.BlockSpec((1, LANES), lambda i: (0, 0))],
        out_specs=[pl.BlockSpec((bl, LANES), lambda i: (i, 0))] * 3,
        out_shape=[jax.ShapeDtypeStruct((m, LANES), F32)] * 3,
        compiler_params=_params(("parallel",)),
        name="rope_tables",
    )(positions.reshape(m, 1), invf, lo)


def _s5_input_map(v, bblk_ref, bu_ref, *, tc, pitch, nb):
    half_w = 4 * LANES
    vb = v.reshape(nb * tc, S5_TILE).astype(BF16)
    for half in range(2):
        r = jnp.dot(vb, bblk_ref[:, half * 2 * half_w:(half + 1) * 2 * half_w],
                    preferred_element_type=F32)
        for b in range(nb):
            for l in range(8):
                bu_ref[l, pl.ds((half * nb + b) * pitch, tc), :] = (
                    r[b * tc:(b + 1) * tc, l * LANES:(l + 1) * LANES])


def _s5_scan(bu_ref, st_ref, a, xs, *, tc, pitch):
    xs = list(xs)
    for t in range(tc):
        for l in range(4):
            bur = bu_ref[l, pl.ds(t, SUBLANES, stride=pitch), :]
            bui = bu_ref[4 + l, pl.ds(t, SUBLANES, stride=pitch), :]
            xr, xi = xs[l], xs[4 + l]
            nr = a[l] * xr - a[4 + l] * xi + bur
            ni = a[l] * xi + a[4 + l] * xr + bui
            st_ref[l, pl.ds(t, SUBLANES, stride=pitch), :] = nr
            st_ref[4 + l, pl.ds(t, SUBLANES, stride=pitch), :] = ni
            xs[l], xs[4 + l] = nr, ni
    return xs


def _s5_output_map(st_ref, cblk_ref, v, d, *, tc, pitch, nb):
    half_w = 4 * LANES
    acc = None
    for half in range(2):
        blocks = []
        for b in range(nb):
            blocks.append(jnp.concatenate(
                [st_ref[l, pl.ds((half * nb + b) * pitch, tc), :] for l in range(8)], axis=1))
        s = jnp.concatenate(blocks, axis=0).astype(BF16)
        part = jnp.dot(s, cblk_ref[half * 2 * half_w:(half + 1) * 2 * half_w, :],
                       preferred_element_type=F32)
        acc = part if acc is None else acc + part
    y = _gelu_tanh(acc + d * v.reshape(nb * tc, S5_TILE))
    return y.reshape(nb, tc, S5_TILE)


def _s5_kernel(va_ref, vc_ref, bblk_ref, cblk_ref, a_ref, d_ref, y_ref,
               bu0_ref, bu1_ref, st0_ref, st1_ref, x_ref, *, tc, pitch, nb):
    s = pl.program_id(1)
    kw = dict(tc=tc, pitch=pitch, nb=nb)

    @pl.when(s == 0)
    def _():
        x_ref[...] = jnp.zeros_like(x_ref)
        bu1_ref[...] = jnp.zeros_like(bu1_ref)
        st0_ref[...] = jnp.zeros_like(st0_ref)

    bus, sts = (bu0_ref, bu1_ref), (st0_ref, st1_ref)
    a = [a_ref[l] for l in range(8)]
    d = d_ref[...]
    xs = [x_ref[l] for l in range(8)]
    for h in range(2):
        _s5_input_map(va_ref[:, h * tc:(h + 1) * tc, :], bblk_ref, bus[h], **kw)
        xs = _s5_scan(bus[1 - h], sts[1 - h], a, xs, tc=tc, pitch=pitch)
        y = _s5_output_map(sts[h], cblk_ref, vc_ref[:, h * tc:(h + 1) * tc, :], d, **kw)
        y_ref[:, h * tc:(h + 1) * tc, :] = y.astype(y_ref.dtype)
    for l in range(8):
        x_ref[l] = xs[l]


def _s5_discretise(a_re, a_im, b_re, b_im, c_re, c_im, log_dt, nb):
    g, p = a_re.shape
    gpt = S5_TILE // S5_GROUP
    nt = g // gpt
    dt = jnp.exp(log_dt)[:, None]
    mag = jnp.exp(a_re * dt)
    lb_re, lb_im = mag * jnp.cos(a_im * dt), mag * jnp.sin(a_im * dt)
    den = a_re * a_re + a_im * a_im
    k_re = ((lb_re - 1.0) * a_re + lb_im * a_im) / den
    k_im = (lb_im * a_re - (lb_re - 1.0) * a_im) / den
    bb_re = k_re[..., None] * b_re - k_im[..., None] * b_im
    bb_im = k_re[..., None] * b_im + k_im[..., None] * b_re
    g8 = gpt // 2
    nstate = 2 * gpt * p

    def state_cols(t):
        x = t.shape[-1]
        return t.reshape(nt, 2, g8, p, x).transpose(0, 4, 1, 2, 3).reshape(nt, x, 2, g8 * p)

    col = np.arange(nstate)
    col_group = col // (2 * g8 * p) * g8 + col % (g8 * p) // p
    own = jnp.asarray(np.arange(S5_TILE)[:, None] // S5_GROUP == col_group[None, :], F32)
    m_in = jnp.stack([state_cols(bb_re), state_cols(bb_im)], axis=3).reshape(nt, S5_GROUP, nstate)
    bblk = jnp.tile(m_in, (1, gpt, 1)) * own
    m_out = jnp.stack([state_cols(c_re.swapaxes(1, 2)), state_cols(-c_im.swapaxes(1, 2))],
                      axis=3).reshape(nt, S5_GROUP, nstate).swapaxes(1, 2)
    cblk = jnp.tile(m_out, (1, 1, gpt)) * own.T

    def a_slabs(lb):
        t = lb.reshape(nt, 2, 4, LANES)
        t = jnp.broadcast_to(t[:, :, None], (nt, 2, nb, 4, LANES))
        return t.transpose(0, 3, 1, 2, 4).reshape(nt, 4, 2 * nb, LANES)

    a = jnp.concatenate([a_slabs(lb_re), a_slabs(lb_im)], axis=1)
    return bblk.astype(BF16), cblk.astype(BF16), a


def _s5_core(v, bblk, cblk, a, d_skip, *, tc=128):
    nb, seq, d = v.shape
    assert 2 * nb == SUBLANES
    nt = d // S5_TILE
    steps = seq // (2 * tc) + 1
    pitch = tc + 4
    nstate = bblk.shape[2]
    slab = pltpu.VMEM((8, SUBLANES * pitch, LANES), F32)
    return pl.pallas_call(
        functools.partial(_s5_kernel, tc=tc, pitch=pitch, nb=nb),
        grid=(nt, steps),
        in_specs=[pl.BlockSpec((nb, 2 * tc, S5_TILE), lambda j, s: (0, jnp.minimum(s, steps - 2), j)),
                  pl.BlockSpec((nb, 2 * tc, S5_TILE), lambda j, s: (0, jnp.maximum(s - 1, 0), j)),
                  pl.BlockSpec((None, S5_TILE, nstate), lambda j, s: (j, 0, 0)),
                  pl.BlockSpec((None, nstate, S5_TILE), lambda j, s: (j, 0, 0)),
                  pl.BlockSpec((None, 8, SUBLANES, LANES), lambda j, s: (j, 0, 0, 0)),
                  pl.BlockSpec((1, S5_TILE), lambda j, s: (0, j))],
        out_specs=pl.BlockSpec((nb, 2 * tc, S5_TILE), lambda j, s: (0, jnp.maximum(s - 1, 0), j)),
        out_shape=jax.ShapeDtypeStruct((nb, seq, d), BF16),
        scratch_shapes=[slab, slab, slab, slab, pltpu.VMEM((8, SUBLANES, LANES), F32)],
        compiler_params=_params(("arbitrary", "arbitrary")),
        name="s5_core",
    )(v, v, bblk, cblk, a, d_skip.reshape(1, d))


def _attn_kernel(sink_ref, bias_ref, q_ref, kp_ref, kc_ref, vp_ref, vc_ref, o_ref, *, n_kv):
    blk = q_ref.shape[0]
    lane = lax.broadcasted_iota(jnp.int32, (2 * blk, LANES), 1)
    pairs = Q_PER_KV // 2
    nt_dims = (((1,), (1,)), ((), ()))

    def on_slot(x, slot, s):
        x = x if s == slot else pltpu.roll(x, HEAD_DIM, axis=1)
        return jnp.where(lane // HEAD_DIM == s, x, 0.0)

    for kvh in range(n_kv):
        grp, slot = kvh // 2, kvh % 2
        cols = slice(grp * LANES, (grp + 1) * LANES)
        kcat = jnp.concatenate([kp_ref[:, cols], kc_ref[:, cols]], axis=0)
        vcat = jnp.concatenate([vp_ref[:, cols], vc_ref[:, cols]], axis=0)
        base = kvh * pairs
        qg = jnp.concatenate([q_ref[:, (base + p) * LANES:(base + p + 1) * LANES]
                              for p in range(pairs)], axis=0)
        out_t = None
        for s in range(2):
            kexp = on_slot(kcat, slot, s).astype(BF16)
            vexp_t = on_slot(vcat, slot, s).T.astype(BF16)
            scores = lax.dot_general(kexp, qg, nt_dims, preferred_element_type=F32)
            es, rdens = [], []
            for p in range(pairs):
                sink = sink_ref[kvh * Q_PER_KV + 2 * p + s]
                sc = scores[:, p * blk:(p + 1) * blk] + bias_ref[...]
                mx = jnp.maximum(jnp.max(sc, axis=0, keepdims=True), sink)
                e = jnp.exp2(sc - mx)
                den = jnp.sum(e, axis=0, keepdims=True) + jnp.exp2(sink - mx)
                es.append(e.astype(BF16))
                rdens.append(1.0 / den)
            part = jnp.dot(vexp_t, jnp.concatenate(es, axis=1),
                           preferred_element_type=F32) * jnp.concatenate(rdens, axis=1)
            out_t = part if out_t is None else out_t + part
        out = out_t.T
        for p in range(pairs):
            o_ref[:, (base + p) * LANES:(base + p + 1) * LANES] = (
                out[p * blk:(p + 1) * blk].astype(o_ref.dtype))


def _attention(q, kv, sinks, *, nbatch, blk=WINDOW):
    m, dq = q.shape
    dkv = kv.shape[1] // 2
    nblk = m // nbatch // blk
    kj, qi = np.arange(2 * blk)[:, None], np.arange(blk)[None, :]
    window = (kj > qi) & (kj <= qi + blk)
    bias = np.where(np.stack([window & (kj >= blk), window]), 0.0, -1e30).astype(np.float32)
    def rows(prev, col):
        return lambda b, n: (b * nblk + (jnp.maximum(n - 1, 0) if prev else n), col)

    return pl.pallas_call(
        functools.partial(_attn_kernel, n_kv=dkv // HEAD_DIM),
        grid=(nbatch, nblk),
        in_specs=[pl.BlockSpec(memory_space=pltpu.SMEM),
                  pl.BlockSpec((None, 2 * blk, blk), lambda b, n: (jnp.minimum(n, 1), 0, 0)),
                  pl.BlockSpec((blk, dq), rows(False, 0)),
                  pl.BlockSpec((blk, dkv), rows(True, 0)),
                  pl.BlockSpec((blk, dkv), rows(False, 0)),
                  pl.BlockSpec((blk, dkv), rows(True, 1)),
                  pl.BlockSpec((blk, dkv), rows(False, 1))],
        out_specs=pl.BlockSpec((blk, dq), rows(False, 0)),
        out_shape=jax.ShapeDtypeStruct((m, dq), BF16),
        compiler_params=_params(("parallel", "arbitrary")),
        name="attention",
    )(sinks * math.log2(math.e), jnp.asarray(bias), q, kv, kv, kv, kv)


def kernel(x, c, positions, norm_g, w_ada, b_ada, w_ff_in, w_ff_out, s5_w_in, s5_a_re, s5_a_im, s5_b_re, s5_b_im, s5_c_re, s5_c_im, s5_d, s5_log_dt, s5_w_glu, s5_b_glu, s5_w_out, kv_norm_g, w_ada_kv, b_ada_kv, w_kv, attn_w_q, attn_sinks, attn_w_o, final_norm_g):
    nbatch, seq, d = x.shape
    depth = norm_g.shape[0]
    n_s5 = s5_w_in.shape[0]
    m = nbatch * seq
    kvw = w_kv.shape[1] // 2

    c_pad = jnp.zeros((SUBLANES, d), F32).at[:nbatch].set(c)
    mods = _ada(c_pad, w_ada, b_ada)[:, :nbatch]
    mod_kv = _ada(c_pad, w_ada_kv[None], b_ada_kv[None])[0, :nbatch]
    rope = _rope_tables(positions)

    per_batch = dict(rows_per_batch=seq)
    mod = mods.reshape(depth, nbatch, N_SUBLAYERS, 3, d)
    shift, scale, gate = mod[:, :, :, 0], mod[:, :, :, 1], mod[:, :, :, 2]

    def gs_of(layer, sub):
        return norm_g[layer, sub][None, :] * (1.0 + scale[layer, :, sub])

    def shift_rows(sh):
        return jnp.zeros((SUBLANES, d), F32).at[:nbatch].set(sh).astype(BF16)

    def ffn(mod_in, xs, layer, which, next_gs, final_g=None):
        act, w_out = _mm_swiglu(mod_in, w_ff_in, w_ff_out, (layer, which), **per_batch)
        return _mm_resid(act, w_out, (), xs, gate[layer, :, 2 * which], next_gs,
                         weight=0.5, final_g=final_g, **per_batch)

    xs = x.reshape(m, d)
    xg, ssq = _prep(xs, gs_of(0, 0), **per_batch)
    kv = None
    for layer in range(depth):
        xs, (xg,), ssq = ffn((xg, ssq, shift_rows(shift[layer, :, 0])), xs, layer, 0, [gs_of(layer, 1)])

        mix_in = (xg, ssq, shift_rows(shift[layer, :, 1]))
        if layer < n_s5:
            i = layer
            v = _mm_plain(mix_in, s5_w_in, (i,), out_dtype=F32, **per_batch)
            bblk, cblk, a = _s5_discretise(s5_a_re[i], s5_a_im[i], s5_b_re[i], s5_b_im[i],
                                           s5_c_re[i], s5_c_im[i], s5_log_dt[i], nbatch)
            y = _s5_core(v.reshape(nbatch, seq, d), bblk, cblk, a, s5_d[i].reshape(-1)).reshape(m, d)
            mixed = _mm_glu(y, s5_w_glu, s5_b_glu[i], (i,))
            w_mix_out, lead = s5_w_out, (i,)
        else:
            j = layer - n_s5
            q = _mm_rope(mix_in, attn_w_q, (j,), rope, out_dtype=BF16,
                         out_scale=HEAD_DIM ** -0.5 * math.log2(math.e), **per_batch)
            mixed = _attention(q, kv, attn_sinks[j], nbatch=nbatch)
            w_mix_out, lead = attn_w_o, (j,)
        xs, (xg,), ssq = _mm_resid(mixed, w_mix_out, lead, xs, gate[layer, :, 1], [gs_of(layer, 2)],
                                   weight=1.0, **per_batch)

        next_gs = [gs_of(layer + 1, 0)] if layer + 1 < depth else []
        if layer == n_s5 - 1:
            next_gs.append(kv_norm_g[None, :] * (1.0 + mod_kv[:, d:]))
        fuse_final = layer + 1 == depth and not next_gs
        xs, xgs, ssq = ffn((xg, ssq, shift_rows(shift[layer, :, 2])), xs, layer, 1, next_gs,
                           final_norm_g if fuse_final else None)
        if layer == n_s5 - 1:
            kv_in = (xgs[-1], ssq, shift_rows(mod_kv[:, :d]))
            kv = _mm_rope(kv_in, w_kv, (), rope, out_scale=1.0, out_dtype=F32, rope_cols=kvw,
                          **per_batch)
        if layer + 1 < depth:
            xg = xgs[0]

    out = xs if fuse_final else _final_norm(xs, final_norm_g)
    return out.reshape(nbatch, seq, d)
```

```python
import functools
import math

import numpy as np
import jax
import jax.numpy as jnp
from jax import lax
from jax.experimental import pallas as pl
from jax.experimental.pallas import tpu as pltpu

F32 = jnp.float32
BF16 = jnp.bfloat16

RMS_EPS = 1e-6
N_SUBLAYERS = 3
S5_GROUP = 16
S5_STATE = 64
HEAD_DIM = 64
Q_PER_KV = 8
WINDOW = 128
ROPE_THETA = 10000.0

LANES = 128
SUBLANES = 8
S5_TILE = 256
VMEM_LIMIT = 56 * 1024 * 1024


def _params(sem):
    return pltpu.CompilerParams(dimension_semantics=sem, vmem_limit_bytes=VMEM_LIMIT)


def _sigmoid(x):
    return 1.0 / (1.0 + jnp.exp(-x))


def _gelu_tanh(x):
    return 0.5 * x * (1.0 + jnp.tanh(math.sqrt(2.0 / math.pi) * (x + 0.044715 * (x * x * x))))


def _ada_kernel(c_ref, w_ref, b_ref, o_ref):
    c = c_ref[...]
    ca = (c * _sigmoid(c)).astype(BF16)
    o_ref[...] = jnp.dot(ca, w_ref[...].astype(BF16), preferred_element_type=F32) + b_ref[...]


def _ada(c_pad, w, b, bn=2048):
    s, d, n = w.shape
    return pl.pallas_call(
        _ada_kernel,
        grid=(s, n // bn),
        in_specs=[pl.BlockSpec((SUBLANES, d), lambda i, j: (0, 0)),
                  pl.BlockSpec((None, d, bn), lambda i, j: (i, 0, j)),
                  pl.BlockSpec((None, 1, bn), lambda i, j: (i, 0, j))],
        out_specs=pl.BlockSpec((None, SUBLANES, bn), lambda i, j: (i, 0, j)),
        out_shape=jax.ShapeDtypeStruct((s, SUBLANES, n), F32),
        compiler_params=_params(("parallel", "parallel")),
        name="ada",
    )(c_pad, w, b.reshape(s, 1, n))


def _lane_block_sums(sq):
    acc = sq[:, :LANES]
    for j in range(1, sq.shape[1] // LANES):
        acc = acc + sq[:, j * LANES:(j + 1) * LANES]
    return acc


def _row_rsqrt(ssq_ref, rows, width):
    part = ssq_ref[0, rows, :]
    for p in range(1, ssq_ref.shape[0]):
        part = part + ssq_ref[p, rows, :]
    ms = jnp.sum(part, axis=-1, keepdims=True) * (1.0 / width)
    return lax.rsqrt(ms + RMS_EPS)


def _prep_kernel(x_ref, gs_ref, xg_ref, ssq_ref):
    x = x_ref[...]
    xg_ref[...] = (x * gs_ref[...]).astype(xg_ref.dtype)
    ssq_ref[0] = _lane_block_sums(x * x)


def _prep(x, gs, *, rows_per_batch, bl=512):
    m, d = x.shape
    per = rows_per_batch // bl
    nb = gs.shape[0]
    return pl.pallas_call(
        _prep_kernel,
        grid=(m // bl,),
        in_specs=[pl.BlockSpec((bl, d), lambda i: (i, 0)),
                  pl.BlockSpec((None, 1, d), lambda i: (i // per, 0, 0))],
        out_specs=[pl.BlockSpec((bl, d), lambda i: (i, 0)),
                   pl.BlockSpec((1, bl, LANES), lambda i: (0, i, 0))],
        out_shape=[jax.ShapeDtypeStruct((m, d), BF16), jax.ShapeDtypeStruct((1, m, LANES), F32)],
        compiler_params=_params(("parallel",)),
        name="prep",
    )(x, gs.reshape(nb, 1, d))


def _final_norm_kernel(x_ref, g_ref, o_ref):
    x = x_ref[...]
    ms = jnp.mean(x * x, axis=-1, keepdims=True)
    o_ref[...] = x * lax.rsqrt(ms + RMS_EPS) * g_ref[...]


def _final_norm(x, g, bl=512):
    m, d = x.shape
    return pl.pallas_call(
        _final_norm_kernel,
        grid=(m // bl,),
        in_specs=[pl.BlockSpec((bl, d), lambda i: (i, 0)),
                  pl.BlockSpec((1, d), lambda i: (0, 0))],
        out_specs=pl.BlockSpec((bl, d), lambda i: (i, 0)),
        out_shape=jax.ShapeDtypeStruct((m, d), F32),
        compiler_params=_params(("parallel",)),
        name="final_norm",
    )(x, g.reshape(1, d))


def _weight_spec(w, lead, bn, col0=0):
    k = w.shape[-2]
    mode = dict(pipeline_mode=pl.Buffered(1)) if bn == w.shape[-1] else {}
    return pl.BlockSpec((None,) * len(lead) + (k, bn), lambda j, i: tuple(lead) + (0, col0 + j), **mode)


def _weight_scratch(w, bn, count=1):
    return [pltpu.VMEM((w.shape[-2], bn), BF16)] * count


def _stage_weight(w_ref, wb_ref, shift_ref=None, sw_ref=None):
    @pl.when(pl.program_id(1) == 0)
    def _():
        wb_ref[...] = w_ref[...].astype(BF16)
        if sw_ref is not None:
            sw_ref[...] = jnp.dot(shift_ref[...], wb_ref[...], preferred_element_type=F32)


def _modulated_dot(xg_ref, wb_ref, ssq_ref, sw_ref, rows_per_batch, rows=None):
    bm = xg_ref.shape[0]
    rows = slice(0, bm) if rows is None else rows
    b = (pl.program_id(1) * bm + rows.start) // rows_per_batch
    acc = jnp.dot(xg_ref[rows, :], wb_ref[...], preferred_element_type=F32)
    return _row_rsqrt(ssq_ref, rows, xg_ref.shape[1]) * acc + sw_ref[pl.ds(b, 1), :]


def _mm_plain_kernel(x_ref, ssq_ref, shift_ref, w_ref, o_ref, wb_ref, sw_ref, *, per):
    _stage_weight(w_ref, wb_ref, shift_ref, sw_ref)
    o_ref[...] = _modulated_dot(x_ref, wb_ref, ssq_ref, sw_ref, per).astype(o_ref.dtype)


def _mm_swiglu_kernel(x_ref, ssq_ref, shift_ref, wg_ref, wu_ref, wo_ref, o_ref, wob_ref,
                      wgb_ref, wub_ref, swg_ref, swu_ref, *, per, sub):
    _stage_weight(wg_ref, wgb_ref, shift_ref, swg_ref)
    _stage_weight(wu_ref, wub_ref, shift_ref, swu_ref)
    step = x_ref.shape[0] // sub
    for r in range(sub):
        rows = slice(r * step, (r + 1) * step)
        g = _modulated_dot(x_ref, wgb_ref, ssq_ref, swg_ref, per, rows)
        u = _modulated_dot(x_ref, wub_ref, ssq_ref, swu_ref, per, rows)
        o_ref[rows, :] = (g * _sigmoid(g) * u).astype(o_ref.dtype)
    wob_ref[...] = wo_ref[...].astype(wob_ref.dtype)


def _mm_glu_kernel(x_ref, w_ref, y_ref, b_ref, o_ref, wb_ref):
    _stage_weight(w_ref, wb_ref)
    acc = jnp.dot(x_ref[...], wb_ref[...], preferred_element_type=F32)
    o_ref[...] = (y_ref[...].astype(F32) * _sigmoid(acc + b_ref[...])).astype(o_ref.dtype)


def _mm_rope_kernel(x_ref, ssq_ref, shift_ref, w_ref, cos_ref, sin_lo_ref, sin_hi_ref, o_ref,
                    wb_ref, sw_ref, *, per, out_scale, rope_groups):
    _stage_weight(w_ref, wb_ref, shift_ref, sw_ref)
    acc = _modulated_dot(x_ref, wb_ref, ssq_ref, sw_ref, per)
    cos, sin_lo, sin_hi = cos_ref[...], sin_lo_ref[...], sin_hi_ref[...]
    half = HEAD_DIM // 2
    for j in range(acc.shape[1] // LANES):
        a = acc[:, j * LANES:(j + 1) * LANES]
        if j < rope_groups:
            a = (a * cos + pltpu.roll(a, LANES - half, axis=1) * sin_lo
                 + pltpu.roll(a, half, axis=1) * sin_hi)
        o_ref[:, j * LANES:(j + 1) * LANES] = (a * out_scale).astype(o_ref.dtype)


def _mm_resid_kernel(x_ref, w_ref, r_ref, gate_ref, *rest, weight, n_next, staged, final):
    n_extra = 1 if final else n_next
    gs_refs, o_ref, xg_refs = rest[:n_extra], rest[n_extra], rest[n_extra + 1:2 * n_next + 1]
    if staged:
        wb_ref = rest[-1]
        _stage_weight(w_ref, wb_ref)
    else:
        wb_ref = w_ref
    acc = jnp.dot(x_ref[...], wb_ref[...], preferred_element_type=F32)
    x_new = r_ref[...] + (weight * (1.0 + gate_ref[...])) * acc
    if final:
        ms = jnp.mean(x_new * x_new, axis=-1, keepdims=True)
        o_ref[...] = x_new * lax.rsqrt(ms + RMS_EPS) * gs_refs[0][...]
        return
    o_ref[...] = x_new
    if not n_next:
        return
    for gs_ref, xg_ref in zip(gs_refs, xg_refs):
        xg_ref[...] = (x_new * gs_ref[...]).astype(xg_ref.dtype)
    ssq_ref = rest[2 * n_next + 1]
    ssq_ref[...] = _lane_block_sums(x_new * x_new)


_MM_SEM = ("arbitrary", "arbitrary")


def _modulated_specs(mod_in, bm):
    xg, ssq, _ = mod_in
    k = xg.shape[1]
    return [pl.BlockSpec((bm, k), lambda j, i: (i, 0)),
            pl.BlockSpec((ssq.shape[0], bm, LANES), lambda j, i: (0, i, 0)),
            pl.BlockSpec((SUBLANES, k), lambda j, i: (0, 0))]


def _mm_plain(mod_in, w, lead=(), *, out_dtype, rows_per_batch, ncols=None, col0=0, bm=1024, bn=1024):
    xg, ssq, shift = mod_in
    m, k = xg.shape
    n = w.shape[-1] if ncols is None else ncols
    bn = min(bn, n)
    return pl.pallas_call(
        functools.partial(_mm_plain_kernel, per=rows_per_batch),
        grid=(n // bn, m // bm),
        in_specs=_modulated_specs(mod_in, bm) + [_weight_spec(w, lead, bn, col0 // bn)],
        out_specs=pl.BlockSpec((bm, bn), lambda j, i: (i, j)),
        out_shape=jax.ShapeDtypeStruct((m, n), out_dtype),
        scratch_shapes=_weight_scratch(w, bn) + [pltpu.VMEM((SUBLANES, bn), F32)],
        compiler_params=_params(_MM_SEM),
        name="mm_plain",
    )(xg, ssq, shift, w)


def _mm_swiglu(mod_in, w_in, w_out, lead=(), *, rows_per_batch, bm=2048, bn=512, sub=2):
    xg, ssq, shift = mod_in
    m, k = xg.shape
    f = w_in.shape[-1] // 2
    nf, nm = f // bn, m // bm
    slab = f // (nf * nm)
    n_out = w_out.shape[-1]
    nl = len(lead)
    return pl.pallas_call(
        functools.partial(_mm_swiglu_kernel, per=rows_per_batch, sub=sub),
        grid=(nf, nm),
        in_specs=_modulated_specs(mod_in, bm) + [
            _weight_spec(w_in, lead, bn),
            _weight_spec(w_in, lead, bn, nf),
            pl.BlockSpec((None,) * nl + (slab, n_out), lambda j, i: tuple(lead) + (j * nm + i, 0))],
        out_specs=[pl.BlockSpec((bm, bn), lambda j, i: (i, j)),
                   pl.BlockSpec((slab, n_out), lambda j, i: (j * nm + i, 0))],
        out_shape=[jax.ShapeDtypeStruct((m, f), BF16), jax.ShapeDtypeStruct((f, n_out), BF16)],
        scratch_shapes=_weight_scratch(w_in, bn, 2) + [pltpu.VMEM((SUBLANES, bn), F32)] * 2,
        compiler_params=_params(_MM_SEM),
        name="mm_swiglu",
    )(xg, ssq, shift, w_in, w_in, w_out)


def _mm_glu(y, w, b, lead=(), *, bm=1024, bn=1024):
    m, k = y.shape
    n = w.shape[-1]
    return pl.pallas_call(
        _mm_glu_kernel,
        grid=(n // bn, m // bm),
        in_specs=[pl.BlockSpec((bm, k), lambda j, i: (i, 0)),
                  _weight_spec(w, lead, bn),
                  pl.BlockSpec((bm, bn), lambda j, i: (i, j)),
                  pl.BlockSpec((1, bn), lambda j, i: (0, j))],
        out_specs=pl.BlockSpec((bm, bn), lambda j, i: (i, j)),
        out_shape=jax.ShapeDtypeStruct((m, n), BF16),
        scratch_shapes=_weight_scratch(w, bn),
        compiler_params=_params(_MM_SEM),
        name="mm_glu",
    )(y, w, y, b.reshape(1, n))


def _mm_rope(mod_in, w, lead, rope, *, out_scale, out_dtype, rows_per_batch, rope_cols=None,
             bm=1024, bn=1024):
    xg, ssq, shift = mod_in
    m, k = xg.shape
    n = w.shape[-1]
    bn = min(bn, n)
    assert rope_cols is None or bn == n
    rope_groups = (bn if rope_cols is None else rope_cols) // LANES
    table = pl.BlockSpec((bm, LANES), lambda j, i: (i, 0))
    return pl.pallas_call(
        functools.partial(_mm_rope_kernel, per=rows_per_batch, out_scale=out_scale,
                          rope_groups=rope_groups),
        grid=(n // bn, m // bm),
        in_specs=_modulated_specs(mod_in, bm) + [_weight_spec(w, lead, bn), table, table, table],
        out_specs=pl.BlockSpec((bm, bn), lambda j, i: (i, j)),
        out_shape=jax.ShapeDtypeStruct((m, n), out_dtype),
        scratch_shapes=_weight_scratch(w, bn) + [pltpu.VMEM((SUBLANES, bn), F32)],
        compiler_params=_params(_MM_SEM),
        name="mm_rope",
    )(xg, ssq, shift, w, *rope)


def _mm_resid(x, w, lead, resid, gate, next_gs=(), *, weight, rows_per_batch, final_g=None,
              bm=512, bn=1024):
    m, k = x.shape
    n = w.shape[-1]
    final = final_g is not None
    if final:
        bm, bn = bm // 2, n
    per = rows_per_batch // bm
    nb = gate.shape[0]
    n_next = len(next_gs)
    staged = w.dtype != BF16
    tile = pl.BlockSpec((bm, bn), lambda j, i: (i, j))
    per_batch = pl.BlockSpec((None, 1, bn), lambda j, i: (i // per, 0, j))
    out_specs = [tile] * (1 + n_next)
    out_shape = [jax.ShapeDtypeStruct((m, n), F32)] + [jax.ShapeDtypeStruct((m, n), BF16)] * n_next
    if n_next:
        out_specs.append(pl.BlockSpec((None, bm, LANES), lambda j, i: (j, i, 0)))
        out_shape.append(jax.ShapeDtypeStruct((n // bn, m, LANES), F32))
    outs = pl.pallas_call(
        functools.partial(_mm_resid_kernel, weight=weight, n_next=n_next, staged=staged, final=final),
        grid=(n // bn, m // bm),
        in_specs=([pl.BlockSpec((bm, k), lambda j, i: (i, 0)),
                   _weight_spec(w, lead, bn), tile, per_batch] + [per_batch] * n_next
                  + [pl.BlockSpec((1, bn), lambda j, i: (0, j))] * final),
        out_specs=out_specs,
        out_shape=out_shape,
        scratch_shapes=_weight_scratch(w, bn) if staged else [],
        compiler_params=_params(_MM_SEM),
        name="mm_resid",
    )(x, w, resid, gate.reshape(nb, 1, n), *[gs.reshape(nb, 1, n) for gs in next_gs],
      *([final_g.reshape(1, n)] if final else []))
    return outs[0], list(outs[1:1 + n_next]), (outs[-1] if n_next else None)


def _rope_kernel(pos_ref, invf_ref, lo_ref, cos_ref, sin_lo_ref, sin_hi_ref):
    ang = pos_ref[...].astype(F32) * invf_ref[...]
    sin = jnp.sin(ang)
    lo = lo_ref[...]
    cos_ref[...] = jnp.cos(ang)
    sin_lo_ref[...] = -sin * lo
    sin_hi_ref[...] = sin * (1.0 - lo)


def _rope_tables(positions, bl=1024):
    m = positions.size
    half = HEAD_DIM // 2
    inv_freq = 1.0 / (ROPE_THETA ** (jnp.arange(0, HEAD_DIM, 2, dtype=F32) / HEAD_DIM))
    invf = jnp.tile(inv_freq, LANES // half).reshape(1, LANES)
    lo = jnp.asarray(np.arange(LANES) % HEAD_DIM < half, F32).reshape(1, LANES)
    return pl.pallas_call(
        _rope_kernel,
        grid=(m // bl,),
        in_specs=[pl.BlockSpec((bl, 1), lambda i: (i, 0)),
                  pl.BlockSpec((1, LANES), lambda i: (0, 0)),
                  pl.BlockSpec((1, LANES), lambda i: (0, 0))],
        out_specs=[pl.BlockSpec((bl, LANES), lambda i: (i, 0))] * 3,
        out_shape=[jax.ShapeDtypeStruct((m, LANES), F32)] * 3,
        compiler_params=_params(("parallel",)),
        name="rope_tables",
    )(positions.reshape(m, 1), invf, lo)


def _s5_input_map(v, bblk_ref, bu_ref, *, tc, pitch, nb):
    half_w = 4 * LANES
    vb = v.reshape(nb * tc, S5_TILE).astype(BF16)
    for half in range(2):
        r = jnp.dot(vb, bblk_ref[:, half * 2 * half_w:(half + 1) * 2 * half_w],
                    preferred_element_type=F32)
        for b in range(nb):
            for l in range(8):
                bu_ref[l, pl.ds((half * nb + b) * pitch, tc), :] = (
                    r[b * tc:(b + 1) * tc, l * LANES:(l + 1) * LANES])


def _s5_scan(bu_ref, st_ref, a, xs, *, tc, pitch):
    xs = list(xs)
    for t in range(tc):
        for l in range(4):
            bur = bu_ref[l, pl.ds(t, SUBLANES, stride=pitch), :]
            bui = bu_ref[4 + l, pl.ds(t, SUBLANES, stride=pitch), :]
            xr, xi = xs[l], xs[4 + l]
            nr = a[l] * xr - a[4 + l] * xi + bur
            ni = a[l] * xi + a[4 + l] * xr + bui
            st_ref[l, pl.ds(t, SUBLANES, stride=pitch), :] = nr
            st_ref[4 + l, pl.ds(t, SUBLANES, stride=pitch), :] = ni
            xs[l], xs[4 + l] = nr, ni
    return xs


def _s5_output_map(st_ref, cblk_ref, v, d, *, tc, pitch, nb):
    half_w = 4 * LANES
    acc = None
    for half in range(2):
        blocks = []
        for b in range(nb):
            blocks.append(jnp.concatenate(
                [st_ref[l, pl.ds((half * nb + b) * pitch, tc), :] for l in range(8)], axis=1))
        s = jnp.concatenate(blocks, axis=0).astype(BF16)
        part = jnp.dot(s, cblk_ref[half * 2 * half_w:(half + 1) * 2 * half_w, :],
                       preferred_element_type=F32)
        acc = part if acc is None else acc + part
    y = _gelu_tanh(acc + d * v.reshape(nb * tc, S5_TILE))
    return y.reshape(nb, tc, S5_TILE)


def _s5_kernel(va_ref, vc_ref, bblk_ref, cblk_ref, a0_ref, a1_ref, d_ref, y_ref,
               bu0_ref, bu1_ref, st0_ref, st1_ref, x_ref, *, tc, pitch, nb, steps_per_tile):
    s = pl.program_id(0)
    kw = dict(tc=tc, pitch=pitch, nb=nb)

    @pl.when(s == 0)
    def _():
        x_ref[...] = jnp.zeros_like(x_ref)
        bu1_ref[...] = jnp.zeros_like(bu1_ref)
        st0_ref[...] = jnp.zeros_like(st0_ref)

    bus, sts = (bu0_ref, bu1_ref), (st0_ref, st1_ref)
    d = d_ref[...]
    xs = [x_ref[l] for l in range(8)]
    for h, a_ref in enumerate((a0_ref, a1_ref)):
        _s5_input_map(va_ref[:, h * tc:(h + 1) * tc, :], bblk_ref, bus[h], **kw)
        if h == 1:
            xs = [jnp.where(s % steps_per_tile == 0, 0.0, x) for x in xs]
        xs = _s5_scan(bus[1 - h], sts[1 - h], [a_ref[l] for l in range(8)], xs, tc=tc, pitch=pitch)
        y = _s5_output_map(sts[h], cblk_ref, vc_ref[:, h * tc:(h + 1) * tc, :], d, **kw)
        y_ref[:, h * tc:(h + 1) * tc, :] = y.astype(y_ref.dtype)
    for l in range(8):
        x_ref[l] = xs[l]


def _s5_discretise(a_re, a_im, b_re, b_im, c_re, c_im, log_dt, nb):
    g, p = a_re.shape
    gpt = S5_TILE // S5_GROUP
    nt = g // gpt
    dt = jnp.exp(log_dt)[:, None]
    mag = jnp.exp(a_re * dt)
    lb_re, lb_im = mag * jnp.cos(a_im * dt), mag * jnp.sin(a_im * dt)
    den = a_re * a_re + a_im * a_im
    k_re = ((lb_re - 1.0) * a_re + lb_im * a_im) / den
    k_im = (lb_im * a_re - (lb_re - 1.0) * a_im) / den
    bb_re = k_re[..., None] * b_re - k_im[..., None] * b_im
    bb_im = k_re[..., None] * b_im + k_im[..., None] * b_re
    g8 = gpt // 2
    nstate = 2 * gpt * p

    def state_cols(t):
        x = t.shape[-1]
        return t.reshape(nt, 2, g8, p, x).transpose(0, 4, 1, 2, 3).reshape(nt, x, 2, g8 * p)

    col = np.arange(nstate)
    col_group = col // (2 * g8 * p) * g8 + col % (g8 * p) // p
    own = jnp.asarray(np.arange(S5_TILE)[:, None] // S5_GROUP == col_group[None, :], F32)
    m_in = jnp.stack([state_cols(bb_re), state_cols(bb_im)], axis=3).reshape(nt, S5_GROUP, nstate)
    bblk = jnp.tile(m_in, (1, gpt, 1)) * own
    m_out = jnp.stack([state_cols(c_re.swapaxes(1, 2)), state_cols(-c_im.swapaxes(1, 2))],
                      axis=3).reshape(nt, S5_GROUP, nstate).swapaxes(1, 2)
    cblk = jnp.tile(m_out, (1, 1, gpt)) * own.T

    def a_slabs(lb):
        t = lb.reshape(nt, 2, 4, LANES)
        t = jnp.broadcast_to(t[:, :, None], (nt, 2, nb, 4, LANES))
        return t.transpose(0, 3, 1, 2, 4).reshape(nt, 4, 2 * nb, LANES)

    a = jnp.concatenate([a_slabs(lb_re), a_slabs(lb_im)], axis=1)
    return bblk.astype(BF16), cblk.astype(BF16), a


def _s5_core(v, bblk, cblk, a, d_skip, *, tc=128):
    nb, seq, d = v.shape
    assert 2 * nb == SUBLANES
    nt = d // S5_TILE
    spt = seq // (2 * tc)
    steps = nt * spt + 1
    pitch = tc + 4
    nstate = bblk.shape[2]
    slab = pltpu.VMEM((8, SUBLANES * pitch, LANES), F32)

    def pair_in(s):
        return jnp.minimum(s, steps - 2)

    def pair_out(s):
        return jnp.maximum(s - 1, 0)

    def tile_scan0(s):
        return jnp.maximum(2 * s - 1, 0) // (2 * spt)

    chunk = (nb, 2 * tc, S5_TILE)
    a_block = (None, 8, SUBLANES, LANES)
    return pl.pallas_call(
        functools.partial(_s5_kernel, tc=tc, pitch=pitch, nb=nb, steps_per_tile=spt),
        grid=(steps,),
        in_specs=[pl.BlockSpec(chunk, lambda s: (0, pair_in(s) % spt, pair_in(s) // spt)),
                  pl.BlockSpec(chunk, lambda s: (0, pair_out(s) % spt, pair_out(s) // spt)),
                  pl.BlockSpec((None, S5_TILE, nstate), lambda s: (pair_in(s) // spt, 0, 0)),
                  pl.BlockSpec((None, nstate, S5_TILE), lambda s: (pair_out(s) // spt, 0, 0)),
                  pl.BlockSpec(a_block, lambda s: (tile_scan0(s), 0, 0, 0)),
                  pl.BlockSpec(a_block, lambda s: (pair_in(s) // spt, 0, 0, 0)),
                  pl.BlockSpec((1, S5_TILE), lambda s: (0, pair_out(s) // spt))],
        out_specs=pl.BlockSpec(chunk, lambda s: (0, pair_out(s) % spt, pair_out(s) // spt)),
        out_shape=jax.ShapeDtypeStruct((nb, seq, d), BF16),
        scratch_shapes=[slab, slab, slab, slab, pltpu.VMEM((8, SUBLANES, LANES), F32)],
        compiler_params=_params(("arbitrary",)),
        name="s5_core",
    )(v, v, bblk, cblk, a, a, d_skip.reshape(1, d))


def _attn_kernel(sink_ref, bias_ref, q_ref, kp_ref, kc_ref, vp_ref, vc_ref, o_ref, *, n_kv):
    blk = q_ref.shape[0]
    lane = lax.broadcasted_iota(jnp.int32, (2 * blk, LANES), 1)
    pairs = Q_PER_KV // 2
    nt_dims = (((1,), (1,)), ((), ()))

    def on_slot(x, slot, s):
        x = x if s == slot else pltpu.roll(x, HEAD_DIM, axis=1)
        return jnp.where(lane // HEAD_DIM == s, x, 0.0)

    for kvh in range(n_kv):
        grp, slot = kvh // 2, kvh % 2
        cols = slice(grp * LANES, (grp + 1) * LANES)
        kcat = jnp.concatenate([kp_ref[:, cols], kc_ref[:, cols]], axis=0)
        vcat = jnp.concatenate([vp_ref[:, cols], vc_ref[:, cols]], axis=0)
        base = kvh * pairs
        qg = jnp.concatenate([q_ref[:, (base + p) * LANES:(base + p + 1) * LANES]
                              for p in range(pairs)], axis=0)
        out_t = None
        for s in range(2):
            kexp = on_slot(kcat, slot, s).astype(BF16)
            vexp_t = on_slot(vcat, slot, s).T.astype(BF16)
            es, rdens = [], []
            for p in range(pairs):
                sink = sink_ref[kvh * Q_PER_KV + 2 * p + s]
                qp = q_ref[:, (base + p) * LANES:(base + p + 1) * LANES]
                sc = lax.dot_general(kexp, qp, nt_dims, preferred_element_type=F32) + bias_ref[...]
                mx = jnp.maximum(jnp.max(sc, axis=0, keepdims=True), sink)
                e = jnp.exp2(sc - mx)
                den = jnp.sum(e, axis=0, keepdims=True) + jnp.exp2(sink - mx)
                es.append(e.astype(BF16))
                rdens.append(1.0 / den)
            part = jnp.dot(vexp_t, jnp.concatenate(es, axis=1),
                           preferred_element_type=F32) * jnp.concatenate(rdens, axis=1)
            out_t = part if out_t is None else out_t + part
        out = out_t.T
        for p in range(pairs):
            o_ref[:, (base + p) * LANES:(base + p + 1) * LANES] = (
                out[p * blk:(p + 1) * blk].astype(o_ref.dtype))


def _attention(q, kv, sinks, *, nbatch, blk=WINDOW):
    m, dq = q.shape
    dkv = kv.shape[1] // 2
    nblk = m // nbatch // blk
    kj, qi = np.arange(2 * blk)[:, None], np.arange(blk)[None, :]
    window = (kj > qi) & (kj <= qi + blk)
    bias = np.where(np.stack([window & (kj >= blk), window]), 0.0, -1e30).astype(np.float32)
    def rows(prev, col):
        return lambda b, n: (b * nblk + (jnp.maximum(n - 1, 0) if prev else n), col)

    return pl.pallas_call(
        functools.partial(_attn_kernel, n_kv=dkv // HEAD_DIM),
        grid=(nbatch, nblk),
        in_specs=[pl.BlockSpec(memory_space=pltpu.SMEM),
                  pl.BlockSpec((None, 2 * blk, blk), lambda b, n: (jnp.minimum(n, 1), 0, 0)),
                  pl.BlockSpec((blk, dq), rows(False, 0)),
                  pl.BlockSpec((blk, dkv), rows(True, 0)),
                  pl.BlockSpec((blk, dkv), rows(False, 0)),
                  pl.BlockSpec((blk, dkv), rows(True, 1)),
                  pl.BlockSpec((blk, dkv), rows(False, 1))],
        out_specs=pl.BlockSpec((blk, dq), rows(False, 0)),
        out_shape=jax.ShapeDtypeStruct((m, dq), BF16),
        compiler_params=_params(("parallel", "arbitrary")),
        name="attention",
    )(sinks * math.log2(math.e), jnp.asarray(bias), q, kv, kv, kv, kv)


def kernel(x, c, positions, norm_g, w_ada, b_ada, w_ff_in, w_ff_out, s5_w_in, s5_a_re, s5_a_im, s5_b_re, s5_b_im, s5_c_re, s5_c_im, s5_d, s5_log_dt, s5_w_glu, s5_b_glu, s5_w_out, kv_norm_g, w_ada_kv, b_ada_kv, w_kv, attn_w_q, attn_sinks, attn_w_o, final_norm_g):
    nbatch, seq, d = x.shape
    depth = norm_g.shape[0]
    n_s5 = s5_w_in.shape[0]
    m = nbatch * seq
    kvw = w_kv.shape[1] // 2

    c_pad = jnp.zeros((SUBLANES, d), F32).at[:nbatch].set(c)
    mods = _ada(c_pad, w_ada, b_ada)[:, :nbatch]
    mod_kv = _ada(c_pad, w_ada_kv[None], b_ada_kv[None])[0, :nbatch]
    rope = _rope_tables(positions)

    per_batch = dict(rows_per_batch=seq)
    mod = mods.reshape(depth, nbatch, N_SUBLAYERS, 3, d)
    shift, scale, gate = mod[:, :, :, 0], mod[:, :, :, 1], mod[:, :, :, 2]

    def gs_of(layer, sub):
        return norm_g[layer, sub][None, :] * (1.0 + scale[layer, :, sub])

    def shift_rows(sh):
        return jnp.zeros((SUBLANES, d), F32).at[:nbatch].set(sh).astype(BF16)

    def ffn(mod_in, xs, layer, which, next_gs, final_g=None):
        act, w_out = _mm_swiglu(mod_in, w_ff_in, w_ff_out, (layer, which), **per_batch)
        return _mm_resid(act, w_out, (), xs, gate[layer, :, 2 * which], next_gs,
                         weight=0.5, final_g=final_g, **per_batch)

    xs = x.reshape(m, d)
    xg, ssq = _prep(xs, gs_of(0, 0), **per_batch)
    kv = None
    for layer in range(depth):
        xs, (xg,), ssq = ffn((xg, ssq, shift_rows(shift[layer, :, 0])), xs, layer, 0, [gs_of(layer, 1)])

        mix_in = (xg, ssq, shift_rows(shift[layer, :, 1]))
        if layer < n_s5:
            i = layer
            v = _mm_plain(mix_in, s5_w_in, (i,), out_dtype=F32, **per_batch)
            bblk, cblk, a = _s5_discretise(s5_a_re[i], s5_a_im[i], s5_b_re[i], s5_b_im[i],
                                           s5_c_re[i], s5_c_im[i], s5_log_dt[i], nbatch)
            y = _s5_core(v.reshape(nbatch, seq, d), bblk, cblk, a, s5_d[i].reshape(-1)).reshape(m, d)
            mixed = _mm_glu(y, s5_w_glu, s5_b_glu[i], (i,))
            w_mix_out, lead = s5_w_out, (i,)
        else:
            j = layer - n_s5
            q = _mm_rope(mix_in, attn_w_q, (j,), rope, out_dtype=BF16,
                         out_scale=HEAD_DIM ** -0.5 * math.log2(math.e), **per_batch)
            mixed = _attention(q, kv, attn_sinks[j], nbatch=nbatch)
            w_mix_out, lead = attn_w_o, (j,)
        xs, (xg,), ssq = _mm_resid(mixed, w_mix_out, lead, xs, gate[layer, :, 1], [gs_of(layer, 2)],
                                   weight=1.0, **per_batch)

        next_gs = [gs_of(layer + 1, 0)] if layer + 1 < depth else []
        if layer == n_s5 - 1:
            next_gs.append(kv_norm_g[None, :] * (1.0 + mod_kv[:, d:]))
        fuse_final = layer + 1 == depth and not next_gs
        xs, xgs, ssq = ffn((xg, ssq, shift_rows(shift[layer, :, 2])), xs, layer, 1, next_gs,
                           final_norm_g if fuse_final else None)
        if layer == n_s5 - 1:
            kv_in = (xgs[-1], ssq, shift_rows(mod_kv[:, :d]))
            kv = _mm_rope(kv_in, w_kv, (), rope, out_scale=1.0, out_dtype=F32, rope_cols=kvw,
                          **per_batch)
        if layer + 1 < depth:
            xg = xgs[0]

    out = xs if fuse_final else _final_norm(xs, final_norm_g)
    return out.reshape(nbatch, seq, d)
```

```python
import functools
import math

import numpy as np
import jax
import jax.numpy as jnp
from jax import lax
from jax.experimental import pallas as pl
from jax.experimental.pallas import tpu as pltpu

F32 = jnp.float32
BF16 = jnp.bfloat16

RMS_EPS = 1e-6
N_SUBLAYERS = 3
S5_GROUP = 16
S5_STATE = 64
HEAD_DIM = 64
Q_PER_KV = 8
WINDOW = 128
ROPE_THETA = 10000.0

LANES = 128
SUBLANES = 8
S5_TILE = 256
VMEM_LIMIT = 56 * 1024 * 1024


def _params(sem):
    return pltpu.CompilerParams(dimension_semantics=sem, vmem_limit_bytes=VMEM_LIMIT)


def _sigmoid(x):
    return 1.0 / (1.0 + jnp.exp(-x))


def _gelu_tanh(x):
    return 0.5 * x * (1.0 + jnp.tanh(math.sqrt(2.0 / math.pi) * (x + 0.044715 * (x * x * x))))


def _ada_kernel(c_ref, w_ref, b_ref, o_ref):
    c = c_ref[...]
    ca = (c * _sigmoid(c)).astype(BF16)
    o_ref[...] = jnp.dot(ca, w_ref[...].astype(BF16), preferred_element_type=F32) + b_ref[...]


def _ada(c_pad, w, b, bn=2048):
    s, d, n = w.shape
    return pl.pallas_call(
        _ada_kernel,
        grid=(s, n // bn),
        in_specs=[pl.BlockSpec((SUBLANES, d), lambda i, j: (0, 0)),
                  pl.BlockSpec((None, d, bn), lambda i, j: (i, 0, j)),
                  pl.BlockSpec((None, 1, bn), lambda i, j: (i, 0, j))],
        out_specs=pl.BlockSpec((None, SUBLANES, bn), lambda i, j: (i, 0, j)),
        out_shape=jax.ShapeDtypeStruct((s, SUBLANES, n), F32),
        compiler_params=_params(("parallel", "parallel")),
        name="ada",
    )(c_pad, w, b.reshape(s, 1, n))


def _lane_block_sums(sq):
    acc = sq[:, :LANES]
    for j in range(1, sq.shape[1] // LANES):
        acc = acc + sq[:, j * LANES:(j + 1) * LANES]
    return acc


def _row_rsqrt(ssq_ref, rows, width):
    part = ssq_ref[0, rows, :]
    for p in range(1, ssq_ref.shape[0]):
        part = part + ssq_ref[p, rows, :]
    ms = jnp.sum(part, axis=-1, keepdims=True) * (1.0 / width)
    return lax.rsqrt(ms + RMS_EPS)


def _prep_kernel(x_ref, gs_ref, xg_ref, ssq_ref):
    x = x_ref[...]
    xg_ref[...] = (x * gs_ref[...]).astype(xg_ref.dtype)
    ssq_ref[0] = _lane_block_sums(x * x)


def _prep(x, gs, *, rows_per_batch, bl=512):
    m, d = x.shape
    per = rows_per_batch // bl
    nb = gs.shape[0]
    return pl.pallas_call(
        _prep_kernel,
        grid=(m // bl,),
        in_specs=[pl.BlockSpec((bl, d), lambda i: (i, 0)),
                  pl.BlockSpec((None, 1, d), lambda i: (i // per, 0, 0))],
        out_specs=[pl.BlockSpec((bl, d), lambda i: (i, 0)),
                   pl.BlockSpec((1, bl, LANES), lambda i: (0, i, 0))],
        out_shape=[jax.ShapeDtypeStruct((m, d), BF16), jax.ShapeDtypeStruct((1, m, LANES), F32)],
        compiler_params=_params(("parallel",)),
        name="prep",
    )(x, gs.reshape(nb, 1, d))


def _final_norm_kernel(x_ref, g_ref, o_ref):
    x = x_ref[...]
    ms = jnp.mean(x * x, axis=-1, keepdims=True)
    o_ref[...] = x * lax.rsqrt(ms + RMS_EPS) * g_ref[...]


def _final_norm(x, g, bl=512):
    m, d = x.shape
    return pl.pallas_call(
        _final_norm_kernel,
        grid=(m // bl,),
        in_specs=[pl.BlockSpec((bl, d), lambda i: (i, 0)),
                  pl.BlockSpec((1, d), lambda i: (0, 0))],
        out_specs=pl.BlockSpec((bl, d), lambda i: (i, 0)),
        out_shape=jax.ShapeDtypeStruct((m, d), F32),
        compiler_params=_params(("parallel",)),
        name="final_norm",
    )(x, g.reshape(1, d))


def _weight_spec(w, lead, bn, col0=0):
    k = w.shape[-2]
    mode = dict(pipeline_mode=pl.Buffered(1)) if bn == w.shape[-1] else {}
    return pl.BlockSpec((None,) * len(lead) + (k, bn), lambda j, i: tuple(lead) + (0, col0 + j), **mode)


def _weight_scratch(w, bn, count=1):
    return [pltpu.VMEM((w.shape[-2], bn), BF16)] * count


def _stage_weight(w_ref, wb_ref, shift_ref=None, sw_ref=None):
    @pl.when(pl.program_id(1) == 0)
    def _():
        wb_ref[...] = w_ref[...].astype(BF16)
        if sw_ref is not None:
            sw_ref[...] = jnp.dot(shift_ref[...], wb_ref[...], preferred_element_type=F32)


def _modulated_dot(xg_ref, wb_ref, ssq_ref, sw_ref, rows_per_batch, rows=None):
    bm = xg_ref.shape[0]
    rows = slice(0, bm) if rows is None else rows
    b = (pl.program_id(1) * bm + rows.start) // rows_per_batch
    acc = jnp.dot(xg_ref[rows, :], wb_ref[...], preferred_element_type=F32)
    return _row_rsqrt(ssq_ref, rows, xg_ref.shape[1]) * acc + sw_ref[pl.ds(b, 1), :]


def _mm_plain_kernel(x_ref, ssq_ref, shift_ref, w_ref, o_ref, wb_ref, sw_ref, *, per):
    _stage_weight(w_ref, wb_ref, shift_ref, sw_ref)
    o_ref[...] = _modulated_dot(x_ref, wb_ref, ssq_ref, sw_ref, per).astype(o_ref.dtype)


def _mm_swiglu_kernel(x_ref, ssq_ref, shift_ref, wg_ref, wu_ref, wo_ref, o_ref, wob_ref,
                      wgb_ref, wub_ref, swg_ref, swu_ref, *, per, sub):
    _stage_weight(wg_ref, wgb_ref, shift_ref, swg_ref)
    _stage_weight(wu_ref, wub_ref, shift_ref, swu_ref)
    step = x_ref.shape[0] // sub
    for r in range(sub):
        rows = slice(r * step, (r + 1) * step)
        g = _modulated_dot(x_ref, wgb_ref, ssq_ref, swg_ref, per, rows)
        u = _modulated_dot(x_ref, wub_ref, ssq_ref, swu_ref, per, rows)
        o_ref[rows, :] = (g * _sigmoid(g) * u).astype(o_ref.dtype)
    wob_ref[...] = wo_ref[...].astype(wob_ref.dtype)


def _mm_glu_kernel(x_ref, w_ref, y_ref, b_ref, o_ref, wb_ref):
    _stage_weight(w_ref, wb_ref)
    acc = jnp.dot(x_ref[...], wb_ref[...], preferred_element_type=F32)
    o_ref[...] = (y_ref[...].astype(F32) * _sigmoid(acc + b_ref[...])).astype(o_ref.dtype)


def _mm_rope_kernel(x_ref, ssq_ref, shift_ref, w_ref, cos_ref, sin_lo_ref, sin_hi_ref, o_ref,
                    wb_ref, sw_ref, *, per, out_scale, rope_groups):
    _stage_weight(w_ref, wb_ref, shift_ref, sw_ref)
    acc = _modulated_dot(x_ref, wb_ref, ssq_ref, sw_ref, per)
    cos, sin_lo, sin_hi = cos_ref[...], sin_lo_ref[...], sin_hi_ref[...]
    half = HEAD_DIM // 2
    for j in range(acc.shape[1] // LANES):
        a = acc[:, j * LANES:(j + 1) * LANES]
        if j < rope_groups:
            a = (a * cos + pltpu.roll(a, LANES - half, axis=1) * sin_lo
                 + pltpu.roll(a, half, axis=1) * sin_hi)
        o_ref[:, j * LANES:(j + 1) * LANES] = (a * out_scale).astype(o_ref.dtype)


def _mm_resid_kernel(x_ref, w_ref, r_ref, gate_ref, *rest, weight, n_next, staged, final, sub):
    n_extra = 1 if final else n_next
    gs_refs, o_ref, xg_refs = rest[:n_extra], rest[n_extra], rest[n_extra + 1:2 * n_next + 1]
    if staged:
        wb_ref = rest[-1]
        _stage_weight(w_ref, wb_ref)
    else:
        wb_ref = w_ref
    coef = weight * (1.0 + gate_ref[...])
    step = x_ref.shape[0] // sub
    for r in range(sub):
        rows = slice(r * step, (r + 1) * step)
        acc = jnp.dot(x_ref[rows, :], wb_ref[...], preferred_element_type=F32)
        x_new = r_ref[rows, :] + coef * acc
        if final:
            ms = jnp.mean(x_new * x_new, axis=-1, keepdims=True)
            o_ref[rows, :] = x_new * lax.rsqrt(ms + RMS_EPS) * gs_refs[0][...]
            continue
        o_ref[rows, :] = x_new
        for gs_ref, xg_ref in zip(gs_refs, xg_refs):
            xg_ref[rows, :] = (x_new * gs_ref[...]).astype(xg_ref.dtype)
        if n_next:
            rest[2 * n_next + 1][rows, :] = _lane_block_sums(x_new * x_new)


_MM_SEM = ("arbitrary", "arbitrary")


def _modulated_specs(mod_in, bm):
    xg, ssq, _ = mod_in
    k = xg.shape[1]
    return [pl.BlockSpec((bm, k), lambda j, i: (i, 0)),
            pl.BlockSpec((ssq.shape[0], bm, LANES), lambda j, i: (0, i, 0)),
            pl.BlockSpec((SUBLANES, k), lambda j, i: (0, 0))]


def _mm_plain(mod_in, w, lead=(), *, out_dtype, rows_per_batch, ncols=None, col0=0, bm=1024, bn=1024):
    xg, ssq, shift = mod_in
    m, k = xg.shape
    n = w.shape[-1] if ncols is None else ncols
    bn = min(bn, n)
    return pl.pallas_call(
        functools.partial(_mm_plain_kernel, per=rows_per_batch),
        grid=(n // bn, m // bm),
        in_specs=_modulated_specs(mod_in, bm) + [_weight_spec(w, lead, bn, col0 // bn)],
        out_specs=pl.BlockSpec((bm, bn), lambda j, i: (i, j)),
        out_shape=jax.ShapeDtypeStruct((m, n), out_dtype),
        scratch_shapes=_weight_scratch(w, bn) + [pltpu.VMEM((SUBLANES, bn), F32)],
        compiler_params=_params(_MM_SEM),
        name="mm_plain",
    )(xg, ssq, shift, w)


def _mm_swiglu(mod_in, w_in, w_out, lead=(), *, rows_per_batch, bm=2048, bn=512, sub=2):
    xg, ssq, shift = mod_in
    m, k = xg.shape
    f = w_in.shape[-1] // 2
    nf, nm = f // bn, m // bm
    slab = f // (nf * nm)
    n_out = w_out.shape[-1]
    nl = len(lead)
    return pl.pallas_call(
        functools.partial(_mm_swiglu_kernel, per=rows_per_batch, sub=sub),
        grid=(nf, nm),
        in_specs=_modulated_specs(mod_in, bm) + [
            _weight_spec(w_in, lead, bn),
            _weight_spec(w_in, lead, bn, nf),
            pl.BlockSpec((None,) * nl + (slab, n_out), lambda j, i: tuple(lead) + (j * nm + i, 0))],
        out_specs=[pl.BlockSpec((bm, bn), lambda j, i: (i, j)),
                   pl.BlockSpec((slab, n_out), lambda j, i: (j * nm + i, 0))],
        out_shape=[jax.ShapeDtypeStruct((m, f), BF16), jax.ShapeDtypeStruct((f, n_out), BF16)],
        scratch_shapes=_weight_scratch(w_in, bn, 2) + [pltpu.VMEM((SUBLANES, bn), F32)] * 2,
        compiler_params=_params(_MM_SEM),
        name="mm_swiglu",
    )(xg, ssq, shift, w_in, w_in, w_out)


def _mm_glu(y, w, b, lead=(), *, bm=1024, bn=1024):
    m, k = y.shape
    n = w.shape[-1]
    return pl.pallas_call(
        _mm_glu_kernel,
        grid=(n // bn, m // bm),
        in_specs=[pl.BlockSpec((bm, k), lambda j, i: (i, 0)),
                  _weight_spec(w, lead, bn),
                  pl.BlockSpec((bm, bn), lambda j, i: (i, j)),
                  pl.BlockSpec((1, bn), lambda j, i: (0, j))],
        out_specs=pl.BlockSpec((bm, bn), lambda j, i: (i, j)),
        out_shape=jax.ShapeDtypeStruct((m, n), BF16),
        scratch_shapes=_weight_scratch(w, bn),
        compiler_params=_params(_MM_SEM),
        name="mm_glu",
    )(y, w, y, b.reshape(1, n))


def _mm_rope(mod_in, w, lead, rope, *, out_scale, out_dtype, rows_per_batch, rope_cols=None,
             bm=1024, bn=1024):
    xg, ssq, shift = mod_in
    m, k = xg.shape
    n = w.shape[-1]
    bn = min(bn, n)
    assert rope_cols is None or bn == n
    rope_groups = (bn if rope_cols is None else rope_cols) // LANES
    table = pl.BlockSpec((bm, LANES), lambda j, i: (i, 0))
    return pl.pallas_call(
        functools.partial(_mm_rope_kernel, per=rows_per_batch, out_scale=out_scale,
                          rope_groups=rope_groups),
        grid=(n // bn, m // bm),
        in_specs=_modulated_specs(mod_in, bm) + [_weight_spec(w, lead, bn), table, table, table],
        out_specs=pl.BlockSpec((bm, bn), lambda j, i: (i, j)),
        out_shape=jax.ShapeDtypeStruct((m, n), out_dtype),
        scratch_shapes=_weight_scratch(w, bn) + [pltpu.VMEM((SUBLANES, bn), F32)],
        compiler_params=_params(_MM_SEM),
        name="mm_rope",
    )(xg, ssq, shift, w, *rope)


def _mm_resid(x, w, lead, resid, gate, next_gs=(), *, weight, rows_per_batch, final_g=None,
              bm=512, bn=1024, sub=1):
    m, k = x.shape
    n = w.shape[-1]
    final = final_g is not None
    if final:
        bm, bn = bm // 2, n
    per = rows_per_batch // bm
    nb = gate.shape[0]
    n_next = len(next_gs)
    staged = w.dtype != BF16
    tile = pl.BlockSpec((bm, bn), lambda j, i: (i, j))
    per_batch = pl.BlockSpec((None, 1, bn), lambda j, i: (i // per, 0, j))
    out_specs = [tile] * (1 + n_next)
    out_shape = [jax.ShapeDtypeStruct((m, n), F32)] + [jax.ShapeDtypeStruct((m, n), BF16)] * n_next
    if n_next:
        out_specs.append(pl.BlockSpec((None, bm, LANES), lambda j, i: (j, i, 0)))
        out_shape.append(jax.ShapeDtypeStruct((n // bn, m, LANES), F32))
    outs = pl.pallas_call(
        functools.partial(_mm_resid_kernel, weight=weight, n_next=n_next, staged=staged, final=final,
                          sub=sub),
        grid=(n // bn, m // bm),
        in_specs=([pl.BlockSpec((bm, k), lambda j, i: (i, 0)),
                   _weight_spec(w, lead, bn), tile, per_batch] + [per_batch] * n_next
                  + [pl.BlockSpec((1, bn), lambda j, i: (0, j))] * final),
        out_specs=out_specs,
        out_shape=out_shape,
        scratch_shapes=_weight_scratch(w, bn) if staged else [],
        compiler_params=_params(_MM_SEM),
        name="mm_resid",
    )(x, w, resid, gate.reshape(nb, 1, n), *[gs.reshape(nb, 1, n) for gs in next_gs],
      *([final_g.reshape(1, n)] if final else []))
    return outs[0], list(outs[1:1 + n_next]), (outs[-1] if n_next else None)


def _rope_kernel(pos_ref, invf_ref, lo_ref, cos_ref, sin_lo_ref, sin_hi_ref):
    ang = pos_ref[...].astype(F32) * invf_ref[...]
    sin = jnp.sin(ang)
    lo = lo_ref[...]
    cos_ref[...] = jnp.cos(ang)
    sin_lo_ref[...] = -sin * lo
    sin_hi_ref[...] = sin * (1.0 - lo)


def _rope_tables(positions, bl=1024):
    m = positions.size
    half = HEAD_DIM // 2
    inv_freq = 1.0 / (ROPE_THETA ** (jnp.arange(0, HEAD_DIM, 2, dtype=F32) / HEAD_DIM))
    invf = jnp.tile(inv_freq, LANES // half).reshape(1, LANES)
    lo = jnp.asarray(np.arange(LANES) % HEAD_DIM < half, F32).reshape(1, LANES)
    return pl.pallas_call(
        _rope_kernel,
        grid=(m // bl,),
        in_specs=[pl.BlockSpec((bl, 1), lambda i: (i, 0)),
                  pl.BlockSpec((1, LANES), lambda i: (0, 0)),
                  pl.BlockSpec((1, LANES), lambda i: (0, 0))],
        out_specs=[pl.BlockSpec((bl, LANES), lambda i: (i, 0))] * 3,
        out_shape=[jax.ShapeDtypeStruct((m, LANES), F32)] * 3,
        compiler_params=_params(("parallel",)),
        name="rope_tables",
    )(positions.reshape(m, 1), invf, lo)


def _s5_input_map(v, bblk_ref, bu_ref, *, tc, pitch, nb):
    half_w = 4 * LANES
    vb = v.reshape(nb * tc, S5_TILE).astype(BF16)
    for half in range(2):
        r = jnp.dot(vb, bblk_ref[:, half * 2 * half_w:(half + 1) * 2 * half_w],
                    preferred_element_type=F32)
        for b in range(nb):
            for l in range(8):
                bu_ref[l, pl.ds((half * nb + b) * pitch, tc), :] = (
                    r[b * tc:(b + 1) * tc, l * LANES:(l + 1) * LANES])


def _s5_scan(bu_ref, st_ref, a, xs, *, tc, pitch):
    xs = list(xs)
    for t in range(tc):
        for l in range(4):
            bur = bu_ref[l, pl.ds(t, SUBLANES, stride=pitch), :]
            bui = bu_ref[4 + l, pl.ds(t, SUBLANES, stride=pitch), :]
            xr, xi = xs[l], xs[4 + l]
            nr = a[l] * xr - a[4 + l] * xi + bur
            ni = a[l] * xi + a[4 + l] * xr + bui
            st_ref[l, pl.ds(t, SUBLANES, stride=pitch), :] = nr
            st_ref[4 + l, pl.ds(t, SUBLANES, stride=pitch), :] = ni
            xs[l], xs[4 + l] = nr, ni
    return xs


def _s5_output_map(st_ref, cblk_ref, v, d, *, tc, pitch, nb):
    half_w = 4 * LANES
    acc = None
    for half in range(2):
        blocks = []
        for b in range(nb):
            blocks.append(jnp.concatenate(
                [st_ref[l, pl.ds((half * nb + b) * pitch, tc), :] for l in range(8)], axis=1))
        s = jnp.concatenate(blocks, axis=0).astype(BF16)
        part = jnp.dot(s, cblk_ref[half * 2 * half_w:(half + 1) * 2 * half_w, :],
                       preferred_element_type=F32)
        acc = part if acc is None else acc + part
    y = _gelu_tanh(acc + d * v.reshape(nb * tc, S5_TILE))
    return y.reshape(nb, tc, S5_TILE)


def _s5_kernel(va_ref, vc_ref, bblk_ref, cblk_ref, a0_ref, a1_ref, d_ref, y_ref,
               bu0_ref, bu1_ref, st0_ref, st1_ref, x_ref, *, tc, pitch, nb, steps_per_tile):
    s = pl.program_id(0)
    kw = dict(tc=tc, pitch=pitch, nb=nb)

    @pl.when(s == 0)
    def _():
        x_ref[...] = jnp.zeros_like(x_ref)
        bu1_ref[...] = jnp.zeros_like(bu1_ref)
        st0_ref[...] = jnp.zeros_like(st0_ref)

    bus, sts = (bu0_ref, bu1_ref), (st0_ref, st1_ref)
    d = d_ref[...]
    xs = [x_ref[l] for l in range(8)]
    for h, a_ref in enumerate((a0_ref, a1_ref)):
        _s5_input_map(va_ref[:, h * tc:(h + 1) * tc, :], bblk_ref, bus[h], **kw)
        if h == 1:
            xs = [jnp.where(s % steps_per_tile == 0, 0.0, x) for x in xs]
        xs = _s5_scan(bus[1 - h], sts[1 - h], [a_ref[l] for l in range(8)], xs, tc=tc, pitch=pitch)
        y = _s5_output_map(sts[h], cblk_ref, vc_ref[:, h * tc:(h + 1) * tc, :], d, **kw)
        y_ref[:, h * tc:(h + 1) * tc, :] = y.astype(y_ref.dtype)
    for l in range(8):
        x_ref[l] = xs[l]


def _s5_discretise(a_re, a_im, b_re, b_im, c_re, c_im, log_dt, nb):
    g, p = a_re.shape
    gpt = S5_TILE // S5_GROUP
    nt = g // gpt
    dt = jnp.exp(log_dt)[:, None]
    mag = jnp.exp(a_re * dt)
    lb_re, lb_im = mag * jnp.cos(a_im * dt), mag * jnp.sin(a_im * dt)
    den = a_re * a_re + a_im * a_im
    k_re = ((lb_re - 1.0) * a_re + lb_im * a_im) / den
    k_im = (lb_im * a_re - (lb_re - 1.0) * a_im) / den
    bb_re = k_re[..., None] * b_re - k_im[..., None] * b_im
    bb_im = k_re[..., None] * b_im + k_im[..., None] * b_re
    g8 = gpt // 2
    nstate = 2 * gpt * p

    def state_cols(t):
        x = t.shape[-1]
        return t.reshape(nt, 2, g8, p, x).transpose(0, 4, 1, 2, 3).reshape(nt, x, 2, g8 * p)

    col = np.arange(nstate)
    col_group = col // (2 * g8 * p) * g8 + col % (g8 * p) // p
    own = jnp.asarray(np.arange(S5_TILE)[:, None] // S5_GROUP == col_group[None, :], F32)
    m_in = jnp.stack([state_cols(bb_re), state_cols(bb_im)], axis=3).reshape(nt, S5_GROUP, nstate)
    bblk = jnp.tile(m_in, (1, gpt, 1)) * own
    m_out = jnp.stack([state_cols(c_re.swapaxes(1, 2)), state_cols(-c_im.swapaxes(1, 2))],
                      axis=3).reshape(nt, S5_GROUP, nstate).swapaxes(1, 2)
    cblk = jnp.tile(m_out, (1, 1, gpt)) * own.T

    def a_slabs(lb):
        t = lb.reshape(nt, 2, 4, LANES)
        t = jnp.broadcast_to(t[:, :, None], (nt, 2, nb, 4, LANES))
        return t.transpose(0, 3, 1, 2, 4).reshape(nt, 4, 2 * nb, LANES)

    a = jnp.concatenate([a_slabs(lb_re), a_slabs(lb_im)], axis=1)
    return bblk.astype(BF16), cblk.astype(BF16), a


def _s5_core(v, bblk, cblk, a, d_skip, *, tc=128):
    nb, seq, d = v.shape
    assert 2 * nb == SUBLANES
    nt = d // S5_TILE
    spt = seq // (2 * tc)
    steps = nt * spt + 1
    pitch = tc + 4
    nstate = bblk.shape[2]
    slab = pltpu.VMEM((8, SUBLANES * pitch, LANES), F32)

    def pair_in(s):
        return jnp.minimum(s, steps - 2)

    def pair_out(s):
        return jnp.maximum(s - 1, 0)

    def tile_scan0(s):
        return jnp.maximum(2 * s - 1, 0) // (2 * spt)

    chunk = (nb, 2 * tc, S5_TILE)
    a_block = (None, 8, SUBLANES, LANES)
    return pl.pallas_call(
        functools.partial(_s5_kernel, tc=tc, pitch=pitch, nb=nb, steps_per_tile=spt),
        grid=(steps,),
        in_specs=[pl.BlockSpec(chunk, lambda s: (0, pair_in(s) % spt, pair_in(s) // spt)),
                  pl.BlockSpec(chunk, lambda s: (0, pair_out(s) % spt, pair_out(s) // spt)),
                  pl.BlockSpec((None, S5_TILE, nstate), lambda s: (pair_in(s) // spt, 0, 0)),
                  pl.BlockSpec((None, nstate, S5_TILE), lambda s: (pair_out(s) // spt, 0, 0)),
                  pl.BlockSpec(a_block, lambda s: (tile_scan0(s), 0, 0, 0)),
                  pl.BlockSpec(a_block, lambda s: (pair_in(s) // spt, 0, 0, 0)),
                  pl.BlockSpec((1, S5_TILE), lambda s: (0, pair_out(s) // spt))],
        out_specs=pl.BlockSpec(chunk, lambda s: (0, pair_out(s) % spt, pair_out(s) // spt)),
        out_shape=jax.ShapeDtypeStruct((nb, seq, d), BF16),
        scratch_shapes=[slab, slab, slab, slab, pltpu.VMEM((8, SUBLANES, LANES), F32)],
        compiler_params=_params(("arbitrary",)),
        name="s5_core",
    )(v, v, bblk, cblk, a, a, d_skip.reshape(1, d))


def _attn_kernel(sink_ref, bias0_ref, bias_ref, q_ref, kp_ref, kc_ref, vp_ref, vc_ref, o_ref,
                 *, n_kv, blk):
    nq = q_ref.shape[0] // blk
    nk = (nq + 1) * blk
    lane = lax.broadcasted_iota(jnp.int32, (nk, LANES), 1)
    pairs = Q_PER_KV // 2
    nt_dims = (((1,), (1,)), ((), ()))

    def on_slot(x, slot, s):
        x = x if s == slot else pltpu.roll(x, HEAD_DIM, axis=1)
        return jnp.where(lane // HEAD_DIM == s, x, 0.0)

    for kvh in range(n_kv):
        grp, slot = kvh // 2, kvh % 2
        cols = slice(grp * LANES, (grp + 1) * LANES)
        kcat = jnp.concatenate([kp_ref[:, cols], kc_ref[:, cols]], axis=0)
        vcat = jnp.concatenate([vp_ref[:, cols], vc_ref[:, cols]], axis=0)
        base = kvh * pairs
        out_t = [None] * nq
        for s in range(2):
            kexp = on_slot(kcat, slot, s).astype(BF16)
            vexp_t = on_slot(vcat, slot, s).T.astype(BF16)
            for t in range(nq):
                keys = slice(t * blk, (t + 2) * blk)
                bias = bias0_ref if t == 0 else bias_ref
                es, rdens = [], []
                for p in range(pairs):
                    sink = sink_ref[kvh * Q_PER_KV + 2 * p + s]
                    qp = q_ref[t * blk:(t + 1) * blk, (base + p) * LANES:(base + p + 1) * LANES]
                    sc = lax.dot_general(kexp[keys, :], qp, nt_dims,
                                         preferred_element_type=F32) + bias[...]
                    mx = jnp.maximum(jnp.max(sc, axis=0, keepdims=True), sink)
                    e = jnp.exp2(sc - mx)
                    den = jnp.sum(e, axis=0, keepdims=True) + jnp.exp2(sink - mx)
                    es.append(e.astype(BF16))
                    rdens.append(1.0 / den)
                part = jnp.dot(vexp_t[:, keys], jnp.concatenate(es, axis=1),
                               preferred_element_type=F32) * jnp.concatenate(rdens, axis=1)
                out_t[t] = part if out_t[t] is None else out_t[t] + part
        for t in range(nq):
            out = out_t[t].T
            for p in range(pairs):
                o_ref[t * blk:(t + 1) * blk, (base + p) * LANES:(base + p + 1) * LANES] = (
                    out[p * blk:(p + 1) * blk].astype(o_ref.dtype))


def _attention(q, kv, sinks, *, nbatch, blk=WINDOW, nq=2):
    m, dq = q.shape
    dkv = kv.shape[1] // 2
    steps = m // nbatch // (nq * blk)
    kj, qi = np.arange(2 * blk)[:, None], np.arange(blk)[None, :]
    window = (kj > qi) & (kj <= qi + blk)
    bias = jnp.asarray(np.where(np.stack([window & (kj >= blk), window]), 0.0, -1e30), F32)
    bias_block = (None, 2 * blk, blk)

    def this(col):
        return lambda b, n: (b * steps + n, col)

    def prev(col):
        return lambda b, n: (b * steps * nq + jnp.maximum(n * nq - 1, 0), col)

    return pl.pallas_call(
        functools.partial(_attn_kernel, n_kv=dkv // HEAD_DIM, blk=blk),
        grid=(nbatch, steps),
        in_specs=[pl.BlockSpec(memory_space=pltpu.SMEM),
                  pl.BlockSpec(bias_block, lambda b, n: (jnp.minimum(n, 1), 0, 0)),
                  pl.BlockSpec(bias_block, lambda b, n: (1, 0, 0)),
                  pl.BlockSpec((nq * blk, dq), this(0)),
                  pl.BlockSpec((blk, dkv), prev(0)),
                  pl.BlockSpec((nq * blk, dkv), this(0)),
                  pl.BlockSpec((blk, dkv), prev(1)),
                  pl.BlockSpec((nq * blk, dkv), this(1))],
        out_specs=pl.BlockSpec((nq * blk, dq), this(0)),
        out_shape=jax.ShapeDtypeStruct((m, dq), BF16),
        compiler_params=_params(("parallel", "arbitrary")),
        name="attention",
    )(sinks * math.log2(math.e), bias, bias, q, kv, kv, kv, kv)


def kernel(x, c, positions, norm_g, w_ada, b_ada, w_ff_in, w_ff_out, s5_w_in, s5_a_re, s5_a_im, s5_b_re, s5_b_im, s5_c_re, s5_c_im, s5_d, s5_log_dt, s5_w_glu, s5_b_glu, s5_w_out, kv_norm_g, w_ada_kv, b_ada_kv, w_kv, attn_w_q, attn_sinks, attn_w_o, final_norm_g):
    nbatch, seq, d = x.shape
    depth = norm_g.shape[0]
    n_s5 = s5_w_in.shape[0]
    m = nbatch * seq
    kvw = w_kv.shape[1] // 2

    c_pad = jnp.zeros((SUBLANES, d), F32).at[:nbatch].set(c)
    mods = _ada(c_pad, w_ada, b_ada)[:, :nbatch]
    mod_kv = _ada(c_pad, w_ada_kv[None], b_ada_kv[None])[0, :nbatch]
    rope = _rope_tables(positions)

    per_batch = dict(rows_per_batch=seq)
    mod = mods.reshape(depth, nbatch, N_SUBLAYERS, 3, d)
    shift, scale, gate = mod[:, :, :, 0], mod[:, :, :, 1], mod[:, :, :, 2]

    def gs_of(layer, sub):
        return norm_g[layer, sub][None, :] * (1.0 + scale[layer, :, sub])

    def shift_rows(sh):
        return jnp.zeros((SUBLANES, d), F32).at[:nbatch].set(sh).astype(BF16)

    def ffn(mod_in, xs, layer, which, next_gs, final_g=None):
        act, w_out = _mm_swiglu(mod_in, w_ff_in, w_ff_out, (layer, which), **per_batch)
        return _mm_resid(act, w_out, (), xs, gate[layer, :, 2 * which], next_gs,
                         weight=0.5, final_g=final_g, **per_batch)

    xs = x.reshape(m, d)
    xg, ssq = _prep(xs, gs_of(0, 0), **per_batch)
    kv = None
    for layer in range(depth):
        xs, (xg,), ssq = ffn((xg, ssq, shift_rows(shift[layer, :, 0])), xs, layer, 0, [gs_of(layer, 1)])

        mix_in = (xg, ssq, shift_rows(shift[layer, :, 1]))
        if layer < n_s5:
            i = layer
            v = _mm_plain(mix_in, s5_w_in, (i,), out_dtype=F32, **per_batch)
            bblk, cblk, a = _s5_discretise(s5_a_re[i], s5_a_im[i], s5_b_re[i], s5_b_im[i],
                                           s5_c_re[i], s5_c_im[i], s5_log_dt[i], nbatch)
            y = _s5_core(v.reshape(nbatch, seq, d), bblk, cblk, a, s5_d[i].reshape(-1)).reshape(m, d)
            mixed = _mm_glu(y, s5_w_glu, s5_b_glu[i], (i,))
            w_mix_out, lead = s5_w_out, (i,)
        else:
            j = layer - n_s5
            q = _mm_rope(mix_in, attn_w_q, (j,), rope, out_dtype=BF16,
                         out_scale=HEAD_DIM ** -0.5 * math.log2(math.e), **per_batch)
            mixed = _attention(q, kv, attn_sinks[j], nbatch=nbatch)
            w_mix_out, lead = attn_w_o, (j,)
        xs, (xg,), ssq = _mm_resid(mixed, w_mix_out, lead, xs, gate[layer, :, 1], [gs_of(layer, 2)],
                                   weight=1.0, bm=1024, sub=2, **per_batch)

        next_gs = [gs_of(layer + 1, 0)] if layer + 1 < depth else []
        if layer == n_s5 - 1:
            next_gs.append(kv_norm_g[None, :] * (1.0 + mod_kv[:, d:]))
        fuse_final = layer + 1 == depth and not next_gs
        xs, xgs, ssq = ffn((xg, ssq, shift_rows(shift[layer, :, 2])), xs, layer, 1, next_gs,
                           final_norm_g if fuse_final else None)
        if layer == n_s5 - 1:
            kv_in = (xgs[-1], ssq, shift_rows(mod_kv[:, :d]))
            kv = _mm_rope(kv_in, w_kv, (), rope, out_scale=1.0, out_dtype=F32, rope_cols=kvw,
                          **per_batch)
        if layer + 1 < depth:
            xg = xgs[0]

    out = xs if fuse_final else _final_norm(xs, final_norm_g)
    return out.reshape(nbatch, seq, d)
```

```python
import functools
import math

import numpy as np
import jax
import jax.numpy as jnp
from jax import lax
from jax.experimental import pallas as pl
from jax.experimental.pallas import tpu as pltpu

F32 = jnp.float32
BF16 = jnp.bfloat16

RMS_EPS = 1e-6
N_SUBLAYERS = 3
S5_GROUP = 16
S5_STATE = 64
HEAD_DIM = 64
Q_PER_KV = 8
WINDOW = 128
ROPE_THETA = 10000.0

LANES = 128
SUBLANES = 8
S5_TILE = 256
VMEM_LIMIT = 56 * 1024 * 1024


def _params(sem):
    return pltpu.CompilerParams(dimension_semantics=sem, vmem_limit_bytes=VMEM_LIMIT)


def _sigmoid(x):
    return 1.0 / (1.0 + jnp.exp(-x))


def _gelu_tanh(x):
    return 0.5 * x * (1.0 + jnp.tanh(math.sqrt(2.0 / math.pi) * (x + 0.044715 * (x * x * x))))


def _ada_kernel(c_ref, w_ref, b_ref, o_ref):
    c = c_ref[...]
    ca = (c * _sigmoid(c)).astype(BF16)
    o_ref[...] = jnp.dot(ca, w_ref[...].astype(BF16), preferred_element_type=F32) + b_ref[...]


def _ada(c_pad, w, b, bn=2048):
    s, d, n = w.shape
    return pl.pallas_call(
        _ada_kernel,
        grid=(s, n // bn),
        in_specs=[pl.BlockSpec((SUBLANES, d), lambda i, j: (0, 0)),
                  pl.BlockSpec((None, d, bn), lambda i, j: (i, 0, j)),
                  pl.BlockSpec((None, 1, bn), lambda i, j: (i, 0, j))],
        out_specs=pl.BlockSpec((None, SUBLANES, bn), lambda i, j: (i, 0, j)),
        out_shape=jax.ShapeDtypeStruct((s, SUBLANES, n), F32),
        compiler_params=_params(("parallel", "parallel")),
        name="ada",
    )(c_pad, w, b.reshape(s, 1, n))


def _lane_block_sums(sq):
    acc = sq[:, :LANES]
    for j in range(1, sq.shape[1] // LANES):
        acc = acc + sq[:, j * LANES:(j + 1) * LANES]
    return acc


def _row_rsqrt(ssq_ref, rows, width):
    part = ssq_ref[0, rows, :]
    for p in range(1, ssq_ref.shape[0]):
        part = part + ssq_ref[p, rows, :]
    ms = jnp.sum(part, axis=-1, keepdims=True) * (1.0 / width)
    return lax.rsqrt(ms + RMS_EPS)


def _prep_kernel(x_ref, gs_ref, xg_ref, ssq_ref):
    x = x_ref[...]
    xg_ref[...] = (x * gs_ref[...]).astype(xg_ref.dtype)
    ssq_ref[0] = _lane_block_sums(x * x)


def _prep(x, gs, *, rows_per_batch, bl=512):
    m, d = x.shape
    per = rows_per_batch // bl
    nb = gs.shape[0]
    return pl.pallas_call(
        _prep_kernel,
        grid=(m // bl,),
        in_specs=[pl.BlockSpec((bl, d), lambda i: (i, 0)),
                  pl.BlockSpec((None, 1, d), lambda i: (i // per, 0, 0))],
        out_specs=[pl.BlockSpec((bl, d), lambda i: (i, 0)),
                   pl.BlockSpec((1, bl, LANES), lambda i: (0, i, 0))],
        out_shape=[jax.ShapeDtypeStruct((m, d), BF16), jax.ShapeDtypeStruct((1, m, LANES), F32)],
        compiler_params=_params(("parallel",)),
        name="prep",
    )(x, gs.reshape(nb, 1, d))


def _final_norm_kernel(x_ref, g_ref, o_ref):
    x = x_ref[...]
    ms = jnp.mean(x * x, axis=-1, keepdims=True)
    o_ref[...] = x * lax.rsqrt(ms + RMS_EPS) * g_ref[...]


def _final_norm(x, g, bl=512):
    m, d = x.shape
    return pl.pallas_call(
        _final_norm_kernel,
        grid=(m // bl,),
        in_specs=[pl.BlockSpec((bl, d), lambda i: (i, 0)),
                  pl.BlockSpec((1, d), lambda i: (0, 0))],
        out_specs=pl.BlockSpec((bl, d), lambda i: (i, 0)),
        out_shape=jax.ShapeDtypeStruct((m, d), F32),
        compiler_params=_params(("parallel",)),
        name="final_norm",
    )(x, g.reshape(1, d))


def _weight_spec(w, lead, bn, col0=0):
    k = w.shape[-2]
    mode = dict(pipeline_mode=pl.Buffered(1)) if bn == w.shape[-1] else {}
    return pl.BlockSpec((None,) * len(lead) + (k, bn), lambda j, i: tuple(lead) + (0, col0 + j), **mode)


def _weight_scratch(w, bn, count=1):
    return [pltpu.VMEM((w.shape[-2], bn), BF16)] * count


def _stage_weight(w_ref, wb_ref, shift_ref=None, sw_ref=None):
    @pl.when(pl.program_id(1) == 0)
    def _():
        wb_ref[...] = w_ref[...].astype(BF16)
        if sw_ref is not None:
            sw_ref[...] = jnp.dot(shift_ref[...], wb_ref[...], preferred_element_type=F32)


def _modulated_dot(xg_ref, wb_ref, ssq_ref, sw_ref, rows_per_batch, rows=None):
    bm = xg_ref.shape[0]
    rows = slice(0, bm) if rows is None else rows
    b = (pl.program_id(1) * bm + rows.start) // rows_per_batch
    acc = jnp.dot(xg_ref[rows, :], wb_ref[...], preferred_element_type=F32)
    return _row_rsqrt(ssq_ref, rows, xg_ref.shape[1]) * acc + sw_ref[pl.ds(b, 1), :]


def _mm_plain_kernel(x_ref, ssq_ref, shift_ref, w_ref, o_ref, wb_ref, sw_ref, *, per):
    _stage_weight(w_ref, wb_ref, shift_ref, sw_ref)
    o_ref[...] = _modulated_dot(x_ref, wb_ref, ssq_ref, sw_ref, per).astype(o_ref.dtype)


def _mm_swiglu_kernel(x_ref, ssq_ref, shift_ref, wg_ref, wu_ref, wo_ref, o_ref, wob_ref,
                      wgb_ref, wub_ref, swg_ref, swu_ref, *, per, sub):
    _stage_weight(wg_ref, wgb_ref, shift_ref, swg_ref)
    _stage_weight(wu_ref, wub_ref, shift_ref, swu_ref)
    step = x_ref.shape[0] // sub
    for r in range(sub):
        rows = slice(r * step, (r + 1) * step)
        g = _modulated_dot(x_ref, wgb_ref, ssq_ref, swg_ref, per, rows)
        u = _modulated_dot(x_ref, wub_ref, ssq_ref, swu_ref, per, rows)
        o_ref[rows, :] = (g * _sigmoid(g) * u).astype(o_ref.dtype)
    wob_ref[...] = wo_ref[...].astype(wob_ref.dtype)


def _mm_glu_kernel(x_ref, w_ref, y_ref, b_ref, o_ref, wb_ref):
    _stage_weight(w_ref, wb_ref)
    acc = jnp.dot(x_ref[...], wb_ref[...], preferred_element_type=F32)
    o_ref[...] = (y_ref[...].astype(F32) * _sigmoid(acc + b_ref[...])).astype(o_ref.dtype)


def _mm_rope_kernel(x_ref, ssq_ref, shift_ref, w_ref, cos_ref, sin_lo_ref, sin_hi_ref, o_ref,
                    wb_ref, sw_ref, *, per, out_scale, rope_groups):
    _stage_weight(w_ref, wb_ref, shift_ref, sw_ref)
    acc = _modulated_dot(x_ref, wb_ref, ssq_ref, sw_ref, per)
    cos, sin_lo, sin_hi = cos_ref[...], sin_lo_ref[...], sin_hi_ref[...]
    half = HEAD_DIM // 2
    for j in range(acc.shape[1] // LANES):
        a = acc[:, j * LANES:(j + 1) * LANES]
        if j < rope_groups:
            a = (a * cos + pltpu.roll(a, LANES - half, axis=1) * sin_lo
                 + pltpu.roll(a, half, axis=1) * sin_hi)
        o_ref[:, j * LANES:(j + 1) * LANES] = (a * out_scale).astype(o_ref.dtype)


def _mm_resid_kernel(x_ref, w_ref, r_ref, gate_ref, *rest, weight, n_next, staged, final, sub):
    n_extra = 1 if final else n_next
    gs_refs, o_ref, xg_refs = rest[:n_extra], rest[n_extra], rest[n_extra + 1:2 * n_next + 1]
    if staged:
        wb_ref = rest[-1]
        _stage_weight(w_ref, wb_ref)
    else:
        wb_ref = w_ref
    coef = weight * (1.0 + gate_ref[...])
    step = x_ref.shape[0] // sub
    for r in range(sub):
        rows = slice(r * step, (r + 1) * step)
        acc = jnp.dot(x_ref[rows, :], wb_ref[...], preferred_element_type=F32)
        x_new = r_ref[rows, :] + coef * acc
        if final:
            ms = jnp.mean(x_new * x_new, axis=-1, keepdims=True)
            o_ref[rows, :] = x_new * lax.rsqrt(ms + RMS_EPS) * gs_refs[0][...]
            continue
        o_ref[rows, :] = x_new
        for gs_ref, xg_ref in zip(gs_refs, xg_refs):
            xg_ref[rows, :] = (x_new * gs_ref[...]).astype(xg_ref.dtype)
        if n_next:
            rest[2 * n_next + 1][rows, :] = _lane_block_sums(x_new * x_new)


_MM_SEM = ("arbitrary", "arbitrary")


def _modulated_specs(mod_in, bm):
    xg, ssq, _ = mod_in
    k = xg.shape[1]
    return [pl.BlockSpec((bm, k), lambda j, i: (i, 0)),
            pl.BlockSpec((ssq.shape[0], bm, LANES), lambda j, i: (0, i, 0)),
            pl.BlockSpec((SUBLANES, k), lambda j, i: (0, 0))]


def _mm_plain(mod_in, w, lead=(), *, out_dtype, rows_per_batch, ncols=None, col0=0, bm=1024, bn=1024):
    xg, ssq, shift = mod_in
    m, k = xg.shape
    n = w.shape[-1] if ncols is None else ncols
    bn = min(bn, n)
    return pl.pallas_call(
        functools.partial(_mm_plain_kernel, per=rows_per_batch),
        grid=(n // bn, m // bm),
        in_specs=_modulated_specs(mod_in, bm) + [_weight_spec(w, lead, bn, col0 // bn)],
        out_specs=pl.BlockSpec((bm, bn), lambda j, i: (i, j)),
        out_shape=jax.ShapeDtypeStruct((m, n), out_dtype),
        scratch_shapes=_weight_scratch(w, bn) + [pltpu.VMEM((SUBLANES, bn), F32)],
        compiler_params=_params(_MM_SEM),
        name="mm_plain",
    )(xg, ssq, shift, w)


def _mm_swiglu(mod_in, w_in, w_out, lead=(), *, rows_per_batch, bm=2048, bn=512, sub=2):
    xg, ssq, shift = mod_in
    m, k = xg.shape
    f = w_in.shape[-1] // 2
    nf, nm = f // bn, m // bm
    slab = f // (nf * nm)
    n_out = w_out.shape[-1]
    nl = len(lead)
    return pl.pallas_call(
        functools.partial(_mm_swiglu_kernel, per=rows_per_batch, sub=sub),
        grid=(nf, nm),
        in_specs=_modulated_specs(mod_in, bm) + [
            _weight_spec(w_in, lead, bn),
            _weight_spec(w_in, lead, bn, nf),
            pl.BlockSpec((None,) * nl + (slab, n_out), lambda j, i: tuple(lead) + (j * nm + i, 0))],
        out_specs=[pl.BlockSpec((bm, bn), lambda j, i: (i, j)),
                   pl.BlockSpec((slab, n_out), lambda j, i: (j * nm + i, 0))],
        out_shape=[jax.ShapeDtypeStruct((m, f), BF16), jax.ShapeDtypeStruct((f, n_out), BF16)],
        scratch_shapes=_weight_scratch(w_in, bn, 2) + [pltpu.VMEM((SUBLANES, bn), F32)] * 2,
        compiler_params=_params(_MM_SEM),
        name="mm_swiglu",
    )(xg, ssq, shift, w_in, w_in, w_out)


def _mm_glu(y, w, b, lead=(), *, bm=1024, bn=1024):
    m, k = y.shape
    n = w.shape[-1]
    return pl.pallas_call(
        _mm_glu_kernel,
        grid=(n // bn, m // bm),
        in_specs=[pl.BlockSpec((bm, k), lambda j, i: (i, 0)),
                  _weight_spec(w, lead, bn),
                  pl.BlockSpec((bm, bn), lambda j, i: (i, j)),
                  pl.BlockSpec((1, bn), lambda j, i: (0, j))],
        out_specs=pl.BlockSpec((bm, bn), lambda j, i: (i, j)),
        out_shape=jax.ShapeDtypeStruct((m, n), BF16),
        scratch_shapes=_weight_scratch(w, bn),
        compiler_params=_params(_MM_SEM),
        name="mm_glu",
    )(y, w, y, b.reshape(1, n))


def _mm_rope(mod_in, w, lead, rope, *, out_scale, out_dtype, rows_per_batch, rope_cols=None,
             bm=1024, bn=1024):
    xg, ssq, shift = mod_in
    m, k = xg.shape
    n = w.shape[-1]
    bn = min(bn, n)
    assert rope_cols is None or bn == n
    rope_groups = (bn if rope_cols is None else rope_cols) // LANES
    table = pl.BlockSpec((bm, LANES), lambda j, i: (i, 0))
    return pl.pallas_call(
        functools.partial(_mm_rope_kernel, per=rows_per_batch, out_scale=out_scale,
                          rope_groups=rope_groups),
        grid=(n // bn, m // bm),
        in_specs=_modulated_specs(mod_in, bm) + [_weight_spec(w, lead, bn), table, table, table],
        out_specs=pl.BlockSpec((bm, bn), lambda j, i: (i, j)),
        out_shape=jax.ShapeDtypeStruct((m, n), out_dtype),
        scratch_shapes=_weight_scratch(w, bn) + [pltpu.VMEM((SUBLANES, bn), F32)],
        compiler_params=_params(_MM_SEM),
        name="mm_rope",
    )(xg, ssq, shift, w, *rope)


def _mm_resid(x, w, lead, resid, gate, next_gs=(), *, weight, rows_per_batch, final_g=None,
              bm=512, bn=1024, sub=1):
    m, k = x.shape
    n = w.shape[-1]
    final = final_g is not None
    if final:
        bm, bn = bm // 2, n
    per = rows_per_batch // bm
    nb = gate.shape[0]
    n_next = len(next_gs)
    staged = w.dtype != BF16
    tile = pl.BlockSpec((bm, bn), lambda j, i: (i, j))
    per_batch = pl.BlockSpec((None, 1, bn), lambda j, i: (i // per, 0, j))
    out_specs = [tile] * (1 + n_next)
    out_shape = [jax.ShapeDtypeStruct((m, n), F32)] + [jax.ShapeDtypeStruct((m, n), BF16)] * n_next
    if n_next:
        out_specs.append(pl.BlockSpec((None, bm, LANES), lambda j, i: (j, i, 0)))
        out_shape.append(jax.ShapeDtypeStruct((n // bn, m, LANES), F32))
    outs = pl.pallas_call(
        functools.partial(_mm_resid_kernel, weight=weight, n_next=n_next, staged=staged, final=final,
                          sub=sub),
        grid=(n // bn, m // bm),
        in_specs=([pl.BlockSpec((bm, k), lambda j, i: (i, 0)),
                   _weight_spec(w, lead, bn), tile, per_batch] + [per_batch] * n_next
                  + [pl.BlockSpec((1, bn), lambda j, i: (0, j))] * final),
        out_specs=out_specs,
        out_shape=out_shape,
        scratch_shapes=_weight_scratch(w, bn) if staged else [],
        compiler_params=_params(_MM_SEM),
        name="mm_resid",
    )(x, w, resid, gate.reshape(nb, 1, n), *[gs.reshape(nb, 1, n) for gs in next_gs],
      *([final_g.reshape(1, n)] if final else []))
    return outs[0], list(outs[1:1 + n_next]), (outs[-1] if n_next else None)


def _rope_kernel(pos_ref, invf_ref, lo_ref, cos_ref, sin_lo_ref, sin_hi_ref):
    ang = pos_ref[...].astype(F32) * invf_ref[...]
    sin = jnp.sin(ang)
    lo = lo_ref[...]
    cos_ref[...] = jnp.cos(ang)
    sin_lo_ref[...] = -sin * lo
    sin_hi_ref[...] = sin * (1.0 - lo)


def _rope_tables(positions, bl=1024):
    m = positions.size
    half = HEAD_DIM // 2
    inv_freq = 1.0 / (ROPE_THETA ** (jnp.arange(0, HEAD_DIM, 2, dtype=F32) / HEAD_DIM))
    invf = jnp.tile(inv_freq, LANES // half).reshape(1, LANES)
    lo = jnp.asarray(np.arange(LANES) % HEAD_DIM < half, F32).reshape(1, LANES)
    return pl.pallas_call(
        _rope_kernel,
        grid=(m // bl,),
        in_specs=[pl.BlockSpec((bl, 1), lambda i: (i, 0)),
                  pl.BlockSpec((1, LANES), lambda i: (0, 0)),
                  pl.BlockSpec((1, LANES), lambda i: (0, 0))],
        out_specs=[pl.BlockSpec((bl, LANES), lambda i: (i, 0))] * 3,
        out_shape=[jax.ShapeDtypeStruct((m, LANES), F32)] * 3,
        compiler_params=_params(("parallel",)),
        name="rope_tables",
    )(positions.reshape(m, 1), invf, lo)


def _s5_input_map(v, bblk_ref, bu_ref, *, tc, pitch, nb):
    half_w = 4 * LANES
    vb = v.reshape(nb * tc, S5_TILE).astype(BF16)
    for half in range(2):
        r = jnp.dot(vb, bblk_ref[:, half * 2 * half_w:(half + 1) * 2 * half_w],
                    preferred_element_type=F32)
        for b in range(nb):
            for l in range(8):
                bu_ref[l, pl.ds((half * nb + b) * pitch, tc), :] = (
                    r[b * tc:(b + 1) * tc, l * LANES:(l + 1) * LANES])


def _s5_scan(bu_ref, st_ref, a, xs, *, tc, pitch):
    xs = list(xs)
    for t in range(tc):
        for l in range(4):
            bur = bu_ref[l, pl.ds(t, SUBLANES, stride=pitch), :]
            bui = bu_ref[4 + l, pl.ds(t, SUBLANES, stride=pitch), :]
            xr, xi = xs[l], xs[4 + l]
            nr = a[l] * xr - a[4 + l] * xi + bur
            ni = a[l] * xi + a[4 + l] * xr + bui
            st_ref[l, pl.ds(t, SUBLANES, stride=pitch), :] = nr
            st_ref[4 + l, pl.ds(t, SUBLANES, stride=pitch), :] = ni
            xs[l], xs[4 + l] = nr, ni
    return xs


def _s5_output_map(st_ref, cblk_ref, v, d, *, tc, pitch, nb):
    half_w = 4 * LANES
    acc = None
    for half in range(2):
        blocks = []
        for b in range(nb):
            blocks.append(jnp.concatenate(
                [st_ref[l, pl.ds((half * nb + b) * pitch, tc), :] for l in range(8)], axis=1))
        s = jnp.concatenate(blocks, axis=0).astype(BF16)
        part = jnp.dot(s, cblk_ref[half * 2 * half_w:(half + 1) * 2 * half_w, :],
                       preferred_element_type=F32)
        acc = part if acc is None else acc + part
    y = _gelu_tanh(acc + d * v.reshape(nb * tc, S5_TILE))
    return y.reshape(nb, tc, S5_TILE)


def _s5_kernel(va_ref, vc_ref, bblk_ref, cblk_ref, a0_ref, a1_ref, d_ref, y_ref,
               bu0_ref, bu1_ref, st0_ref, st1_ref, x_ref, *, tc, pitch, nb, steps_per_tile):
    s = pl.program_id(0)
    kw = dict(tc=tc, pitch=pitch, nb=nb)

    @pl.when(s == 0)
    def _():
        x_ref[...] = jnp.zeros_like(x_ref)
        bu1_ref[...] = jnp.zeros_like(bu1_ref)
        st0_ref[...] = jnp.zeros_like(st0_ref)

    bus, sts = (bu0_ref, bu1_ref), (st0_ref, st1_ref)
    d = d_ref[...]
    xs = [x_ref[l] for l in range(8)]
    for h, a_ref in enumerate((a0_ref, a1_ref)):
        _s5_input_map(va_ref[:, h * tc:(h + 1) * tc, :], bblk_ref, bus[h], **kw)
        if h == 1:
            xs = [jnp.where(s % steps_per_tile == 0, 0.0, x) for x in xs]
        xs = _s5_scan(bus[1 - h], sts[1 - h], [a_ref[l] for l in range(8)], xs, tc=tc, pitch=pitch)
        y = _s5_output_map(sts[h], cblk_ref, vc_ref[:, h * tc:(h + 1) * tc, :], d, **kw)
        y_ref[:, h * tc:(h + 1) * tc, :] = y.astype(y_ref.dtype)
    for l in range(8):
        x_ref[l] = xs[l]


def _s5_expand_kernel(m_in_ref, m_out_ref, own_ref, b_ref, c_ref):
    own = own_ref[...]
    copies = own.shape[0] // m_in_ref.shape[0]
    b_ref[...] = (jnp.concatenate([m_in_ref[...]] * copies, axis=0) * own).astype(b_ref.dtype)
    c_ref[...] = (jnp.concatenate([m_out_ref[...]] * copies, axis=0) * own).T.astype(c_ref.dtype)


def _s5_discretise(a_re, a_im, b_re, b_im, c_re, c_im, log_dt, nb):
    g, p = a_re.shape
    gpt = S5_TILE // S5_GROUP
    nt = g // gpt
    dt = jnp.exp(log_dt)[:, None]
    mag = jnp.exp(a_re * dt)
    lb_re, lb_im = mag * jnp.cos(a_im * dt), mag * jnp.sin(a_im * dt)
    den = a_re * a_re + a_im * a_im
    k_re = ((lb_re - 1.0) * a_re + lb_im * a_im) / den
    k_im = (lb_im * a_re - (lb_re - 1.0) * a_im) / den
    bb_re = k_re[..., None] * b_re - k_im[..., None] * b_im
    bb_im = k_re[..., None] * b_im + k_im[..., None] * b_re
    g8 = gpt // 2
    nstate = 2 * gpt * p

    def state_cols(t):
        x = t.shape[-1]
        return t.reshape(nt, 2, g8, p, x).transpose(0, 4, 1, 2, 3).reshape(nt, x, 2, g8 * p)

    col = np.arange(nstate)
    col_group = col // (2 * g8 * p) * g8 + col % (g8 * p) // p
    own = jnp.asarray(np.arange(S5_TILE)[:, None] // S5_GROUP == col_group[None, :], F32)
    m_in = jnp.stack([state_cols(bb_re), state_cols(bb_im)], axis=3).reshape(nt, S5_GROUP, nstate)
    m_out = jnp.stack([state_cols(c_re.swapaxes(1, 2)), state_cols(-c_im.swapaxes(1, 2))],
                      axis=3).reshape(nt, S5_GROUP, nstate)
    bblk, cblk = pl.pallas_call(
        _s5_expand_kernel,
        grid=(nt,),
        in_specs=[pl.BlockSpec((None, S5_GROUP, nstate), lambda j: (j, 0, 0)),
                  pl.BlockSpec((None, S5_GROUP, nstate), lambda j: (j, 0, 0)),
                  pl.BlockSpec((S5_TILE, nstate), lambda j: (0, 0))],
        out_specs=[pl.BlockSpec((None, S5_TILE, nstate), lambda j: (j, 0, 0)),
                   pl.BlockSpec((None, nstate, S5_TILE), lambda j: (j, 0, 0))],
        out_shape=[jax.ShapeDtypeStruct((nt, S5_TILE, nstate), BF16),
                   jax.ShapeDtypeStruct((nt, nstate, S5_TILE), BF16)],
        compiler_params=_params(("parallel",)),
        name="s5_expand",
    )(m_in, m_out, own)

    def a_slabs(lb):
        t = lb.reshape(nt, 2, 4, LANES)
        t = jnp.broadcast_to(t[:, :, None], (nt, 2, nb, 4, LANES))
        return t.transpose(0, 3, 1, 2, 4).reshape(nt, 4, 2 * nb, LANES)

    a = jnp.concatenate([a_slabs(lb_re), a_slabs(lb_im)], axis=1)
    return bblk, cblk, a


def _s5_core(v, bblk, cblk, a, d_skip, *, tc=128):
    nb, seq, d = v.shape
    assert 2 * nb == SUBLANES
    nt = d // S5_TILE
    spt = seq // (2 * tc)
    steps = nt * spt + 1
    pitch = tc + 4
    nstate = bblk.shape[2]
    slab = pltpu.VMEM((8, SUBLANES * pitch, LANES), F32)

    def pair_in(s):
        return jnp.minimum(s, steps - 2)

    def pair_out(s):
        return jnp.maximum(s - 1, 0)

    def tile_scan0(s):
        return jnp.maximum(2 * s - 1, 0) // (2 * spt)

    chunk = (nb, 2 * tc, S5_TILE)
    a_block = (None, 8, SUBLANES, LANES)
    return pl.pallas_call(
        functools.partial(_s5_kernel, tc=tc, pitch=pitch, nb=nb, steps_per_tile=spt),
        grid=(steps,),
        in_specs=[pl.BlockSpec(chunk, lambda s: (0, pair_in(s) % spt, pair_in(s) // spt)),
                  pl.BlockSpec(chunk, lambda s: (0, pair_out(s) % spt, pair_out(s) // spt)),
                  pl.BlockSpec((None, S5_TILE, nstate), lambda s: (pair_in(s) // spt, 0, 0)),
                  pl.BlockSpec((None, nstate, S5_TILE), lambda s: (pair_out(s) // spt, 0, 0)),
                  pl.BlockSpec(a_block, lambda s: (tile_scan0(s), 0, 0, 0)),
                  pl.BlockSpec(a_block, lambda s: (pair_in(s) // spt, 0, 0, 0)),
                  pl.BlockSpec((1, S5_TILE), lambda s: (0, pair_out(s) // spt))],
        out_specs=pl.BlockSpec(chunk, lambda s: (0, pair_out(s) % spt, pair_out(s) // spt)),
        out_shape=jax.ShapeDtypeStruct((nb, seq, d), BF16),
        scratch_shapes=[slab, slab, slab, slab, pltpu.VMEM((8, SUBLANES, LANES), F32)],
        compiler_params=_params(("arbitrary",)),
        name="s5_core",
    )(v, v, bblk, cblk, a, a, d_skip.reshape(1, d))


def _attn_kernel(sink_ref, bias0_ref, bias_ref, q_ref, kp_ref, kc_ref, vp_ref, vc_ref, o_ref,
                 *, n_kv, blk):
    nq = q_ref.shape[0] // blk
    nk = (nq + 1) * blk
    lane = lax.broadcasted_iota(jnp.int32, (nk, LANES), 1)
    pairs = Q_PER_KV // 2
    nt_dims = (((1,), (1,)), ((), ()))

    def on_slot(x, slot, s):
        x = x if s == slot else pltpu.roll(x, HEAD_DIM, axis=1)
        return jnp.where(lane // HEAD_DIM == s, x, 0.0)

    for kvh in range(n_kv):
        grp, slot = kvh // 2, kvh % 2
        cols = slice(grp * LANES, (grp + 1) * LANES)
        kcat = jnp.concatenate([kp_ref[:, cols], kc_ref[:, cols]], axis=0)
        vcat = jnp.concatenate([vp_ref[:, cols], vc_ref[:, cols]], axis=0)
        base = kvh * pairs
        out_t = [None] * nq
        for s in range(2):
            kexp = on_slot(kcat, slot, s).astype(BF16)
            vexp_t = on_slot(vcat, slot, s).T.astype(BF16)
            for t in range(nq):
                keys = slice(t * blk, (t + 2) * blk)
                bias = bias0_ref if t == 0 else bias_ref
                es, rdens = [], []
                for p in range(pairs):
                    sink = sink_ref[kvh * Q_PER_KV + 2 * p + s]
                    qp = q_ref[t * blk:(t + 1) * blk, (base + p) * LANES:(base + p + 1) * LANES]
                    sc = lax.dot_general(kexp[keys, :], qp, nt_dims,
                                         preferred_element_type=F32) + bias[...]
                    mx = jnp.maximum(jnp.max(sc, axis=0, keepdims=True), sink)
                    e = jnp.exp2(sc - mx)
                    den = jnp.sum(e, axis=0, keepdims=True) + jnp.exp2(sink - mx)
                    es.append(e.astype(BF16))
                    rdens.append(1.0 / den)
                part = jnp.dot(vexp_t[:, keys], jnp.concatenate(es, axis=1),
                               preferred_element_type=F32) * jnp.concatenate(rdens, axis=1)
                out_t[t] = part if out_t[t] is None else out_t[t] + part
        for t in range(nq):
            out = out_t[t].T
            for p in range(pairs):
                o_ref[t * blk:(t + 1) * blk, (base + p) * LANES:(base + p + 1) * LANES] = (
                    out[p * blk:(p + 1) * blk].astype(o_ref.dtype))


def _attention(q, kv, sinks, *, nbatch, blk=WINDOW, nq=4):
    m, dq = q.shape
    dkv = kv.shape[1] // 2
    steps = m // nbatch // (nq * blk)
    kj, qi = np.arange(2 * blk)[:, None], np.arange(blk)[None, :]
    window = (kj > qi) & (kj <= qi + blk)
    bias = jnp.asarray(np.where(np.stack([window & (kj >= blk), window]), 0.0, -1e30), F32)
    bias_block = (None, 2 * blk, blk)

    def this(col):
        return lambda b, n: (b * steps + n, col)

    def prev(col):
        return lambda b, n: (b * steps * nq + jnp.maximum(n * nq - 1, 0), col)

    return pl.pallas_call(
        functools.partial(_attn_kernel, n_kv=dkv // HEAD_DIM, blk=blk),
        grid=(nbatch, steps),
        in_specs=[pl.BlockSpec(memory_space=pltpu.SMEM),
                  pl.BlockSpec(bias_block, lambda b, n: (jnp.minimum(n, 1), 0, 0)),
                  pl.BlockSpec(bias_block, lambda b, n: (1, 0, 0)),
                  pl.BlockSpec((nq * blk, dq), this(0)),
                  pl.BlockSpec((blk, dkv), prev(0)),
                  pl.BlockSpec((nq * blk, dkv), this(0)),
                  pl.BlockSpec((blk, dkv), prev(1)),
                  pl.BlockSpec((nq * blk, dkv), this(1))],
        out_specs=pl.BlockSpec((nq * blk, dq), this(0)),
        out_shape=jax.ShapeDtypeStruct((m, dq), BF16),
        compiler_params=_params(("parallel", "arbitrary")),
        name="attention",
    )(sinks * math.log2(math.e), bias, bias, q, kv, kv, kv, kv)


def kernel(x, c, positions, norm_g, w_ada, b_ada, w_ff_in, w_ff_out, s5_w_in, s5_a_re, s5_a_im, s5_b_re, s5_b_im, s5_c_re, s5_c_im, s5_d, s5_log_dt, s5_w_glu, s5_b_glu, s5_w_out, kv_norm_g, w_ada_kv, b_ada_kv, w_kv, attn_w_q, attn_sinks, attn_w_o, final_norm_g):
    nbatch, seq, d = x.shape
    depth = norm_g.shape[0]
    n_s5 = s5_w_in.shape[0]
    m = nbatch * seq
    kvw = w_kv.shape[1] // 2

    c_pad = jnp.zeros((SUBLANES, d), F32).at[:nbatch].set(c)
    mods = _ada(c_pad, w_ada, b_ada)[:, :nbatch]
    mod_kv = _ada(c_pad, w_ada_kv[None], b_ada_kv[None])[0, :nbatch]
    rope = _rope_tables(positions)

    per_batch = dict(rows_per_batch=seq)
    mod = mods.reshape(depth, nbatch, N_SUBLAYERS, 3, d)
    shift, scale, gate = mod[:, :, :, 0], mod[:, :, :, 1], mod[:, :, :, 2]

    def gs_of(layer, sub):
        return norm_g[layer, sub][None, :] * (1.0 + scale[layer, :, sub])

    def shift_rows(sh):
        return jnp.zeros((SUBLANES, d), F32).at[:nbatch].set(sh).astype(BF16)

    def ffn(mod_in, xs, layer, which, next_gs, final_g=None):
        act, w_out = _mm_swiglu(mod_in, w_ff_in, w_ff_out, (layer, which), **per_batch)
        return _mm_resid(act, w_out, (), xs, gate[layer, :, 2 * which], next_gs,
                         weight=0.5, final_g=final_g, **per_batch)

    xs = x.reshape(m, d)
    xg, ssq = _prep(xs, gs_of(0, 0), **per_batch)
    kv = None
    for layer in range(depth):
        xs, (xg,), ssq = ffn((xg, ssq, shift_rows(shift[layer, :, 0])), xs, layer, 0, [gs_of(layer, 1)])

        mix_in = (xg, ssq, shift_rows(shift[layer, :, 1]))
        if layer < n_s5:
            i = layer
            v = _mm_plain(mix_in, s5_w_in, (i,), out_dtype=F32, **per_batch)
            bblk, cblk, a = _s5_discretise(s5_a_re[i], s5_a_im[i], s5_b_re[i], s5_b_im[i],
                                           s5_c_re[i], s5_c_im[i], s5_log_dt[i], nbatch)
            y = _s5_core(v.reshape(nbatch, seq, d), bblk, cblk, a, s5_d[i].reshape(-1)).reshape(m, d)
            mixed = _mm_glu(y, s5_w_glu, s5_b_glu[i], (i,))
            w_mix_out, lead = s5_w_out, (i,)
        else:
            j = layer - n_s5
            q = _mm_rope(mix_in, attn_w_q, (j,), rope, out_dtype=BF16,
                         out_scale=HEAD_DIM ** -0.5 * math.log2(math.e), **per_batch)
            mixed = _attention(q, kv, attn_sinks[j], nbatch=nbatch)
            w_mix_out, lead = attn_w_o, (j,)
        xs, (xg,), ssq = _mm_resid(mixed, w_mix_out, lead, xs, gate[layer, :, 1], [gs_of(layer, 2)],
                                   weight=1.0, bm=1024, sub=2, **per_batch)

        next_gs = [gs_of(layer + 1, 0)] if layer + 1 < depth else []
        if layer == n_s5 - 1:
            next_gs.append(kv_norm_g[None, :] * (1.0 + mod_kv[:, d:]))
        fuse_final = layer + 1 == depth and not next_gs
        xs, xgs, ssq = ffn((xg, ssq, shift_rows(shift[layer, :, 2])), xs, layer, 1, next_gs,
                           final_norm_g if fuse_final else None)
        if layer == n_s5 - 1:
            kv_in = (xgs[-1], ssq, shift_rows(mod_kv[:, :d]))
            kv = _mm_rope(kv_in, w_kv, (), rope, out_scale=1.0, out_dtype=F32, rope_cols=kvw,
                          **per_batch)
        if layer + 1 < depth:
            xg = xgs[0]

    out = xs if fuse_final else _final_norm(xs, final_norm_g)
    return out.reshape(nbatch, seq, d)
```

```python
import functools
import math

import numpy as np
import jax
import jax.numpy as jnp
from jax import lax
from jax.experimental import pallas as pl
from jax.experimental.pallas import tpu as pltpu

F32 = jnp.float32
BF16 = jnp.bfloat16

RMS_EPS = 1e-6
N_SUBLAYERS = 3
S5_GROUP = 16
S5_STATE = 64
HEAD_DIM = 64
Q_PER_KV = 8
WINDOW = 128
ROPE_THETA = 10000.0

LANES = 128
SUBLANES = 8
S5_TILE = 256
VMEM_LIMIT = 56 * 1024 * 1024


def _params(sem):
    return pltpu.CompilerParams(dimension_semantics=sem, vmem_limit_bytes=VMEM_LIMIT)


def _sigmoid(x):
    return 1.0 / (1.0 + jnp.exp(-x))


def _gelu_tanh(x):
    return 0.5 * x * (1.0 + jnp.tanh(math.sqrt(2.0 / math.pi) * (x + 0.044715 * (x * x * x))))


def _ada_kernel(c_ref, w_ref, b_ref, o_ref):
    c = c_ref[...]
    ca = (c * _sigmoid(c)).astype(BF16)
    o_ref[...] = jnp.dot(ca, w_ref[...].astype(BF16), preferred_element_type=F32) + b_ref[...]


def _ada(c_pad, w, b, bn=2048):
    s, d, n = w.shape
    return pl.pallas_call(
        _ada_kernel,
        grid=(s, n // bn),
        in_specs=[pl.BlockSpec((SUBLANES, d), lambda i, j: (0, 0)),
                  pl.BlockSpec((None, d, bn), lambda i, j: (i, 0, j)),
                  pl.BlockSpec((None, 1, bn), lambda i, j: (i, 0, j))],
        out_specs=pl.BlockSpec((None, SUBLANES, bn), lambda i, j: (i, 0, j)),
        out_shape=jax.ShapeDtypeStruct((s, SUBLANES, n), F32),
        compiler_params=_params(("parallel", "parallel")),
        name="ada",
    )(c_pad, w, b.reshape(s, 1, n))


def _lane_block_sums(sq):
    acc = sq[:, :LANES]
    for j in range(1, sq.shape[1] // LANES):
        acc = acc + sq[:, j * LANES:(j + 1) * LANES]
    return acc


def _row_rsqrt(ssq_ref, rows, width):
    part = ssq_ref[0, rows, :]
    for p in range(1, ssq_ref.shape[0]):
        part = part + ssq_ref[p, rows, :]
    ms = jnp.sum(part, axis=-1, keepdims=True) * (1.0 / width)
    return lax.rsqrt(ms + RMS_EPS)


def _prep_kernel(x_ref, gs_ref, xg_ref, ssq_ref):
    x = x_ref[...]
    xg_ref[...] = (x * gs_ref[...]).astype(xg_ref.dtype)
    ssq_ref[0] = _lane_block_sums(x * x)


def _prep(x, gs, *, rows_per_batch, bl=1024):
    m, d = x.shape
    per = rows_per_batch // bl
    nb = gs.shape[0]
    return pl.pallas_call(
        _prep_kernel,
        grid=(m // bl,),
        in_specs=[pl.BlockSpec((bl, d), lambda i: (i, 0)),
                  pl.BlockSpec((None, 1, d), lambda i: (i // per, 0, 0))],
        out_specs=[pl.BlockSpec((bl, d), lambda i: (i, 0)),
                   pl.BlockSpec((1, bl, LANES), lambda i: (0, i, 0))],
        out_shape=[jax.ShapeDtypeStruct((m, d), BF16), jax.ShapeDtypeStruct((1, m, LANES), F32)],
        compiler_params=_params(("parallel",)),
        name="prep",
    )(x, gs.reshape(nb, 1, d))


def _final_norm_kernel(x_ref, g_ref, o_ref):
    x = x_ref[...]
    ms = jnp.mean(x * x, axis=-1, keepdims=True)
    o_ref[...] = x * lax.rsqrt(ms + RMS_EPS) * g_ref[...]


def _final_norm(x, g, bl=512):
    m, d = x.shape
    return pl.pallas_call(
        _final_norm_kernel,
        grid=(m // bl,),
        in_specs=[pl.BlockSpec((bl, d), lambda i: (i, 0)),
                  pl.BlockSpec((1, d), lambda i: (0, 0))],
        out_specs=pl.BlockSpec((bl, d), lambda i: (i, 0)),
        out_shape=jax.ShapeDtypeStruct((m, d), F32),
        compiler_params=_params(("parallel",)),
        name="final_norm",
    )(x, g.reshape(1, d))


def _weight_spec(w, lead, bn, col0=0):
    k = w.shape[-2]
    mode = dict(pipeline_mode=pl.Buffered(1)) if bn == w.shape[-1] else {}
    return pl.BlockSpec((None,) * len(lead) + (k, bn), lambda j, i: tuple(lead) + (0, col0 + j), **mode)


def _weight_scratch(w, bn, count=1):
    return [pltpu.VMEM((w.shape[-2], bn), BF16)] * count


def _stage_weight(w_ref, wb_ref, shift_ref=None, sw_ref=None):
    @pl.when(pl.program_id(1) == 0)
    def _():
        wb_ref[...] = w_ref[...].astype(BF16)
        if sw_ref is not None:
            sw_ref[...] = jnp.dot(shift_ref[...], wb_ref[...], preferred_element_type=F32)


def _modulated_dot(xg_ref, wb_ref, ssq_ref, sw_ref, rows_per_batch, rows=None):
    bm = xg_ref.shape[0]
    rows = slice(0, bm) if rows is None else rows
    b = (pl.program_id(1) * bm + rows.start) // rows_per_batch
    acc = jnp.dot(xg_ref[rows, :], wb_ref[...], preferred_element_type=F32)
    return _row_rsqrt(ssq_ref, rows, xg_ref.shape[1]) * acc + sw_ref[pl.ds(b, 1), :]


def _mm_plain_kernel(x_ref, ssq_ref, shift_ref, w_ref, o_ref, wb_ref, sw_ref, *, per):
    _stage_weight(w_ref, wb_ref, shift_ref, sw_ref)
    o_ref[...] = _modulated_dot(x_ref, wb_ref, ssq_ref, sw_ref, per).astype(o_ref.dtype)


def _mm_swiglu_kernel(x_ref, ssq_ref, shift_ref, wg_ref, wu_ref, wo_ref, o_ref, wob_ref,
                      wgb_ref, wub_ref, swg_ref, swu_ref, *, per, sub):
    _stage_weight(wg_ref, wgb_ref, shift_ref, swg_ref)
    _stage_weight(wu_ref, wub_ref, shift_ref, swu_ref)
    step = x_ref.shape[0] // sub
    for r in range(sub):
        rows = slice(r * step, (r + 1) * step)
        g = _modulated_dot(x_ref, wgb_ref, ssq_ref, swg_ref, per, rows)
        u = _modulated_dot(x_ref, wub_ref, ssq_ref, swu_ref, per, rows)
        o_ref[rows, :] = (g * _sigmoid(g) * u).astype(o_ref.dtype)
    wob_ref[...] = wo_ref[...].astype(wob_ref.dtype)


def _mm_glu_kernel(x_ref, w_ref, y_ref, b_ref, o_ref, wb_ref):
    _stage_weight(w_ref, wb_ref)
    acc = jnp.dot(x_ref[...], wb_ref[...], preferred_element_type=F32)
    o_ref[...] = (y_ref[...].astype(F32) * _sigmoid(acc + b_ref[...])).astype(o_ref.dtype)


def _mm_rope_kernel(x_ref, ssq_ref, shift_ref, w_ref, cos_ref, sin_lo_ref, sin_hi_ref, o_ref,
                    wb_ref, sw_ref, *, per, out_scale, rope_groups):
    _stage_weight(w_ref, wb_ref, shift_ref, sw_ref)
    acc = _modulated_dot(x_ref, wb_ref, ssq_ref, sw_ref, per)
    cos, sin_lo, sin_hi = cos_ref[...], sin_lo_ref[...], sin_hi_ref[...]
    half = HEAD_DIM // 2
    for j in range(acc.shape[1] // LANES):
        a = acc[:, j * LANES:(j + 1) * LANES]
        if j < rope_groups:
            a = (a * cos + pltpu.roll(a, LANES - half, axis=1) * sin_lo
                 + pltpu.roll(a, half, axis=1) * sin_hi)
        o_ref[:, j * LANES:(j + 1) * LANES] = (a * out_scale).astype(o_ref.dtype)


def _mm_resid_kernel(x_ref, w_ref, r_ref, gate_ref, *rest, weight, n_next, staged, final, sub):
    n_extra = 1 if final else n_next
    gs_refs, o_ref, xg_refs = rest[:n_extra], rest[n_extra], rest[n_extra + 1:2 * n_next + 1]
    if staged:
        wb_ref = rest[-1]
        _stage_weight(w_ref, wb_ref)
    else:
        wb_ref = w_ref
    coef = weight * (1.0 + gate_ref[...])
    step = x_ref.shape[0] // sub
    for r in range(sub):
        rows = slice(r * step, (r + 1) * step)
        acc = jnp.dot(x_ref[rows, :], wb_ref[...], preferred_element_type=F32)
        x_new = r_ref[rows, :] + coef * acc
        if final:
            ms = jnp.mean(x_new * x_new, axis=-1, keepdims=True)
            o_ref[rows, :] = x_new * lax.rsqrt(ms + RMS_EPS) * gs_refs[0][...]
            continue
        o_ref[rows, :] = x_new
        for gs_ref, xg_ref in zip(gs_refs, xg_refs):
            xg_ref[rows, :] = (x_new * gs_ref[...]).astype(xg_ref.dtype)
        if n_next:
            rest[2 * n_next + 1][rows, :] = _lane_block_sums(x_new * x_new)


_MM_SEM = ("arbitrary", "arbitrary")


def _modulated_specs(mod_in, bm):
    xg, ssq, _ = mod_in
    k = xg.shape[1]
    return [pl.BlockSpec((bm, k), lambda j, i: (i, 0)),
            pl.BlockSpec((ssq.shape[0], bm, LANES), lambda j, i: (0, i, 0)),
            pl.BlockSpec((SUBLANES, k), lambda j, i: (0, 0))]


def _mm_plain(mod_in, w, lead=(), *, out_dtype, rows_per_batch, bm=1024, bn=1024):
    xg, ssq, shift = mod_in
    m, k = xg.shape
    n = w.shape[-1]
    bn = min(bn, n)
    return pl.pallas_call(
        functools.partial(_mm_plain_kernel, per=rows_per_batch),
        grid=(n // bn, m // bm),
        in_specs=_modulated_specs(mod_in, bm) + [_weight_spec(w, lead, bn)],
        out_specs=pl.BlockSpec((bm, bn), lambda j, i: (i, j)),
        out_shape=jax.ShapeDtypeStruct((m, n), out_dtype),
        scratch_shapes=_weight_scratch(w, bn) + [pltpu.VMEM((SUBLANES, bn), F32)],
        compiler_params=_params(_MM_SEM),
        name="mm_plain",
    )(xg, ssq, shift, w)


def _mm_swiglu(mod_in, w_in, w_out, lead=(), *, rows_per_batch, bm=2048, bn=512, sub=2):
    xg, ssq, shift = mod_in
    m, k = xg.shape
    f = w_in.shape[-1] // 2
    nf, nm = f // bn, m // bm
    slab = f // (nf * nm)
    n_out = w_out.shape[-1]
    nl = len(lead)
    return pl.pallas_call(
        functools.partial(_mm_swiglu_kernel, per=rows_per_batch, sub=sub),
        grid=(nf, nm),
        in_specs=_modulated_specs(mod_in, bm) + [
            _weight_spec(w_in, lead, bn),
            _weight_spec(w_in, lead, bn, nf),
            pl.BlockSpec((None,) * nl + (slab, n_out), lambda j, i: tuple(lead) + (j * nm + i, 0))],
        out_specs=[pl.BlockSpec((bm, bn), lambda j, i: (i, j)),
                   pl.BlockSpec((slab, n_out), lambda j, i: (j * nm + i, 0))],
        out_shape=[jax.ShapeDtypeStruct((m, f), BF16), jax.ShapeDtypeStruct((f, n_out), BF16)],
        scratch_shapes=_weight_scratch(w_in, bn, 2) + [pltpu.VMEM((SUBLANES, bn), F32)] * 2,
        compiler_params=_params(_MM_SEM),
        name="mm_swiglu",
    )(xg, ssq, shift, w_in, w_in, w_out)


def _mm_glu(y, w, b, lead=(), *, bm=1024, bn=1024):
    m, k = y.shape
    n = w.shape[-1]
    return pl.pallas_call(
        _mm_glu_kernel,
        grid=(n // bn, m // bm),
        in_specs=[pl.BlockSpec((bm, k), lambda j, i: (i, 0)),
                  _weight_spec(w, lead, bn),
                  pl.BlockSpec((bm, bn), lambda j, i: (i, j)),
                  pl.BlockSpec((1, bn), lambda j, i: (0, j))],
        out_specs=pl.BlockSpec((bm, bn), lambda j, i: (i, j)),
        out_shape=jax.ShapeDtypeStruct((m, n), BF16),
        scratch_shapes=_weight_scratch(w, bn),
        compiler_params=_params(_MM_SEM),
        name="mm_glu",
    )(y, w, y, b.reshape(1, n))


def _mm_rope(mod_in, w, lead, rope, *, out_scale, out_dtype, rows_per_batch, rope_cols=None,
             bm=1024, bn=1024):
    xg, ssq, shift = mod_in
    m, k = xg.shape
    n = w.shape[-1]
    bn = min(bn, n)
    assert rope_cols is None or bn == n
    rope_groups = (bn if rope_cols is None else rope_cols) // LANES
    table = pl.BlockSpec((bm, LANES), lambda j, i: (i, 0))
    return pl.pallas_call(
        functools.partial(_mm_rope_kernel, per=rows_per_batch, out_scale=out_scale,
                          rope_groups=rope_groups),
        grid=(n // bn, m // bm),
        in_specs=_modulated_specs(mod_in, bm) + [_weight_spec(w, lead, bn), table, table, table],
        out_specs=pl.BlockSpec((bm, bn), lambda j, i: (i, j)),
        out_shape=jax.ShapeDtypeStruct((m, n), out_dtype),
        scratch_shapes=_weight_scratch(w, bn) + [pltpu.VMEM((SUBLANES, bn), F32)],
        compiler_params=_params(_MM_SEM),
        name="mm_rope",
    )(xg, ssq, shift, w, *rope)


def _mm_resid(x, w, lead, resid, gate, next_gs=(), *, weight, rows_per_batch, final_g=None,
              bm=512, bn=1024, sub=1):
    m, k = x.shape
    n = w.shape[-1]
    final = final_g is not None
    if final:
        bm, bn = bm // 2, n
    per = rows_per_batch // bm
    nb = gate.shape[0]
    n_next = len(next_gs)
    staged = w.dtype != BF16
    tile = pl.BlockSpec((bm, bn), lambda j, i: (i, j))
    per_batch = pl.BlockSpec((None, 1, bn), lambda j, i: (i // per, 0, j))
    out_specs = [tile] * (1 + n_next)
    out_shape = [jax.ShapeDtypeStruct((m, n), F32)] + [jax.ShapeDtypeStruct((m, n), BF16)] * n_next
    if n_next:
        out_specs.append(pl.BlockSpec((None, bm, LANES), lambda j, i: (j, i, 0)))
        out_shape.append(jax.ShapeDtypeStruct((n // bn, m, LANES), F32))
    outs = pl.pallas_call(
        functools.partial(_mm_resid_kernel, weight=weight, n_next=n_next, staged=staged, final=final,
                          sub=sub),
        grid=(n // bn, m // bm),
        in_specs=([pl.BlockSpec((bm, k), lambda j, i: (i, 0)),
                   _weight_spec(w, lead, bn), tile, per_batch] + [per_batch] * n_next
                  + [pl.BlockSpec((1, bn), lambda j, i: (0, j))] * final),
        out_specs=out_specs,
        out_shape=out_shape,
        scratch_shapes=_weight_scratch(w, bn) if staged else [],
        compiler_params=_params(_MM_SEM),
        name="mm_resid",
    )(x, w, resid, gate.reshape(nb, 1, n), *[gs.reshape(nb, 1, n) for gs in next_gs],
      *([final_g.reshape(1, n)] if final else []))
    return outs[0], list(outs[1:1 + n_next]), (outs[-1] if n_next else None)


def _rope_kernel(pos_ref, invf_ref, lo_ref, cos_ref, sin_lo_ref, sin_hi_ref):
    ang = pos_ref[...].astype(F32) * invf_ref[...]
    sin = jnp.sin(ang)
    lo = lo_ref[...]
    cos_ref[...] = jnp.cos(ang)
    sin_lo_ref[...] = -sin * lo
    sin_hi_ref[...] = sin * (1.0 - lo)


def _rope_tables(positions, bl=2048):
    m = positions.size
    half = HEAD_DIM // 2
    inv_freq = 1.0 / (ROPE_THETA ** (jnp.arange(0, HEAD_DIM, 2, dtype=F32) / HEAD_DIM))
    invf = jnp.tile(inv_freq, LANES // half).reshape(1, LANES)
    lo = jnp.asarray(np.arange(LANES) % HEAD_DIM < half, F32).reshape(1, LANES)
    return pl.pallas_call(
        _rope_kernel,
        grid=(m // bl,),
        in_specs=[pl.BlockSpec((bl, 1), lambda i: (i, 0)),
                  pl.BlockSpec((1, LANES), lambda i: (0, 0)),
                  pl.BlockSpec((1, LANES), lambda i: (0, 0))],
        out_specs=[pl.BlockSpec((bl, LANES), lambda i: (i, 0))] * 3,
        out_shape=[jax.ShapeDtypeStruct((m, LANES), F32)] * 3,
        compiler_params=_params(("parallel",)),
        name="rope_tables",
    )(positions.reshape(m, 1), invf, lo)


def _s5_input_map(v, bblk_ref, bu_ref, *, tc, pitch, nb):
    half_w = 4 * LANES
    vb = v.reshape(nb * tc, S5_TILE).astype(BF16)
    for half in range(2):
        r = jnp.dot(vb, bblk_ref[:, half * 2 * half_w:(half + 1) * 2 * half_w],
                    preferred_element_type=F32)
        for b in range(nb):
            for l in range(8):
                bu_ref[l, pl.ds((half * nb + b) * pitch, tc), :] = (
                    r[b * tc:(b + 1) * tc, l * LANES:(l + 1) * LANES])


def _s5_scan(bu_ref, st_ref, a, xs, *, tc, pitch):
    xs = list(xs)
    for t in range(tc):
        for l in range(4):
            bur = bu_ref[l, pl.ds(t, SUBLANES, stride=pitch), :]
            bui = bu_ref[4 + l, pl.ds(t, SUBLANES, stride=pitch), :]
            xr, xi = xs[l], xs[4 + l]
            nr = a[l] * xr - a[4 + l] * xi + bur
            ni = a[l] * xi + a[4 + l] * xr + bui
            st_ref[l, pl.ds(t, SUBLANES, stride=pitch), :] = nr
            st_ref[4 + l, pl.ds(t, SUBLANES, stride=pitch), :] = ni
            xs[l], xs[4 + l] = nr, ni
    return xs


def _s5_output_map(st_ref, cblk_ref, v, d, *, tc, pitch, nb):
    half_w = 4 * LANES
    acc = None
    for half in range(2):
        blocks = []
        for b in range(nb):
            blocks.append(jnp.concatenate(
                [st_ref[l, pl.ds((half * nb + b) * pitch, tc), :] for l in range(8)], axis=1))
        s = jnp.concatenate(blocks, axis=0).astype(BF16)
        part = jnp.dot(s, cblk_ref[half * 2 * half_w:(half + 1) * 2 * half_w, :],
                       preferred_element_type=F32)
        acc = part if acc is None else acc + part
    y = _gelu_tanh(acc + d * v.reshape(nb * tc, S5_TILE))
    return y.reshape(nb, tc, S5_TILE)


def _s5_kernel(va_ref, vc_ref, bblk_ref, cblk_ref, a0_ref, a1_ref, d_ref, y_ref,
               bu0_ref, bu1_ref, st0_ref, st1_ref, x_ref, *, tc, pitch, nb, steps_per_tile):
    s = pl.program_id(0)
    kw = dict(tc=tc, pitch=pitch, nb=nb)

    @pl.when(s == 0)
    def _():
        x_ref[...] = jnp.zeros_like(x_ref)
        bu1_ref[...] = jnp.zeros_like(bu1_ref)
        st0_ref[...] = jnp.zeros_like(st0_ref)

    bus, sts = (bu0_ref, bu1_ref), (st0_ref, st1_ref)
    d = d_ref[...]
    xs = [x_ref[l] for l in range(8)]
    for h, a_ref in enumerate((a0_ref, a1_ref)):
        _s5_input_map(va_ref[:, h * tc:(h + 1) * tc, :], bblk_ref, bus[h], **kw)
        if h == 1:
            xs = [jnp.where(s % steps_per_tile == 0, 0.0, x) for x in xs]
        xs = _s5_scan(bus[1 - h], sts[1 - h], [a_ref[l] for l in range(8)], xs, tc=tc, pitch=pitch)
        y = _s5_output_map(sts[h], cblk_ref, vc_ref[:, h * tc:(h + 1) * tc, :], d, **kw)
        y_ref[:, h * tc:(h + 1) * tc, :] = y.astype(y_ref.dtype)
    for l in range(8):
        x_ref[l] = xs[l]


def _s5_expand_kernel(m_in_ref, m_out_ref, own_ref, b_ref, c_ref):
    own = own_ref[...]
    copies = own.shape[0] // m_in_ref.shape[0]
    b_ref[...] = (jnp.concatenate([m_in_ref[...]] * copies, axis=0) * own).astype(b_ref.dtype)
    c_ref[...] = (jnp.concatenate([m_out_ref[...]] * copies, axis=0) * own).T.astype(c_ref.dtype)


def _s5_discretise(a_re, a_im, b_re, b_im, c_re, c_im, log_dt, nb):
    g, p = a_re.shape
    gpt = S5_TILE // S5_GROUP
    nt = g // gpt
    dt = jnp.exp(log_dt)[:, None]
    mag = jnp.exp(a_re * dt)
    lb_re, lb_im = mag * jnp.cos(a_im * dt), mag * jnp.sin(a_im * dt)
    den = a_re * a_re + a_im * a_im
    k_re = ((lb_re - 1.0) * a_re + lb_im * a_im) / den
    k_im = (lb_im * a_re - (lb_re - 1.0) * a_im) / den
    bb_re = k_re[..., None] * b_re - k_im[..., None] * b_im
    bb_im = k_re[..., None] * b_im + k_im[..., None] * b_re
    g8 = gpt // 2
    nstate = 2 * gpt * p

    def state_cols(t):
        x = t.shape[-1]
        return t.reshape(nt, 2, g8, p, x).transpose(0, 4, 1, 2, 3).reshape(nt, x, 2, g8 * p)

    col = np.arange(nstate)
    col_group = col // (2 * g8 * p) * g8 + col % (g8 * p) // p
    own = jnp.asarray(np.arange(S5_TILE)[:, None] // S5_GROUP == col_group[None, :], F32)
    m_in = jnp.stack([state_cols(bb_re), state_cols(bb_im)], axis=3).reshape(nt, S5_GROUP, nstate)
    m_out = jnp.stack([state_cols(c_re.swapaxes(1, 2)), state_cols(-c_im.swapaxes(1, 2))],
                      axis=3).reshape(nt, S5_GROUP, nstate)
    bblk, cblk = pl.pallas_call(
        _s5_expand_kernel,
        grid=(nt,),
        in_specs=[pl.BlockSpec((None, S5_GROUP, nstate), lambda j: (j, 0, 0)),
                  pl.BlockSpec((None, S5_GROUP, nstate), lambda j: (j, 0, 0)),
                  pl.BlockSpec((S5_TILE, nstate), lambda j: (0, 0))],
        out_specs=[pl.BlockSpec((None, S5_TILE, nstate), lambda j: (j, 0, 0)),
                   pl.BlockSpec((None, nstate, S5_TILE), lambda j: (j, 0, 0))],
        out_shape=[jax.ShapeDtypeStruct((nt, S5_TILE, nstate), BF16),
                   jax.ShapeDtypeStruct((nt, nstate, S5_TILE), BF16)],
        compiler_params=_params(("parallel",)),
        name="s5_expand",
    )(m_in, m_out, own)

    def a_slabs(lb):
        t = lb.reshape(nt, 2, 4, LANES)
        t = jnp.broadcast_to(t[:, :, None], (nt, 2, nb, 4, LANES))
        return t.transpose(0, 3, 1, 2, 4).reshape(nt, 4, 2 * nb, LANES)

    a = jnp.concatenate([a_slabs(lb_re), a_slabs(lb_im)], axis=1)
    return bblk, cblk, a


def _s5_core(v, bblk, cblk, a, d_skip, *, tc=128):
    nb, seq, d = v.shape
    assert 2 * nb == SUBLANES
    nt = d // S5_TILE
    spt = seq // (2 * tc)
    steps = nt * spt + 1
    pitch = tc + 4
    nstate = bblk.shape[2]
    slab = pltpu.VMEM((8, SUBLANES * pitch, LANES), F32)

    def pair_in(s):
        return jnp.minimum(s, steps - 2)

    def pair_out(s):
        return jnp.maximum(s - 1, 0)

    def tile_scan0(s):
        return jnp.maximum(2 * s - 1, 0) // (2 * spt)

    chunk = (nb, 2 * tc, S5_TILE)
    a_block = (None, 8, SUBLANES, LANES)
    return pl.pallas_call(
        functools.partial(_s5_kernel, tc=tc, pitch=pitch, nb=nb, steps_per_tile=spt),
        grid=(steps,),
        in_specs=[pl.BlockSpec(chunk, lambda s: (0, pair_in(s) % spt, pair_in(s) // spt)),
                  pl.BlockSpec(chunk, lambda s: (0, pair_out(s) % spt, pair_out(s) // spt)),
                  pl.BlockSpec((None, S5_TILE, nstate), lambda s: (pair_in(s) // spt, 0, 0)),
                  pl.BlockSpec((None, nstate, S5_TILE), lambda s: (pair_out(s) // spt, 0, 0)),
                  pl.BlockSpec(a_block, lambda s: (tile_scan0(s), 0, 0, 0)),
                  pl.BlockSpec(a_block, lambda s: (pair_in(s) // spt, 0, 0, 0)),
                  pl.BlockSpec((1, S5_TILE), lambda s: (0, pair_out(s) // spt))],
        out_specs=pl.BlockSpec(chunk, lambda s: (0, pair_out(s) % spt, pair_out(s) // spt)),
        out_shape=jax.ShapeDtypeStruct((nb, seq, d), BF16),
        scratch_shapes=[slab, slab, slab, slab, pltpu.VMEM((8, SUBLANES, LANES), F32)],
        compiler_params=_params(("arbitrary",)),
        name="s5_core",
    )(v, v, bblk, cblk, a, a, d_skip.reshape(1, d))


def _attn_kernel(sink_ref, bias0_ref, bias_ref, q_ref, kp_ref, kc_ref, vp_ref, vc_ref, o_ref,
                 *, n_kv, blk):
    nq = q_ref.shape[0] // blk
    nk = (nq + 1) * blk
    lane = lax.broadcasted_iota(jnp.int32, (nk, LANES), 1)
    pairs = Q_PER_KV // 2
    nt_dims = (((1,), (1,)), ((), ()))

    def on_slot(x, slot, s):
        x = x if s == slot else pltpu.roll(x, HEAD_DIM, axis=1)
        return jnp.where(lane // HEAD_DIM == s, x, 0.0)

    for kvh in range(n_kv):
        grp, slot = kvh // 2, kvh % 2
        cols = slice(grp * LANES, (grp + 1) * LANES)
        kcat = jnp.concatenate([kp_ref[:, cols], kc_ref[:, cols]], axis=0)
        vcat = jnp.concatenate([vp_ref[:, cols], vc_ref[:, cols]], axis=0)
        base = kvh * pairs
        out_t = [None] * nq
        for s in range(2):
            kexp = on_slot(kcat, slot, s).astype(BF16)
            vexp_t = on_slot(vcat, slot, s).T.astype(BF16)
            for t in range(nq):
                keys = slice(t * blk, (t + 2) * blk)
                bias = bias0_ref if t == 0 else bias_ref
                es, rdens = [], []
                for p in range(pairs):
                    sink = sink_ref[kvh * Q_PER_KV + 2 * p + s]
                    qp = q_ref[t * blk:(t + 1) * blk, (base + p) * LANES:(base + p + 1) * LANES]
                    sc = lax.dot_general(kexp[keys, :], qp, nt_dims,
                                         preferred_element_type=F32) + bias[...]
                    mx = jnp.maximum(jnp.max(sc, axis=0, keepdims=True), sink)
                    e = jnp.exp2(sc - mx)
                    den = jnp.sum(e, axis=0, keepdims=True) + jnp.exp2(sink - mx)
                    es.append(e.astype(BF16))
                    rdens.append(1.0 / den)
                part = jnp.dot(vexp_t[:, keys], jnp.concatenate(es, axis=1),
                               preferred_element_type=F32) * jnp.concatenate(rdens, axis=1)
                out_t[t] = part if out_t[t] is None else out_t[t] + part
        for t in range(nq):
            out = out_t[t].T
            for p in range(pairs):
                o_ref[t * blk:(t + 1) * blk, (base + p) * LANES:(base + p + 1) * LANES] = (
                    out[p * blk:(p + 1) * blk].astype(o_ref.dtype))


def _attention(q, kv, sinks, *, nbatch, blk=WINDOW, nq=4):
    m, dq = q.shape
    dkv = kv.shape[1] // 2
    steps = m // nbatch // (nq * blk)
    kj, qi = np.arange(2 * blk)[:, None], np.arange(blk)[None, :]
    window = (kj > qi) & (kj <= qi + blk)
    bias = jnp.asarray(np.where(np.stack([window & (kj >= blk), window]), 0.0, -1e30), F32)
    bias_block = (None, 2 * blk, blk)

    def this(col):
        return lambda b, n: (b * steps + n, col)

    def prev(col):
        return lambda b, n: (b * steps * nq + jnp.maximum(n * nq - 1, 0), col)

    return pl.pallas_call(
        functools.partial(_attn_kernel, n_kv=dkv // HEAD_DIM, blk=blk),
        grid=(nbatch, steps),
        in_specs=[pl.BlockSpec(memory_space=pltpu.SMEM),
                  pl.BlockSpec(bias_block, lambda b, n: (jnp.minimum(n, 1), 0, 0)),
                  pl.BlockSpec(bias_block, lambda b, n: (1, 0, 0)),
                  pl.BlockSpec((nq * blk, dq), this(0)),
                  pl.BlockSpec((blk, dkv), prev(0)),
                  pl.BlockSpec((nq * blk, dkv), this(0)),
                  pl.BlockSpec((blk, dkv), prev(1)),
                  pl.BlockSpec((nq * blk, dkv), this(1))],
        out_specs=pl.BlockSpec((nq * blk, dq), this(0)),
        out_shape=jax.ShapeDtypeStruct((m, dq), BF16),
        compiler_params=_params(("parallel", "arbitrary")),
        name="attention",
    )(sinks * math.log2(math.e), bias, bias, q, kv, kv, kv, kv)


def kernel(x, c, positions, norm_g, w_ada, b_ada, w_ff_in, w_ff_out, s5_w_in, s5_a_re, s5_a_im, s5_b_re, s5_b_im, s5_c_re, s5_c_im, s5_d, s5_log_dt, s5_w_glu, s5_b_glu, s5_w_out, kv_norm_g, w_ada_kv, b_ada_kv, w_kv, attn_w_q, attn_sinks, attn_w_o, final_norm_g):
    nbatch, seq, d = x.shape
    depth = norm_g.shape[0]
    n_s5 = s5_w_in.shape[0]
    m = nbatch * seq
    kvw = w_kv.shape[1] // 2

    c_pad = jnp.zeros((SUBLANES, d), F32).at[:nbatch].set(c)
    mods = _ada(c_pad, w_ada, b_ada)[:, :nbatch]
    mod_kv = _ada(c_pad, w_ada_kv[None], b_ada_kv[None])[0, :nbatch]
    rope = _rope_tables(positions)

    per_batch = dict(rows_per_batch=seq)
    mod = mods.reshape(depth, nbatch, N_SUBLAYERS, 3, d)
    shift, scale, gate = mod[:, :, :, 0], mod[:, :, :, 1], mod[:, :, :, 2]

    def gs_of(layer, sub):
        return norm_g[layer, sub][None, :] * (1.0 + scale[layer, :, sub])

    def shift_rows(sh):
        return jnp.zeros((SUBLANES, d), F32).at[:nbatch].set(sh).astype(BF16)

    def ffn(mod_in, xs, layer, which, next_gs, final_g=None):
        act, w_out = _mm_swiglu(mod_in, w_ff_in, w_ff_out, (layer, which), **per_batch)
        return _mm_resid(act, w_out, (), xs, gate[layer, :, 2 * which], next_gs,
                         weight=0.5, final_g=final_g, **per_batch)

    xs = x.reshape(m, d)
    xg, ssq = _prep(xs, gs_of(0, 0), **per_batch)
    kv = None
    for layer in range(depth):
        xs, (xg,), ssq = ffn((xg, ssq, shift_rows(shift[layer, :, 0])), xs, layer, 0, [gs_of(layer, 1)])

        mix_in = (xg, ssq, shift_rows(shift[layer, :, 1]))
        if layer < n_s5:
            i = layer
            v = _mm_plain(mix_in, s5_w_in, (i,), out_dtype=F32, **per_batch)
            bblk, cblk, a = _s5_discretise(s5_a_re[i], s5_a_im[i], s5_b_re[i], s5_b_im[i],
                                           s5_c_re[i], s5_c_im[i], s5_log_dt[i], nbatch)
            y = _s5_core(v.reshape(nbatch, seq, d), bblk, cblk, a, s5_d[i].reshape(-1)).reshape(m, d)
            mixed = _mm_glu(y, s5_w_glu, s5_b_glu[i], (i,))
            w_mix_out, lead = s5_w_out, (i,)
        else:
            j = layer - n_s5
            q = _mm_rope(mix_in, attn_w_q, (j,), rope, out_dtype=BF16,
                         out_scale=HEAD_DIM ** -0.5 * math.log2(math.e), **per_batch)
            mixed = _attention(q, kv, attn_sinks[j], nbatch=nbatch)
            w_mix_out, lead = attn_w_o, (j,)
        xs, (xg,), ssq = _mm_resid(mixed, w_mix_out, lead, xs, gate[layer, :, 1], [gs_of(layer, 2)],
                                   weight=1.0, bm=1024, sub=2, **per_batch)

        next_gs = [gs_of(layer + 1, 0)] if layer + 1 < depth else []
        if layer == n_s5 - 1:
            next_gs.append(kv_norm_g[None, :] * (1.0 + mod_kv[:, d:]))
        fuse_final = layer + 1 == depth and not next_gs
        xs, xgs, ssq = ffn((xg, ssq, shift_rows(shift[layer, :, 2])), xs, layer, 1, next_gs,
                           final_norm_g if fuse_final else None)
        if layer == n_s5 - 1:
            kv_in = (xgs[-1], ssq, shift_rows(mod_kv[:, :d]))
            kv = _mm_rope(kv_in, w_kv, (), rope, out_scale=1.0, out_dtype=F32, rope_cols=kvw,
                          **per_batch)
        if layer + 1 < depth:
            xg = xgs[0]

    out = xs if fuse_final else _final_norm(xs, final_norm_g)
    return out.reshape(nbatch, seq, d)
```

```python
import functools
import math

import numpy as np
import jax
import jax.numpy as jnp
from jax import lax
from jax.experimental import pallas as pl
from jax.experimental.pallas import tpu as pltpu

F32 = jnp.float32
BF16 = jnp.bfloat16

RMS_EPS = 1e-6
N_SUBLAYERS = 3
S5_GROUP = 16
S5_STATE = 64
HEAD_DIM = 64
Q_PER_KV = 8
WINDOW = 128
ROPE_THETA = 10000.0

LANES = 128
SUBLANES = 8
S5_TILE = 256
VMEM_LIMIT = 56 * 1024 * 1024


def _params(sem):
    return pltpu.CompilerParams(dimension_semantics=sem, vmem_limit_bytes=VMEM_LIMIT)


def _sigmoid(x):
    return 1.0 / (1.0 + jnp.exp(-x))


def _gelu_tanh(x):
    return 0.5 * x * (1.0 + jnp.tanh(math.sqrt(2.0 / math.pi) * (x + 0.044715 * (x * x * x))))


def _ada_kernel(c_ref, w_ref, b_ref, o_ref):
    c = c_ref[...]
    ca = (c * _sigmoid(c)).astype(BF16)
    o_ref[...] = jnp.dot(ca, w_ref[...].astype(BF16), preferred_element_type=F32) + b_ref[...]


def _ada(c_pad, w, b, bn=2048):
    s, d, n = w.shape
    return pl.pallas_call(
        _ada_kernel,
        grid=(s, n // bn),
        in_specs=[pl.BlockSpec((SUBLANES, d), lambda i, j: (0, 0)),
                  pl.BlockSpec((None, d, bn), lambda i, j: (i, 0, j)),
                  pl.BlockSpec((None, 1, bn), lambda i, j: (i, 0, j))],
        out_specs=pl.BlockSpec((None, SUBLANES, bn), lambda i, j: (i, 0, j)),
        out_shape=jax.ShapeDtypeStruct((s, SUBLANES, n), F32),
        compiler_params=_params(("parallel", "parallel")),
        name="ada",
    )(c_pad, w, b.reshape(s, 1, n))


def _lane_block_sums(sq):
    acc = sq[:, :LANES]
    for j in range(1, sq.shape[1] // LANES):
        acc = acc + sq[:, j * LANES:(j + 1) * LANES]
    return acc


def _row_rsqrt(ssq_ref, rows, width):
    part = ssq_ref[0, rows, :]
    for p in range(1, ssq_ref.shape[0]):
        part = part + ssq_ref[p, rows, :]
    ms = jnp.sum(part, axis=-1, keepdims=True) * (1.0 / width)
    return lax.rsqrt(ms + RMS_EPS)


def _prep_kernel(x_ref, gs_ref, xg_ref, ssq_ref):
    x = x_ref[...]
    xg_ref[...] = (x * gs_ref[...]).astype(xg_ref.dtype)
    ssq_ref[0] = _lane_block_sums(x * x)


def _prep(x, gs, *, rows_per_batch, bl=1024):
    m, d = x.shape
    per = rows_per_batch // bl
    nb = gs.shape[0]
    return pl.pallas_call(
        _prep_kernel,
        grid=(m // bl,),
        in_specs=[pl.BlockSpec((bl, d), lambda i: (i, 0)),
                  pl.BlockSpec((None, 1, d), lambda i: (i // per, 0, 0))],
        out_specs=[pl.BlockSpec((bl, d), lambda i: (i, 0)),
                   pl.BlockSpec((1, bl, LANES), lambda i: (0, i, 0))],
        out_shape=[jax.ShapeDtypeStruct((m, d), BF16), jax.ShapeDtypeStruct((1, m, LANES), F32)],
        compiler_params=_params(("parallel",)),
        name="prep",
    )(x, gs.reshape(nb, 1, d))


def _final_norm_kernel(x_ref, g_ref, o_ref):
    x = x_ref[...]
    ms = jnp.mean(x * x, axis=-1, keepdims=True)
    o_ref[...] = x * lax.rsqrt(ms + RMS_EPS) * g_ref[...]


def _final_norm(x, g, bl=512):
    m, d = x.shape
    return pl.pallas_call(
        _final_norm_kernel,
        grid=(m // bl,),
        in_specs=[pl.BlockSpec((bl, d), lambda i: (i, 0)),
                  pl.BlockSpec((1, d), lambda i: (0, 0))],
        out_specs=pl.BlockSpec((bl, d), lambda i: (i, 0)),
        out_shape=jax.ShapeDtypeStruct((m, d), F32),
        compiler_params=_params(("parallel",)),
        name="final_norm",
    )(x, g.reshape(1, d))


def _weight_spec(w, lead, bn, col0=0):
    k = w.shape[-2]
    mode = dict(pipeline_mode=pl.Buffered(1)) if bn == w.shape[-1] else {}
    return pl.BlockSpec((None,) * len(lead) + (k, bn), lambda j, i: tuple(lead) + (0, col0 + j), **mode)


def _weight_scratch(w, bn, count=1):
    return [pltpu.VMEM((w.shape[-2], bn), BF16)] * count


def _stage_weight(w_ref, wb_ref, shift_ref=None, sw_ref=None):
    @pl.when(pl.program_id(1) == 0)
    def _():
        wb_ref[...] = w_ref[...].astype(BF16)
        if sw_ref is not None:
            sw_ref[...] = jnp.dot(shift_ref[...], wb_ref[...], preferred_element_type=F32)


def _modulated_dot(xg_ref, wb_ref, ssq_ref, sw_ref, rows_per_batch, rows=None):
    bm = xg_ref.shape[0]
    rows = slice(0, bm) if rows is None else rows
    b = (pl.program_id(1) * bm + rows.start) // rows_per_batch
    acc = jnp.dot(xg_ref[rows, :], wb_ref[...], preferred_element_type=F32)
    return _row_rsqrt(ssq_ref, rows, xg_ref.shape[1]) * acc + sw_ref[pl.ds(b, 1), :]


def _mm_plain_kernel(x_ref, ssq_ref, shift_ref, w_ref, o_ref, wb_ref, sw_ref, *, per):
    _stage_weight(w_ref, wb_ref, shift_ref, sw_ref)
    o_ref[...] = _modulated_dot(x_ref, wb_ref, ssq_ref, sw_ref, per).astype(o_ref.dtype)


def _mm_swiglu_kernel(x_ref, ssq_ref, shift_ref, wg_ref, wu_ref, wo_ref, o_ref, wob_ref,
                      wgb_ref, wub_ref, swg_ref, swu_ref, *, per, sub):
    _stage_weight(wg_ref, wgb_ref, shift_ref, swg_ref)
    _stage_weight(wu_ref, wub_ref, shift_ref, swu_ref)
    step = x_ref.shape[0] // sub
    for r in range(sub):
        rows = slice(r * step, (r + 1) * step)
        g = _modulated_dot(x_ref, wgb_ref, ssq_ref, swg_ref, per, rows)
        u = _modulated_dot(x_ref, wub_ref, ssq_ref, swu_ref, per, rows)
        o_ref[rows, :] = (g * _sigmoid(g) * u).astype(o_ref.dtype)
    wob_ref[...] = wo_ref[...].astype(wob_ref.dtype)


def _mm_glu_kernel(x_ref, w_ref, y_ref, b_ref, o_ref, wb_ref):
    _stage_weight(w_ref, wb_ref)
    acc = jnp.dot(x_ref[...], wb_ref[...], preferred_element_type=F32)
    o_ref[...] = (y_ref[...].astype(F32) * _sigmoid(acc + b_ref[...])).astype(o_ref.dtype)


def _mm_rope_kernel(x_ref, ssq_ref, shift_ref, w_ref, cos_ref, sin_lo_ref, sin_hi_ref, o_ref,
                    wb_ref, sw_ref, *, per, out_scale, rope_groups):
    _stage_weight(w_ref, wb_ref, shift_ref, sw_ref)
    acc = _modulated_dot(x_ref, wb_ref, ssq_ref, sw_ref, per)
    cos, sin_lo, sin_hi = cos_ref[...], sin_lo_ref[...], sin_hi_ref[...]
    half = HEAD_DIM // 2
    for j in range(acc.shape[1] // LANES):
        a = acc[:, j * LANES:(j + 1) * LANES]
        if j < rope_groups:
            a = (a * cos + pltpu.roll(a, LANES - half, axis=1) * sin_lo
                 + pltpu.roll(a, half, axis=1) * sin_hi)
        o_ref[:, j * LANES:(j + 1) * LANES] = (a * out_scale).astype(o_ref.dtype)


def _mm_resid_kernel(x_ref, w_ref, r_ref, gate_ref, *rest, weight, n_next, staged, final, sub):
    n_extra = 1 if final else n_next
    gs_refs, o_ref, xg_refs = rest[:n_extra], rest[n_extra], rest[n_extra + 1:2 * n_next + 1]
    if staged:
        wb_ref = rest[-1]
        _stage_weight(w_ref, wb_ref)
    else:
        wb_ref = w_ref
    coef = weight * (1.0 + gate_ref[...])
    step = x_ref.shape[0] // sub
    for r in range(sub):
        rows = slice(r * step, (r + 1) * step)
        acc = jnp.dot(x_ref[rows, :], wb_ref[...], preferred_element_type=F32)
        x_new = r_ref[rows, :] + coef * acc
        if final:
            ms = jnp.mean(x_new * x_new, axis=-1, keepdims=True)
            o_ref[rows, :] = x_new * lax.rsqrt(ms + RMS_EPS) * gs_refs[0][...]
            continue
        o_ref[rows, :] = x_new
        for gs_ref, xg_ref in zip(gs_refs, xg_refs):
            xg_ref[rows, :] = (x_new * gs_ref[...]).astype(xg_ref.dtype)
        if n_next:
            rest[2 * n_next + 1][rows, :] = _lane_block_sums(x_new * x_new)


_MM_SEM = ("arbitrary", "arbitrary")


def _modulated_specs(mod_in, bm):
    xg, ssq, _ = mod_in
    k = xg.shape[1]
    return [pl.BlockSpec((bm, k), lambda j, i: (i, 0)),
            pl.BlockSpec((ssq.shape[0], bm, LANES), lambda j, i: (0, i, 0)),
            pl.BlockSpec((SUBLANES, k), lambda j, i: (0, 0))]


def _mm_plain(mod_in, w, lead=(), *, out_dtype, rows_per_batch, bm=1024, bn=1024):
    xg, ssq, shift = mod_in
    m, k = xg.shape
    n = w.shape[-1]
    bn = min(bn, n)
    return pl.pallas_call(
        functools.partial(_mm_plain_kernel, per=rows_per_batch),
        grid=(n // bn, m // bm),
        in_specs=_modulated_specs(mod_in, bm) + [_weight_spec(w, lead, bn)],
        out_specs=pl.BlockSpec((bm, bn), lambda j, i: (i, j)),
        out_shape=jax.ShapeDtypeStruct((m, n), out_dtype),
        scratch_shapes=_weight_scratch(w, bn) + [pltpu.VMEM((SUBLANES, bn), F32)],
        compiler_params=_params(_MM_SEM),
        name="mm_plain",
    )(xg, ssq, shift, w)


def _mm_swiglu(mod_in, w_in, w_out, lead=(), *, rows_per_batch, bm=2048, bn=512, sub=2):
    xg, ssq, shift = mod_in
    m, k = xg.shape
    f = w_in.shape[-1] // 2
    nf, nm = f // bn, m // bm
    slab = f // (nf * nm)
    n_out = w_out.shape[-1]
    nl = len(lead)
    return pl.pallas_call(
        functools.partial(_mm_swiglu_kernel, per=rows_per_batch, sub=sub),
        grid=(nf, nm),
        in_specs=_modulated_specs(mod_in, bm) + [
            _weight_spec(w_in, lead, bn),
            _weight_spec(w_in, lead, bn, nf),
            pl.BlockSpec((None,) * nl + (slab, n_out), lambda j, i: tuple(lead) + (j * nm + i, 0))],
        out_specs=[pl.BlockSpec((bm, bn), lambda j, i: (i, j)),
                   pl.BlockSpec((slab, n_out), lambda j, i: (j * nm + i, 0))],
        out_shape=[jax.ShapeDtypeStruct((m, f), BF16), jax.ShapeDtypeStruct((f, n_out), BF16)],
        scratch_shapes=_weight_scratch(w_in, bn, 2) + [pltpu.VMEM((SUBLANES, bn), F32)] * 2,
        compiler_params=_params(_MM_SEM),
        name="mm_swiglu",
    )(xg, ssq, shift, w_in, w_in, w_out)


def _mm_glu(y, w, b, lead=(), *, bm=1024, bn=1024):
    m, k = y.shape
    n = w.shape[-1]
    return pl.pallas_call(
        _mm_glu_kernel,
        grid=(n // bn, m // bm),
        in_specs=[pl.BlockSpec((bm, k), lambda j, i: (i, 0)),
                  _weight_spec(w, lead, bn),
                  pl.BlockSpec((bm, bn), lambda j, i: (i, j)),
                  pl.BlockSpec((1, bn), lambda j, i: (0, j))],
        out_specs=pl.BlockSpec((bm, bn), lambda j, i: (i, j)),
        out_shape=jax.ShapeDtypeStruct((m, n), BF16),
        scratch_shapes=_weight_scratch(w, bn),
        compiler_params=_params(_MM_SEM),
        name="mm_glu",
    )(y, w, y, b.reshape(1, n))


def _mm_rope(mod_in, w, lead, rope, *, out_scale, out_dtype, rows_per_batch, rope_cols=None,
             bm=1024, bn=1024):
    xg, ssq, shift = mod_in
    m, k = xg.shape
    n = w.shape[-1]
    bn = min(bn, n)
    assert rope_cols is None or bn == n
    rope_groups = (bn if rope_cols is None else rope_cols) // LANES
    table = pl.BlockSpec((bm, LANES), lambda j, i: (i, 0))
    return pl.pallas_call(
        functools.partial(_mm_rope_kernel, per=rows_per_batch, out_scale=out_scale,
                          rope_groups=rope_groups),
        grid=(n // bn, m // bm),
        in_specs=_modulated_specs(mod_in, bm) + [_weight_spec(w, lead, bn), table, table, table],
        out_specs=pl.BlockSpec((bm, bn), lambda j, i: (i, j)),
        out_shape=jax.ShapeDtypeStruct((m, n), out_dtype),
        scratch_shapes=_weight_scratch(w, bn) + [pltpu.VMEM((SUBLANES, bn), F32)],
        compiler_params=_params(_MM_SEM),
        name="mm_rope",
    )(xg, ssq, shift, w, *rope)


def _mm_resid(x, w, lead, resid, gate, next_gs=(), *, weight, rows_per_batch, final_g=None,
              bm=512, bn=1024, sub=1):
    m, k = x.shape
    n = w.shape[-1]
    final = final_g is not None
    if final:
        bm, bn = bm // 2, n
    per = rows_per_batch // bm
    nb = gate.shape[0]
    n_next = len(next_gs)
    staged = w.dtype != BF16
    tile = pl.BlockSpec((bm, bn), lambda j, i: (i, j))
    per_batch = pl.BlockSpec((None, 1, bn), lambda j, i: (i // per, 0, j))
    out_specs = [tile] * (1 + n_next)
    out_shape = [jax.ShapeDtypeStruct((m, n), F32)] + [jax.ShapeDtypeStruct((m, n), BF16)] * n_next
    if n_next:
        out_specs.append(pl.BlockSpec((None, bm, LANES), lambda j, i: (j, i, 0)))
        out_shape.append(jax.ShapeDtypeStruct((n // bn, m, LANES), F32))
    outs = pl.pallas_call(
        functools.partial(_mm_resid_kernel, weight=weight, n_next=n_next, staged=staged, final=final,
                          sub=sub),
        grid=(n // bn, m // bm),
        in_specs=([pl.BlockSpec((bm, k), lambda j, i: (i, 0)),
                   _weight_spec(w, lead, bn), tile, per_batch] + [per_batch] * n_next
                  + [pl.BlockSpec((1, bn), lambda j, i: (0, j))] * final),
        out_specs=out_specs,
        out_shape=out_shape,
        scratch_shapes=_weight_scratch(w, bn) if staged else [],
        compiler_params=_params(_MM_SEM),
        name="mm_resid",
    )(x, w, resid, gate.reshape(nb, 1, n), *[gs.reshape(nb, 1, n) for gs in next_gs],
      *([final_g.reshape(1, n)] if final else []))
    return outs[0], list(outs[1:1 + n_next]), (outs[-1] if n_next else None)


def _rope_kernel(pos_ref, invf_ref, lo_ref, cos_ref, sin_lo_ref, sin_hi_ref):
    ang = pos_ref[...].astype(F32) * invf_ref[...]
    sin = jnp.sin(ang)
    lo = lo_ref[...]
    cos_ref[...] = jnp.cos(ang)
    sin_lo_ref[...] = -sin * lo
    sin_hi_ref[...] = sin * (1.0 - lo)


def _rope_tables(positions, bl=2048):
    m = positions.size
    half = HEAD_DIM // 2
    inv_freq = 1.0 / (ROPE_THETA ** (jnp.arange(0, HEAD_DIM, 2, dtype=F32) / HEAD_DIM))
    invf = jnp.tile(inv_freq, LANES // half).reshape(1, LANES)
    lo = jnp.asarray(np.arange(LANES) % HEAD_DIM < half, F32).reshape(1, LANES)
    return pl.pallas_call(
        _rope_kernel,
        grid=(m // bl,),
        in_specs=[pl.BlockSpec((bl, 1), lambda i: (i, 0)),
                  pl.BlockSpec((1, LANES), lambda i: (0, 0)),
                  pl.BlockSpec((1, LANES), lambda i: (0, 0))],
        out_specs=[pl.BlockSpec((bl, LANES), lambda i: (i, 0))] * 3,
        out_shape=[jax.ShapeDtypeStruct((m, LANES), F32)] * 3,
        compiler_params=_params(("parallel",)),
        name="rope_tables",
    )(positions.reshape(m, 1), invf, lo)


def _s5_input_map(v, bblk_ref, bu_ref, *, tc, pitch, nb):
    half_w = 4 * LANES
    vb = v.reshape(nb * tc, S5_TILE).astype(BF16)
    for half in range(2):
        r = jnp.dot(vb, bblk_ref[:, half * 2 * half_w:(half + 1) * 2 * half_w],
                    preferred_element_type=F32)
        for b in range(nb):
            for l in range(8):
                bu_ref[l, pl.ds((half * nb + b) * pitch, tc), :] = (
                    r[b * tc:(b + 1) * tc, l * LANES:(l + 1) * LANES])


def _s5_scan(bu_ref, st_ref, a, xs, *, tc, pitch):
    xs = list(xs)
    for t in range(tc):
        for l in range(4):
            bur = bu_ref[l, pl.ds(t, SUBLANES, stride=pitch), :]
            bui = bu_ref[4 + l, pl.ds(t, SUBLANES, stride=pitch), :]
            xr, xi = xs[l], xs[4 + l]
            nr = a[l] * xr - a[4 + l] * xi + bur
            ni = a[l] * xi + a[4 + l] * xr + bui
            st_ref[l, pl.ds(t, SUBLANES, stride=pitch), :] = nr
            st_ref[4 + l, pl.ds(t, SUBLANES, stride=pitch), :] = ni
            xs[l], xs[4 + l] = nr, ni
    return xs


def _s5_output_map(st_ref, cblk_ref, v, d, *, tc, pitch, nb):
    half_w = 4 * LANES
    acc = None
    for half in range(2):
        blocks = []
        for b in range(nb):
            blocks.append(jnp.concatenate(
                [st_ref[l, pl.ds((half * nb + b) * pitch, tc), :] for l in range(8)], axis=1))
        s = jnp.concatenate(blocks, axis=0).astype(BF16)
        part = jnp.dot(s, cblk_ref[half * 2 * half_w:(half + 1) * 2 * half_w, :],
                       preferred_element_type=F32)
        acc = part if acc is None else acc + part
    y = _gelu_tanh(acc + d * v.reshape(nb * tc, S5_TILE))
    return y.reshape(nb, tc, S5_TILE)


def _s5_kernel(va_ref, vc_ref, bblk_ref, cblk_ref, a0_ref, a1_ref, d_ref, y_ref,
               bu0_ref, bu1_ref, st0_ref, st1_ref, x_ref, *, tc, pitch, nb, steps_per_tile):
    s = pl.program_id(0)
    kw = dict(tc=tc, pitch=pitch, nb=nb)

    @pl.when(s == 0)
    def _():
        x_ref[...] = jnp.zeros_like(x_ref)
        bu1_ref[...] = jnp.zeros_like(bu1_ref)
        st0_ref[...] = jnp.zeros_like(st0_ref)

    bus, sts = (bu0_ref, bu1_ref), (st0_ref, st1_ref)
    d = d_ref[...]
    xs = [x_ref[l] for l in range(8)]
    for h, a_ref in enumerate((a0_ref, a1_ref)):
        _s5_input_map(va_ref[:, h * tc:(h + 1) * tc, :], bblk_ref, bus[h], **kw)
        if h == 1:
            xs = [jnp.where(s % steps_per_tile == 0, 0.0, x) for x in xs]
        xs = _s5_scan(bus[1 - h], sts[1 - h], [a_ref[l] for l in range(8)], xs, tc=tc, pitch=pitch)
        y = _s5_output_map(sts[h], cblk_ref, vc_ref[:, h * tc:(h + 1) * tc, :], d, **kw)
        y_ref[:, h * tc:(h + 1) * tc, :] = y.astype(y_ref.dtype)
    for l in range(8):
        x_ref[l] = xs[l]


def _s5_expand_kernel(m_in_ref, m_out_ref, own_ref, b_ref, c_ref):
    own = own_ref[...]
    copies = own.shape[0] // m_in_ref.shape[0]
    b_ref[...] = (jnp.concatenate([m_in_ref[...]] * copies, axis=0) * own).astype(b_ref.dtype)
    c_ref[...] = (jnp.concatenate([m_out_ref[...]] * copies, axis=0) * own).T.astype(c_ref.dtype)


def _s5_discretise(a_re, a_im, b_re, b_im, c_re, c_im, log_dt, nb):
    g, p = a_re.shape
    gpt = S5_TILE // S5_GROUP
    nt = g // gpt
    dt = jnp.exp(log_dt)[:, None]
    mag = jnp.exp(a_re * dt)
    lb_re, lb_im = mag * jnp.cos(a_im * dt), mag * jnp.sin(a_im * dt)
    den = a_re * a_re + a_im * a_im
    k_re = ((lb_re - 1.0) * a_re + lb_im * a_im) / den
    k_im = (lb_im * a_re - (lb_re - 1.0) * a_im) / den
    bb_re = k_re[..., None] * b_re - k_im[..., None] * b_im
    bb_im = k_re[..., None] * b_im + k_im[..., None] * b_re
    g8 = gpt // 2
    nstate = 2 * gpt * p

    def state_cols(t):
        x = t.shape[-1]
        return t.reshape(nt, 2, g8, p, x).transpose(0, 4, 1, 2, 3).reshape(nt, x, 2, g8 * p)

    col = np.arange(nstate)
    col_group = col // (2 * g8 * p) * g8 + col % (g8 * p) // p
    own = jnp.asarray(np.arange(S5_TILE)[:, None] // S5_GROUP == col_group[None, :], F32)
    m_in = jnp.stack([state_cols(bb_re), state_cols(bb_im)], axis=3).reshape(nt, S5_GROUP, nstate)
    m_out = jnp.stack([state_cols(c_re.swapaxes(1, 2)), state_cols(-c_im.swapaxes(1, 2))],
                      axis=3).reshape(nt, S5_GROUP, nstate)
    bblk, cblk = pl.pallas_call(
        _s5_expand_kernel,
        grid=(nt,),
        in_specs=[pl.BlockSpec((None, S5_GROUP, nstate), lambda j: (j, 0, 0)),
                  pl.BlockSpec((None, S5_GROUP, nstate), lambda j: (j, 0, 0)),
                  pl.BlockSpec((S5_TILE, nstate), lambda j: (0, 0))],
        out_specs=[pl.BlockSpec((None, S5_TILE, nstate), lambda j: (j, 0, 0)),
                   pl.BlockSpec((None, nstate, S5_TILE), lambda j: (j, 0, 0))],
        out_shape=[jax.ShapeDtypeStruct((nt, S5_TILE, nstate), BF16),
                   jax.ShapeDtypeStruct((nt, nstate, S5_TILE), BF16)],
        compiler_params=_params(("parallel",)),
        name="s5_expand",
    )(m_in, m_out, own)

    def a_slabs(lb):
        t = lb.reshape(nt, 2, 4, LANES)
        t = jnp.broadcast_to(t[:, :, None], (nt, 2, nb, 4, LANES))
        return t.transpose(0, 3, 1, 2, 4).reshape(nt, 4, 2 * nb, LANES)

    a = jnp.concatenate([a_slabs(lb_re), a_slabs(lb_im)], axis=1)
    return bblk, cblk, a


def _s5_core(v, bblk, cblk, a, d_skip, *, tc=128):
    nb, seq, d = v.shape
    assert 2 * nb == SUBLANES
    nt = d // S5_TILE
    spt = seq // (2 * tc)
    steps = nt * spt + 1
    pitch = tc + 4
    nstate = bblk.shape[2]
    slab = pltpu.VMEM((8, SUBLANES * pitch, LANES), F32)

    def pair_in(s):
        return jnp.minimum(s, steps - 2)

    def pair_out(s):
        return jnp.maximum(s - 1, 0)

    def tile_scan0(s):
        return jnp.maximum(2 * s - 1, 0) // (2 * spt)

    chunk = (nb, 2 * tc, S5_TILE)
    a_block = (None, 8, SUBLANES, LANES)
    return pl.pallas_call(
        functools.partial(_s5_kernel, tc=tc, pitch=pitch, nb=nb, steps_per_tile=spt),
        grid=(steps,),
        in_specs=[pl.BlockSpec(chunk, lambda s: (0, pair_in(s) % spt, pair_in(s) // spt)),
                  pl.BlockSpec(chunk, lambda s: (0, pair_out(s) % spt, pair_out(s) // spt)),
                  pl.BlockSpec((None, S5_TILE, nstate), lambda s: (pair_in(s) // spt, 0, 0)),
                  pl.BlockSpec((None, nstate, S5_TILE), lambda s: (pair_out(s) // spt, 0, 0)),
                  pl.BlockSpec(a_block, lambda s: (tile_scan0(s), 0, 0, 0)),
                  pl.BlockSpec(a_block, lambda s: (pair_in(s) // spt, 0, 0, 0)),
                  pl.BlockSpec((1, S5_TILE), lambda s: (0, pair_out(s) // spt))],
        out_specs=pl.BlockSpec(chunk, lambda s: (0, pair_out(s) % spt, pair_out(s) // spt)),
        out_shape=jax.ShapeDtypeStruct((nb, seq, d), BF16),
        scratch_shapes=[slab, slab, slab, slab, pltpu.VMEM((8, SUBLANES, LANES), F32)],
        compiler_params=_params(("arbitrary",)),
        name="s5_core",
    )(v, v, bblk, cblk, a, a, d_skip.reshape(1, d))


def _attn_kernel(sink_ref, bias0_ref, bias_ref, q_ref, kp_ref, kc_ref, vp_ref, vc_ref, o_ref,
                 *, n_kv, blk):
    nq = q_ref.shape[0] // blk
    nk = (nq + 1) * blk
    lane = lax.broadcasted_iota(jnp.int32, (nk, LANES), 1)
    pairs = Q_PER_KV // 2
    nt_dims = (((1,), (1,)), ((), ()))

    def on_slot(x, slot, s):
        x = x if s == slot else pltpu.roll(x, HEAD_DIM, axis=1)
        return jnp.where(lane // HEAD_DIM == s, x, 0.0)

    for kvh in range(n_kv):
        grp, slot = kvh // 2, kvh % 2
        cols = slice(grp * LANES, (grp + 1) * LANES)
        kcat = jnp.concatenate([kp_ref[:, cols], kc_ref[:, cols]], axis=0)
        vcat = jnp.concatenate([vp_ref[:, cols], vc_ref[:, cols]], axis=0)
        base = kvh * pairs
        out_t = [None] * nq
        for s in range(2):
            kexp = on_slot(kcat, slot, s).astype(BF16)
            vexp_t = on_slot(vcat, slot, s).T.astype(BF16)
            for t in range(nq):
                keys = slice(t * blk, (t + 2) * blk)
                bias = bias0_ref if t == 0 else bias_ref
                es, rdens = [], []
                for p in range(pairs):
                    sink = sink_ref[kvh * Q_PER_KV + 2 * p + s]
                    qp = q_ref[t * blk:(t + 1) * blk, (base + p) * LANES:(base + p + 1) * LANES]
                    sc = lax.dot_general(kexp[keys, :], qp, nt_dims,
                                         preferred_element_type=F32) + bias[...]
                    mx = jnp.maximum(jnp.max(sc, axis=0, keepdims=True), sink)
                    e = jnp.exp2(sc - mx)
                    den = jnp.sum(e, axis=0, keepdims=True) + jnp.exp2(sink - mx)
                    es.append(e.astype(BF16))
                    rdens.append(1.0 / den)
                part = jnp.dot(vexp_t[:, keys], jnp.concatenate(es, axis=1),
                               preferred_element_type=F32) * jnp.concatenate(rdens, axis=1)
                out_t[t] = part if out_t[t] is None else out_t[t] + part
        for t in range(nq):
            out = out_t[t].T
            for p in range(pairs):
                o_ref[t * blk:(t + 1) * blk, (base + p) * LANES:(base + p + 1) * LANES] = (
                    out[p * blk:(p + 1) * blk].astype(o_ref.dtype))


def _attention(q, kv, sinks, *, nbatch, blk=WINDOW, nq=8):
    m, dq = q.shape
    dkv = kv.shape[1] // 2
    steps = m // nbatch // (nq * blk)
    kj, qi = np.arange(2 * blk)[:, None], np.arange(blk)[None, :]
    window = (kj > qi) & (kj <= qi + blk)
    bias = jnp.asarray(np.where(np.stack([window & (kj >= blk), window]), 0.0, -1e30), F32)
    bias_block = (None, 2 * blk, blk)

    def this(col):
        return lambda b, n: (b * steps + n, col)

    def prev(col):
        return lambda b, n: (b * steps * nq + jnp.maximum(n * nq - 1, 0), col)

    return pl.pallas_call(
        functools.partial(_attn_kernel, n_kv=dkv // HEAD_DIM, blk=blk),
        grid=(nbatch, steps),
        in_specs=[pl.BlockSpec(memory_space=pltpu.SMEM),
                  pl.BlockSpec(bias_block, lambda b, n: (jnp.minimum(n, 1), 0, 0)),
                  pl.BlockSpec(bias_block, lambda b, n: (1, 0, 0)),
                  pl.BlockSpec((nq * blk, dq), this(0)),
                  pl.BlockSpec((blk, dkv), prev(0)),
                  pl.BlockSpec((nq * blk, dkv), this(0)),
                  pl.BlockSpec((blk, dkv), prev(1)),
                  pl.BlockSpec((nq * blk, dkv), this(1))],
        out_specs=pl.BlockSpec((nq * blk, dq), this(0)),
        out_shape=jax.ShapeDtypeStruct((m, dq), BF16),
        compiler_params=_params(("parallel", "arbitrary")),
        name="attention",
    )(sinks * math.log2(math.e), bias, bias, q, kv, kv, kv, kv)


def kernel(x, c, positions, norm_g, w_ada, b_ada, w_ff_in, w_ff_out, s5_w_in, s5_a_re, s5_a_im, s5_b_re, s5_b_im, s5_c_re, s5_c_im, s5_d, s5_log_dt, s5_w_glu, s5_b_glu, s5_w_out, kv_norm_g, w_ada_kv, b_ada_kv, w_kv, attn_w_q, attn_sinks, attn_w_o, final_norm_g):
    nbatch, seq, d = x.shape
    depth = norm_g.shape[0]
    n_s5 = s5_w_in.shape[0]
    m = nbatch * seq
    kvw = w_kv.shape[1] // 2

    c_pad = jnp.zeros((SUBLANES, d), F32).at[:nbatch].set(c)
    mods = _ada(c_pad, w_ada, b_ada)[:, :nbatch]
    mod_kv = _ada(c_pad, w_ada_kv[None], b_ada_kv[None])[0, :nbatch]
    rope = _rope_tables(positions)

    per_batch = dict(rows_per_batch=seq)
    mod = mods.reshape(depth, nbatch, N_SUBLAYERS, 3, d)
    shift, scale, gate = mod[:, :, :, 0], mod[:, :, :, 1], mod[:, :, :, 2]

    def gs_of(layer, sub):
        return norm_g[layer, sub][None, :] * (1.0 + scale[layer, :, sub])

    def shift_rows(sh):
        return jnp.zeros((SUBLANES, d), F32).at[:nbatch].set(sh).astype(BF16)

    def ffn(mod_in, xs, layer, which, next_gs, final_g=None):
        act, w_out = _mm_swiglu(mod_in, w_ff_in, w_ff_out, (layer, which), **per_batch)
        return _mm_resid(act, w_out, (), xs, gate[layer, :, 2 * which], next_gs,
                         weight=0.5, final_g=final_g, **per_batch)

    xs = x.reshape(m, d)
    xg, ssq = _prep(xs, gs_of(0, 0), **per_batch)
    kv = None
    for layer in range(depth):
        xs, (xg,), ssq = ffn((xg, ssq, shift_rows(shift[layer, :, 0])), xs, layer, 0, [gs_of(layer, 1)])

        mix_in = (xg, ssq, shift_rows(shift[layer, :, 1]))
        if layer < n_s5:
            i = layer
            v = _mm_plain(mix_in, s5_w_in, (i,), out_dtype=F32, **per_batch)
            bblk, cblk, a = _s5_discretise(s5_a_re[i], s5_a_im[i], s5_b_re[i], s5_b_im[i],
                                           s5_c_re[i], s5_c_im[i], s5_log_dt[i], nbatch)
            y = _s5_core(v.reshape(nbatch, seq, d), bblk, cblk, a, s5_d[i].reshape(-1)).reshape(m, d)
            mixed = _mm_glu(y, s5_w_glu, s5_b_glu[i], (i,))
            w_mix_out, lead = s5_w_out, (i,)
        else:
            j = layer - n_s5
            q = _mm_rope(mix_in, attn_w_q, (j,), rope, out_dtype=BF16,
                         out_scale=HEAD_DIM ** -0.5 * math.log2(math.e), **per_batch)
            mixed = _attention(q, kv, attn_sinks[j], nbatch=nbatch)
            w_mix_out, lead = attn_w_o, (j,)
        xs, (xg,), ssq = _mm_resid(mixed, w_mix_out, lead, xs, gate[layer, :, 1], [gs_of(layer, 2)],
                                   weight=1.0, bm=1024, sub=2, **per_batch)

        next_gs = [gs_of(layer + 1, 0)] if layer + 1 < depth else []
        if layer == n_s5 - 1:
            next_gs.append(kv_norm_g[None, :] * (1.0 + mod_kv[:, d:]))
        fuse_final = layer + 1 == depth and not next_gs
        xs, xgs, ssq = ffn((xg, ssq, shift_rows(shift[layer, :, 2])), xs, layer, 1, next_gs,
                           final_norm_g if fuse_final else None)
        if layer == n_s5 - 1:
            kv_in = (xgs[-1], ssq, shift_rows(mod_kv[:, :d]))
            kv = _mm_rope(kv_in, w_kv, (), rope, out_scale=1.0, out_dtype=F32, rope_cols=kvw,
                          **per_batch)
        if layer + 1 < depth:
            xg = xgs[0]

    out = xs if fuse_final else _final_norm(xs, final_norm_g)
    return out.reshape(nbatch, seq, d)
```

```python
import functools
import math

import numpy as np
import jax
import jax.numpy as jnp
from jax import lax
from jax.experimental import pallas as pl
from jax.experimental.pallas import tpu as pltpu

F32 = jnp.float32
BF16 = jnp.bfloat16

RMS_EPS = 1e-6
N_SUBLAYERS = 3
S5_GROUP = 16
S5_STATE = 64
HEAD_DIM = 64
Q_PER_KV = 8
WINDOW = 128
ROPE_THETA = 10000.0

LANES = 128
SUBLANES = 8
S5_TILE = 256
VMEM_LIMIT = 56 * 1024 * 1024


def _params(sem):
    return pltpu.CompilerParams(dimension_semantics=sem, vmem_limit_bytes=VMEM_LIMIT)


def _sigmoid(x):
    return 1.0 / (1.0 + jnp.exp(-x))


def _gelu_tanh(x):
    return 0.5 * x * (1.0 + jnp.tanh(math.sqrt(2.0 / math.pi) * (x + 0.044715 * (x * x * x))))


def _ada_kernel(c_ref, w_ref, b_ref, o_ref):
    c = c_ref[...]
    ca = (c * _sigmoid(c)).astype(BF16)
    o_ref[...] = jnp.dot(ca, w_ref[...].astype(BF16), preferred_element_type=F32) + b_ref[...]


def _ada(c_pad, w, b, bn=2048):
    s, d, n = w.shape
    return pl.pallas_call(
        _ada_kernel,
        grid=(s, n // bn),
        in_specs=[pl.BlockSpec((SUBLANES, d), lambda i, j: (0, 0)),
                  pl.BlockSpec((None, d, bn), lambda i, j: (i, 0, j)),
                  pl.BlockSpec((None, 1, bn), lambda i, j: (i, 0, j))],
        out_specs=pl.BlockSpec((None, SUBLANES, bn), lambda i, j: (i, 0, j)),
        out_shape=jax.ShapeDtypeStruct((s, SUBLANES, n), F32),
        compiler_params=_params(("parallel", "parallel")),
        name="ada",
    )(c_pad, w, b.reshape(s, 1, n))


def _lane_block_sums(sq):
    acc = sq[:, :LANES]
    for j in range(1, sq.shape[1] // LANES):
        acc = acc + sq[:, j * LANES:(j + 1) * LANES]
    return acc


def _row_rsqrt(ssq_ref, rows, width):
    part = ssq_ref[0, rows, :]
    for p in range(1, ssq_ref.shape[0]):
        part = part + ssq_ref[p, rows, :]
    ms = jnp.sum(part, axis=-1, keepdims=True) * (1.0 / width)
    return lax.rsqrt(ms + RMS_EPS)


def _prep_kernel(x_ref, gs_ref, xg_ref, ssq_ref):
    x = x_ref[...]
    xg_ref[...] = (x * gs_ref[...]).astype(xg_ref.dtype)
    ssq_ref[0] = _lane_block_sums(x * x)


def _prep(x, gs, *, rows_per_batch, bl=1024):
    m, d = x.shape
    per = rows_per_batch // bl
    nb = gs.shape[0]
    return pl.pallas_call(
        _prep_kernel,
        grid=(m // bl,),
        in_specs=[pl.BlockSpec((bl, d), lambda i: (i, 0)),
                  pl.BlockSpec((None, 1, d), lambda i: (i // per, 0, 0))],
        out_specs=[pl.BlockSpec((bl, d), lambda i: (i, 0)),
                   pl.BlockSpec((1, bl, LANES), lambda i: (0, i, 0))],
        out_shape=[jax.ShapeDtypeStruct((m, d), BF16), jax.ShapeDtypeStruct((1, m, LANES), F32)],
        compiler_params=_params(("parallel",)),
        name="prep",
    )(x, gs.reshape(nb, 1, d))


def _final_norm_kernel(x_ref, g_ref, o_ref):
    x = x_ref[...]
    ms = jnp.mean(x * x, axis=-1, keepdims=True)
    o_ref[...] = x * lax.rsqrt(ms + RMS_EPS) * g_ref[...]


def _final_norm(x, g, bl=512):
    m, d = x.shape
    return pl.pallas_call(
        _final_norm_kernel,
        grid=(m // bl,),
        in_specs=[pl.BlockSpec((bl, d), lambda i: (i, 0)),
                  pl.BlockSpec((1, d), lambda i: (0, 0))],
        out_specs=pl.BlockSpec((bl, d), lambda i: (i, 0)),
        out_shape=jax.ShapeDtypeStruct((m, d), F32),
        compiler_params=_params(("parallel",)),
        name="final_norm",
    )(x, g.reshape(1, d))


def _weight_spec(w, lead, bn, col0=0):
    k = w.shape[-2]
    mode = dict(pipeline_mode=pl.Buffered(1)) if bn == w.shape[-1] else {}
    return pl.BlockSpec((None,) * len(lead) + (k, bn), lambda j, i: tuple(lead) + (0, col0 + j), **mode)


def _weight_scratch(w, bn, count=1):
    return [pltpu.VMEM((w.shape[-2], bn), BF16)] * count


def _stage_weight(w_ref, wb_ref, shift_ref=None, sw_ref=None):
    @pl.when(pl.program_id(1) == 0)
    def _():
        wb_ref[...] = w_ref[...].astype(BF16)
        if sw_ref is not None:
            sw_ref[...] = jnp.dot(shift_ref[...], wb_ref[...], preferred_element_type=F32)


def _modulated_dot(xg_ref, wb_ref, ssq_ref, sw_ref, rows_per_batch, rows=None):
    bm = xg_ref.shape[0]
    rows = slice(0, bm) if rows is None else rows
    b = (pl.program_id(1) * bm + rows.start) // rows_per_batch
    acc = jnp.dot(xg_ref[rows, :], wb_ref[...], preferred_element_type=F32)
    return _row_rsqrt(ssq_ref, rows, xg_ref.shape[1]) * acc + sw_ref[pl.ds(b, 1), :]


def _mm_plain_kernel(x_ref, ssq_ref, shift_ref, w_ref, o_ref, wb_ref, sw_ref, *, per):
    _stage_weight(w_ref, wb_ref, shift_ref, sw_ref)
    o_ref[...] = _modulated_dot(x_ref, wb_ref, ssq_ref, sw_ref, per).astype(o_ref.dtype)


def _mm_swiglu_kernel(x_ref, ssq_ref, shift_ref, wg_ref, wu_ref, wo_ref, o_ref, wob_ref,
                      wgb_ref, wub_ref, swg_ref, swu_ref, *, per, sub):
    _stage_weight(wg_ref, wgb_ref, shift_ref, swg_ref)
    _stage_weight(wu_ref, wub_ref, shift_ref, swu_ref)
    step = x_ref.shape[0] // sub
    for r in range(sub):
        rows = slice(r * step, (r + 1) * step)
        g = _modulated_dot(x_ref, wgb_ref, ssq_ref, swg_ref, per, rows)
        u = _modulated_dot(x_ref, wub_ref, ssq_ref, swu_ref, per, rows)
        o_ref[rows, :] = (g * _sigmoid(g) * u).astype(o_ref.dtype)
    wob_ref[...] = wo_ref[...].astype(wob_ref.dtype)


def _mm_glu_kernel(x_ref, w_ref, y_ref, b_ref, o_ref, wb_ref):
    _stage_weight(w_ref, wb_ref)
    acc = jnp.dot(x_ref[...], wb_ref[...], preferred_element_type=F32)
    o_ref[...] = (y_ref[...].astype(F32) * _sigmoid(acc + b_ref[...])).astype(o_ref.dtype)


def _mm_rope_kernel(x_ref, ssq_ref, shift_ref, w_ref, cos_ref, sin_lo_ref, sin_hi_ref, o_ref,
                    wb_ref, sw_ref, *, per, out_scale, rope_groups):
    _stage_weight(w_ref, wb_ref, shift_ref, sw_ref)
    acc = _modulated_dot(x_ref, wb_ref, ssq_ref, sw_ref, per)
    cos, sin_lo, sin_hi = cos_ref[...], sin_lo_ref[...], sin_hi_ref[...]
    half = HEAD_DIM // 2
    for j in range(acc.shape[1] // LANES):
        a = acc[:, j * LANES:(j + 1) * LANES]
        if j < rope_groups:
            a = (a * cos + pltpu.roll(a, LANES - half, axis=1) * sin_lo
                 + pltpu.roll(a, half, axis=1) * sin_hi)
        o_ref[:, j * LANES:(j + 1) * LANES] = (a * out_scale).astype(o_ref.dtype)


def _mm_resid_kernel(x_ref, w_ref, r_ref, gate_ref, *rest, weight, n_next, staged, final, sub):
    n_extra = 1 if final else n_next
    gs_refs, o_ref, xg_refs = rest[:n_extra], rest[n_extra], rest[n_extra + 1:2 * n_next + 1]
    if staged:
        wb_ref = rest[-1]
        _stage_weight(w_ref, wb_ref)
    else:
        wb_ref = w_ref
    coef = weight * (1.0 + gate_ref[...])
    step = x_ref.shape[0] // sub
    for r in range(sub):
        rows = slice(r * step, (r + 1) * step)
        acc = jnp.dot(x_ref[rows, :], wb_ref[...], preferred_element_type=F32)
        x_new = r_ref[rows, :] + coef * acc
        if final:
            ms = jnp.mean(x_new * x_new, axis=-1, keepdims=True)
            o_ref[rows, :] = x_new * lax.rsqrt(ms + RMS_EPS) * gs_refs[0][...]
            continue
        o_ref[rows, :] = x_new
        for gs_ref, xg_ref in zip(gs_refs, xg_refs):
            xg_ref[rows, :] = (x_new * gs_ref[...]).astype(xg_ref.dtype)
        if n_next:
            rest[2 * n_next + 1][rows, :] = _lane_block_sums(x_new * x_new)


_MM_SEM = ("arbitrary", "arbitrary")


def _modulated_specs(mod_in, bm):
    xg, ssq, _ = mod_in
    k = xg.shape[1]
    return [pl.BlockSpec((bm, k), lambda j, i: (i, 0)),
            pl.BlockSpec((ssq.shape[0], bm, LANES), lambda j, i: (0, i, 0)),
            pl.BlockSpec((SUBLANES, k), lambda j, i: (0, 0))]


def _mm_plain(mod_in, w, lead=(), *, out_dtype, rows_per_batch, bm=1024, bn=1024):
    xg, ssq, shift = mod_in
    m, k = xg.shape
    n = w.shape[-1]
    bn = min(bn, n)
    return pl.pallas_call(
        functools.partial(_mm_plain_kernel, per=rows_per_batch),
        grid=(n // bn, m // bm),
        in_specs=_modulated_specs(mod_in, bm) + [_weight_spec(w, lead, bn)],
        out_specs=pl.BlockSpec((bm, bn), lambda j, i: (i, j)),
        out_shape=jax.ShapeDtypeStruct((m, n), out_dtype),
        scratch_shapes=_weight_scratch(w, bn) + [pltpu.VMEM((SUBLANES, bn), F32)],
        compiler_params=_params(_MM_SEM),
        name="mm_plain",
    )(xg, ssq, shift, w)


def _mm_swiglu(mod_in, w_in, w_out, lead=(), *, rows_per_batch, bm=2048, bn=512, sub=2):
    xg, ssq, shift = mod_in
    m, k = xg.shape
    f = w_in.shape[-1] // 2
    nf, nm = f // bn, m // bm
    slab = f // (nf * nm)
    n_out = w_out.shape[-1]
    nl = len(lead)
    return pl.pallas_call(
        functools.partial(_mm_swiglu_kernel, per=rows_per_batch, sub=sub),
        grid=(nf, nm),
        in_specs=_modulated_specs(mod_in, bm) + [
            _weight_spec(w_in, lead, bn),
            _weight_spec(w_in, lead, bn, nf),
            pl.BlockSpec((None,) * nl + (slab, n_out), lambda j, i: tuple(lead) + (j * nm + i, 0))],
        out_specs=[pl.BlockSpec((bm, bn), lambda j, i: (i, j)),
                   pl.BlockSpec((slab, n_out), lambda j, i: (j * nm + i, 0))],
        out_shape=[jax.ShapeDtypeStruct((m, f), BF16), jax.ShapeDtypeStruct((f, n_out), BF16)],
        scratch_shapes=_weight_scratch(w_in, bn, 2) + [pltpu.VMEM((SUBLANES, bn), F32)] * 2,
        compiler_params=_params(_MM_SEM),
        name="mm_swiglu",
    )(xg, ssq, shift, w_in, w_in, w_out)


def _mm_glu(y, w, b, lead=(), *, bm=1024, bn=1024):
    m, k = y.shape
    n = w.shape[-1]
    return pl.pallas_call(
        _mm_glu_kernel,
        grid=(n // bn, m // bm),
        in_specs=[pl.BlockSpec((bm, k), lambda j, i: (i, 0)),
                  _weight_spec(w, lead, bn),
                  pl.BlockSpec((bm, bn), lambda j, i: (i, j)),
                  pl.BlockSpec((1, bn), lambda j, i: (0, j))],
        out_specs=pl.BlockSpec((bm, bn), lambda j, i: (i, j)),
        out_shape=jax.ShapeDtypeStruct((m, n), BF16),
        scratch_shapes=_weight_scratch(w, bn),
        compiler_params=_params(_MM_SEM),
        name="mm_glu",
    )(y, w, y, b.reshape(1, n))


def _mm_rope(mod_in, w, lead, rope, *, out_scale, out_dtype, rows_per_batch, rope_cols=None,
             bm=1024, bn=1024):
    xg, ssq, shift = mod_in
    m, k = xg.shape
    n = w.shape[-1]
    bn = min(bn, n)
    assert rope_cols is None or bn == n
    rope_groups = (bn if rope_cols is None else rope_cols) // LANES
    table = pl.BlockSpec((bm, LANES), lambda j, i: (i, 0))
    return pl.pallas_call(
        functools.partial(_mm_rope_kernel, per=rows_per_batch, out_scale=out_scale,
                          rope_groups=rope_groups),
        grid=(n // bn, m // bm),
        in_specs=_modulated_specs(mod_in, bm) + [_weight_spec(w, lead, bn), table, table, table],
        out_specs=pl.BlockSpec((bm, bn), lambda j, i: (i, j)),
        out_shape=jax.ShapeDtypeStruct((m, n), out_dtype),
        scratch_shapes=_weight_scratch(w, bn) + [pltpu.VMEM((SUBLANES, bn), F32)],
        compiler_params=_params(_MM_SEM),
        name="mm_rope",
    )(xg, ssq, shift, w, *rope)


def _mm_resid(x, w, lead, resid, gate, next_gs=(), *, weight, rows_per_batch, final_g=None,
              whole_rows=False, bm=512, bn=1024, sub=1):
    m, k = x.shape
    n = w.shape[-1]
    final = final_g is not None
    if final or whole_rows:
        bm, bn = bm // 2, n
    per = rows_per_batch // bm
    nb = gate.shape[0]
    n_next = len(next_gs)
    staged = w.dtype != BF16
    tile = pl.BlockSpec((bm, bn), lambda j, i: (i, j))
    per_batch = pl.BlockSpec((None, 1, bn), lambda j, i: (i // per, 0, j))
    out_specs = [tile] * (1 + n_next)
    out_shape = [jax.ShapeDtypeStruct((m, n), F32)] + [jax.ShapeDtypeStruct((m, n), BF16)] * n_next
    if n_next:
        out_specs.append(pl.BlockSpec((None, bm, LANES), lambda j, i: (j, i, 0)))
        out_shape.append(jax.ShapeDtypeStruct((n // bn, m, LANES), F32))
    outs = pl.pallas_call(
        functools.partial(_mm_resid_kernel, weight=weight, n_next=n_next, staged=staged, final=final,
                          sub=sub),
        grid=(n // bn, m // bm),
        in_specs=([pl.BlockSpec((bm, k), lambda j, i: (i, 0)),
                   _weight_spec(w, lead, bn), tile, per_batch] + [per_batch] * n_next
                  + [pl.BlockSpec((1, bn), lambda j, i: (0, j))] * final),
        out_specs=out_specs,
        out_shape=out_shape,
        scratch_shapes=_weight_scratch(w, bn) if staged else [],
        compiler_params=_params(_MM_SEM),
        name="mm_resid",
    )(x, w, resid, gate.reshape(nb, 1, n), *[gs.reshape(nb, 1, n) for gs in next_gs],
      *([final_g.reshape(1, n)] if final else []))
    return outs[0], list(outs[1:1 + n_next]), (outs[-1] if n_next else None)


def _rope_kernel(pos_ref, invf_ref, lo_ref, cos_ref, sin_lo_ref, sin_hi_ref):
    ang = pos_ref[...].astype(F32) * invf_ref[...]
    sin = jnp.sin(ang)
    lo = lo_ref[...]
    cos_ref[...] = jnp.cos(ang)
    sin_lo_ref[...] = -sin * lo
    sin_hi_ref[...] = sin * (1.0 - lo)


def _rope_tables(positions, bl=2048):
    m = positions.size
    half = HEAD_DIM // 2
    inv_freq = 1.0 / (ROPE_THETA ** (jnp.arange(0, HEAD_DIM, 2, dtype=F32) / HEAD_DIM))
    invf = jnp.tile(inv_freq, LANES // half).reshape(1, LANES)
    lo = jnp.asarray(np.arange(LANES) % HEAD_DIM < half, F32).reshape(1, LANES)
    return pl.pallas_call(
        _rope_kernel,
        grid=(m // bl,),
        in_specs=[pl.BlockSpec((bl, 1), lambda i: (i, 0)),
                  pl.BlockSpec((1, LANES), lambda i: (0, 0)),
                  pl.BlockSpec((1, LANES), lambda i: (0, 0))],
        out_specs=[pl.BlockSpec((bl, LANES), lambda i: (i, 0))] * 3,
        out_shape=[jax.ShapeDtypeStruct((m, LANES), F32)] * 3,
        compiler_params=_params(("parallel",)),
        name="rope_tables",
    )(positions.reshape(m, 1), invf, lo)


def _s5_input_map(v, bblk_ref, bu_ref, *, tc, pitch, nb):
    half_w = 4 * LANES
    vb = v.reshape(nb * tc, S5_TILE).astype(BF16)
    for half in range(2):
        r = jnp.dot(vb, bblk_ref[:, half * 2 * half_w:(half + 1) * 2 * half_w],
                    preferred_element_type=F32)
        for b in range(nb):
            for l in range(8):
                bu_ref[l, pl.ds((half * nb + b) * pitch, tc), :] = (
                    r[b * tc:(b + 1) * tc, l * LANES:(l + 1) * LANES])


def _s5_scan(bu_ref, st_ref, a, xs, *, tc, pitch):
    xs = list(xs)
    for t in range(tc):
        for l in range(4):
            bur = bu_ref[l, pl.ds(t, SUBLANES, stride=pitch), :]
            bui = bu_ref[4 + l, pl.ds(t, SUBLANES, stride=pitch), :]
            xr, xi = xs[l], xs[4 + l]
            nr = a[l] * xr - a[4 + l] * xi + bur
            ni = a[l] * xi + a[4 + l] * xr + bui
            st_ref[l, pl.ds(t, SUBLANES, stride=pitch), :] = nr
            st_ref[4 + l, pl.ds(t, SUBLANES, stride=pitch), :] = ni
            xs[l], xs[4 + l] = nr, ni
    return xs


def _s5_output_map(st_ref, cblk_ref, v, d, *, tc, pitch, nb):
    half_w = 4 * LANES
    acc = None
    for half in range(2):
        blocks = []
        for b in range(nb):
            blocks.append(jnp.concatenate(
                [st_ref[l, pl.ds((half * nb + b) * pitch, tc), :] for l in range(8)], axis=1))
        s = jnp.concatenate(blocks, axis=0).astype(BF16)
        part = jnp.dot(s, cblk_ref[half * 2 * half_w:(half + 1) * 2 * half_w, :],
                       preferred_element_type=F32)
        acc = part if acc is None else acc + part
    y = _gelu_tanh(acc + d * v.reshape(nb * tc, S5_TILE))
    return y.reshape(nb, tc, S5_TILE)


def _s5_kernel(va_ref, vc_ref, bblk_ref, cblk_ref, a0_ref, a1_ref, d_ref, y_ref,
               bu0_ref, bu1_ref, st0_ref, st1_ref, x_ref, *, tc, pitch, nb, steps_per_tile):
    s = pl.program_id(0)
    kw = dict(tc=tc, pitch=pitch, nb=nb)

    @pl.when(s == 0)
    def _():
        x_ref[...] = jnp.zeros_like(x_ref)
        bu1_ref[...] = jnp.zeros_like(bu1_ref)
        st0_ref[...] = jnp.zeros_like(st0_ref)

    bus, sts = (bu0_ref, bu1_ref), (st0_ref, st1_ref)
    d = d_ref[...]
    xs = [x_ref[l] for l in range(8)]
    for h, a_ref in enumerate((a0_ref, a1_ref)):
        _s5_input_map(va_ref[:, h * tc:(h + 1) * tc, :], bblk_ref, bus[h], **kw)
        if h == 1:
            xs = [jnp.where(s % steps_per_tile == 0, 0.0, x) for x in xs]
        xs = _s5_scan(bus[1 - h], sts[1 - h], [a_ref[l] for l in range(8)], xs, tc=tc, pitch=pitch)
        y = _s5_output_map(sts[h], cblk_ref, vc_ref[:, h * tc:(h + 1) * tc, :], d, **kw)
        y_ref[:, h * tc:(h + 1) * tc, :] = y.astype(y_ref.dtype)
    for l in range(8):
        x_ref[l] = xs[l]


def _s5_expand_kernel(m_in_ref, m_out_ref, own_ref, b_ref, c_ref):
    own = own_ref[...]
    copies = own.shape[0] // m_in_ref.shape[0]
    b_ref[...] = (jnp.concatenate([m_in_ref[...]] * copies, axis=0) * own).astype(b_ref.dtype)
    c_ref[...] = (jnp.concatenate([m_out_ref[...]] * copies, axis=0) * own).T.astype(c_ref.dtype)


def _s5_discretise(a_re, a_im, b_re, b_im, c_re, c_im, log_dt, nb):
    g, p = a_re.shape
    gpt = S5_TILE // S5_GROUP
    nt = g // gpt
    dt = jnp.exp(log_dt)[:, None]
    mag = jnp.exp(a_re * dt)
    lb_re, lb_im = mag * jnp.cos(a_im * dt), mag * jnp.sin(a_im * dt)
    den = a_re * a_re + a_im * a_im
    k_re = ((lb_re - 1.0) * a_re + lb_im * a_im) / den
    k_im = (lb_im * a_re - (lb_re - 1.0) * a_im) / den
    bb_re = k_re[..., None] * b_re - k_im[..., None] * b_im
    bb_im = k_re[..., None] * b_im + k_im[..., None] * b_re
    g8 = gpt // 2
    nstate = 2 * gpt * p

    def state_cols(t):
        x = t.shape[-1]
        return t.reshape(nt, 2, g8, p, x).transpose(0, 4, 1, 2, 3).reshape(nt, x, 2, g8 * p)

    col = np.arange(nstate)
    col_group = col // (2 * g8 * p) * g8 + col % (g8 * p) // p
    own = jnp.asarray(np.arange(S5_TILE)[:, None] // S5_GROUP == col_group[None, :], F32)
    m_in = jnp.stack([state_cols(bb_re), state_cols(bb_im)], axis=3).reshape(nt, S5_GROUP, nstate)
    m_out = jnp.stack([state_cols(c_re.swapaxes(1, 2)), state_cols(-c_im.swapaxes(1, 2))],
                      axis=3).reshape(nt, S5_GROUP, nstate)
    bblk, cblk = pl.pallas_call(
        _s5_expand_kernel,
        grid=(nt,),
        in_specs=[pl.BlockSpec((None, S5_GROUP, nstate), lambda j: (j, 0, 0)),
                  pl.BlockSpec((None, S5_GROUP, nstate), lambda j: (j, 0, 0)),
                  pl.BlockSpec((S5_TILE, nstate), lambda j: (0, 0))],
        out_specs=[pl.BlockSpec((None, S5_TILE, nstate), lambda j: (j, 0, 0)),
                   pl.BlockSpec((None, nstate, S5_TILE), lambda j: (j, 0, 0))],
        out_shape=[jax.ShapeDtypeStruct((nt, S5_TILE, nstate), BF16),
                   jax.ShapeDtypeStruct((nt, nstate, S5_TILE), BF16)],
        compiler_params=_params(("parallel",)),
        name="s5_expand",
    )(m_in, m_out, own)

    def a_slabs(lb):
        t = lb.reshape(nt, 2, 4, LANES)
        t = jnp.broadcast_to(t[:, :, None], (nt, 2, nb, 4, LANES))
        return t.transpose(0, 3, 1, 2, 4).reshape(nt, 4, 2 * nb, LANES)

    a = jnp.concatenate([a_slabs(lb_re), a_slabs(lb_im)], axis=1)
    return bblk, cblk, a


def _s5_core(v, bblk, cblk, a, d_skip, *, tc=128):
    nb, seq, d = v.shape
    assert 2 * nb == SUBLANES
    nt = d // S5_TILE
    spt = seq // (2 * tc)
    steps = nt * spt + 1
    pitch = tc + 4
    nstate = bblk.shape[2]
    slab = pltpu.VMEM((8, SUBLANES * pitch, LANES), F32)

    def pair_in(s):
        return jnp.minimum(s, steps - 2)

    def pair_out(s):
        return jnp.maximum(s - 1, 0)

    def tile_scan0(s):
        return jnp.maximum(2 * s - 1, 0) // (2 * spt)

    chunk = (nb, 2 * tc, S5_TILE)
    a_block = (None, 8, SUBLANES, LANES)
    return pl.pallas_call(
        functools.partial(_s5_kernel, tc=tc, pitch=pitch, nb=nb, steps_per_tile=spt),
        grid=(steps,),
        in_specs=[pl.BlockSpec(chunk, lambda s: (0, pair_in(s) % spt, pair_in(s) // spt)),
                  pl.BlockSpec(chunk, lambda s: (0, pair_out(s) % spt, pair_out(s) // spt)),
                  pl.BlockSpec((None, S5_TILE, nstate), lambda s: (pair_in(s) // spt, 0, 0)),
                  pl.BlockSpec((None, nstate, S5_TILE), lambda s: (pair_out(s) // spt, 0, 0)),
                  pl.BlockSpec(a_block, lambda s: (tile_scan0(s), 0, 0, 0)),
                  pl.BlockSpec(a_block, lambda s: (pair_in(s) // spt, 0, 0, 0)),
                  pl.BlockSpec((1, S5_TILE), lambda s: (0, pair_out(s) // spt))],
        out_specs=pl.BlockSpec(chunk, lambda s: (0, pair_out(s) % spt, pair_out(s) // spt)),
        out_shape=jax.ShapeDtypeStruct((nb, seq, d), BF16),
        scratch_shapes=[slab, slab, slab, slab, pltpu.VMEM((8, SUBLANES, LANES), F32)],
        compiler_params=_params(("arbitrary",)),
        name="s5_core",
    )(v, v, bblk, cblk, a, a, d_skip.reshape(1, d))


def _attn_kernel(sink_ref, bias0_ref, bias_ref, q_ref, kp_ref, kc_ref, vp_ref, vc_ref, o_ref,
                 *, n_kv, blk):
    nq = q_ref.shape[0] // blk
    nk = (nq + 1) * blk
    lane = lax.broadcasted_iota(jnp.int32, (nk, LANES), 1)
    pairs = Q_PER_KV // 2
    nt_dims = (((1,), (1,)), ((), ()))

    def on_slot(x, slot, s):
        x = x if s == slot else pltpu.roll(x, HEAD_DIM, axis=1)
        return jnp.where(lane // HEAD_DIM == s, x, 0.0)

    for kvh in range(n_kv):
        grp, slot = kvh // 2, kvh % 2
        cols = slice(grp * LANES, (grp + 1) * LANES)
        kcat = jnp.concatenate([kp_ref[:, cols], kc_ref[:, cols]], axis=0)
        vcat = jnp.concatenate([vp_ref[:, cols], vc_ref[:, cols]], axis=0)
        base = kvh * pairs
        out_t = [None] * nq
        for s in range(2):
            kexp = on_slot(kcat, slot, s).astype(BF16)
            vexp_t = on_slot(vcat, slot, s).T.astype(BF16)
            for t in range(nq):
                keys = slice(t * blk, (t + 2) * blk)
                bias = bias0_ref if t == 0 else bias_ref
                es, rdens = [], []
                for p in range(pairs):
                    sink = sink_ref[kvh * Q_PER_KV + 2 * p + s]
                    qp = q_ref[t * blk:(t + 1) * blk, (base + p) * LANES:(base + p + 1) * LANES]
                    sc = lax.dot_general(kexp[keys, :], qp, nt_dims,
                                         preferred_element_type=F32) + bias[...]
                    mx = jnp.maximum(jnp.max(sc, axis=0, keepdims=True), sink)
                    e = jnp.exp2(sc - mx)
                    den = jnp.sum(e, axis=0, keepdims=True) + jnp.exp2(sink - mx)
                    es.append(e.astype(BF16))
                    rdens.append(1.0 / den)
                part = jnp.dot(vexp_t[:, keys], jnp.concatenate(es, axis=1),
                               preferred_element_type=F32) * jnp.concatenate(rdens, axis=1)
                out_t[t] = part if out_t[t] is None else out_t[t] + part
        for t in range(nq):
            out = out_t[t].T
            for p in range(pairs):
                o_ref[t * blk:(t + 1) * blk, (base + p) * LANES:(base + p + 1) * LANES] = (
                    out[p * blk:(p + 1) * blk].astype(o_ref.dtype))


def _attention(q, kv, sinks, *, nbatch, blk=WINDOW, nq=4):
    m, dq = q.shape
    dkv = kv.shape[1] // 2
    steps = m // nbatch // (nq * blk)
    kj, qi = np.arange(2 * blk)[:, None], np.arange(blk)[None, :]
    window = (kj > qi) & (kj <= qi + blk)
    bias = jnp.asarray(np.where(np.stack([window & (kj >= blk), window]), 0.0, -1e30), F32)
    bias_block = (None, 2 * blk, blk)

    def this(col):
        return lambda b, n: (b * steps + n, col)

    def prev(col):
        return lambda b, n: (b * steps * nq + jnp.maximum(n * nq - 1, 0), col)

    return pl.pallas_call(
        functools.partial(_attn_kernel, n_kv=dkv // HEAD_DIM, blk=blk),
        grid=(nbatch, steps),
        in_specs=[pl.BlockSpec(memory_space=pltpu.SMEM),
                  pl.BlockSpec(bias_block, lambda b, n: (jnp.minimum(n, 1), 0, 0)),
                  pl.BlockSpec(bias_block, lambda b, n: (1, 0, 0)),
                  pl.BlockSpec((nq * blk, dq), this(0)),
                  pl.BlockSpec((blk, dkv), prev(0)),
                  pl.BlockSpec((nq * blk, dkv), this(0)),
                  pl.BlockSpec((blk, dkv), prev(1)),
                  pl.BlockSpec((nq * blk, dkv), this(1))],
        out_specs=pl.BlockSpec((nq * blk, dq), this(0)),
        out_shape=jax.ShapeDtypeStruct((m, dq), BF16),
        compiler_params=_params(("parallel", "arbitrary")),
        name="attention",
    )(sinks * math.log2(math.e), bias, bias, q, kv, kv, kv, kv)


def kernel(x, c, positions, norm_g, w_ada, b_ada, w_ff_in, w_ff_out, s5_w_in, s5_a_re, s5_a_im, s5_b_re, s5_b_im, s5_c_re, s5_c_im, s5_d, s5_log_dt, s5_w_glu, s5_b_glu, s5_w_out, kv_norm_g, w_ada_kv, b_ada_kv, w_kv, attn_w_q, attn_sinks, attn_w_o, final_norm_g):
    nbatch, seq, d = x.shape
    depth = norm_g.shape[0]
    n_s5 = s5_w_in.shape[0]
    m = nbatch * seq
    kvw = w_kv.shape[1] // 2

    c_pad = jnp.zeros((SUBLANES, d), F32).at[:nbatch].set(c)
    mods = _ada(c_pad, w_ada, b_ada)[:, :nbatch]
    mod_kv = _ada(c_pad, w_ada_kv[None], b_ada_kv[None])[0, :nbatch]
    rope = _rope_tables(positions)

    per_batch = dict(rows_per_batch=seq)
    mod = mods.reshape(depth, nbatch, N_SUBLAYERS, 3, d)
    shift, scale, gate = mod[:, :, :, 0], mod[:, :, :, 1], mod[:, :, :, 2]

    def gs_of(layer, sub):
        return norm_g[layer, sub][None, :] * (1.0 + scale[layer, :, sub])

    def shift_rows(sh):
        return jnp.zeros((SUBLANES, d), F32).at[:nbatch].set(sh).astype(BF16)

    def ffn(mod_in, xs, layer, which, next_gs, final_g=None):
        act, w_out = _mm_swiglu(mod_in, w_ff_in, w_ff_out, (layer, which), **per_batch)
        return _mm_resid(act, w_out, (), xs, gate[layer, :, 2 * which], next_gs,
                         weight=0.5, final_g=final_g, whole_rows=True, **per_batch)

    xs = x.reshape(m, d)
    xg, ssq = _prep(xs, gs_of(0, 0), **per_batch)
    kv = None
    for layer in range(depth):
        xs, (xg,), ssq = ffn((xg, ssq, shift_rows(shift[layer, :, 0])), xs, layer, 0, [gs_of(layer, 1)])

        mix_in = (xg, ssq, shift_rows(shift[layer, :, 1]))
        if layer < n_s5:
            i = layer
            v = _mm_plain(mix_in, s5_w_in, (i,), out_dtype=F32, **per_batch)
            bblk, cblk, a = _s5_discretise(s5_a_re[i], s5_a_im[i], s5_b_re[i], s5_b_im[i],
                                           s5_c_re[i], s5_c_im[i], s5_log_dt[i], nbatch)
            y = _s5_core(v.reshape(nbatch, seq, d), bblk, cblk, a, s5_d[i].reshape(-1)).reshape(m, d)
            mixed = _mm_glu(y, s5_w_glu, s5_b_glu[i], (i,))
            w_mix_out, lead = s5_w_out, (i,)
        else:
            j = layer - n_s5
            q = _mm_rope(mix_in, attn_w_q, (j,), rope, out_dtype=BF16,
                         out_scale=HEAD_DIM ** -0.5 * math.log2(math.e), **per_batch)
            mixed = _attention(q, kv, attn_sinks[j], nbatch=nbatch)
            w_mix_out, lead = attn_w_o, (j,)
        xs, (xg,), ssq = _mm_resid(mixed, w_mix_out, lead, xs, gate[layer, :, 1], [gs_of(layer, 2)],
                                   weight=1.0, bm=1024, sub=2, **per_batch)

        next_gs = [gs_of(layer + 1, 0)] if layer + 1 < depth else []
        if layer == n_s5 - 1:
            next_gs.append(kv_norm_g[None, :] * (1.0 + mod_kv[:, d:]))
        fuse_final = layer + 1 == depth and not next_gs
        xs, xgs, ssq = ffn((xg, ssq, shift_rows(shift[layer, :, 2])), xs, layer, 1, next_gs,
                           final_norm_g if fuse_final else None)
        if layer == n_s5 - 1:
            kv_in = (xgs[-1], ssq, shift_rows(mod_kv[:, :d]))
            kv = _mm_rope(kv_in, w_kv, (), rope, out_scale=1.0, out_dtype=F32, rope_cols=kvw,
                          **per_batch)
        if layer + 1 < depth:
            xg = xgs[0]

    out = xs if fuse_final else _final_norm(xs, final_norm_g)
    return out.reshape(nbatch, seq, d)
```
